```python
import math
import jax, jax.numpy as jnp
from jax import lax
import numpy as np

D_MODEL = 1024
BATCH = 4
SEQ = 4096
DEPTH = 1

N_META = 16
D_FF = 2816
D_CONV = 512
CONV_WIDTH = 31
D_SSM = 512
SSM_GROUP = 16
N_SSM_GROUPS = D_SSM // SSM_GROUP
SSM_STATE = 64
N_BRANCHES = 2
D_IN = 2 * D_CONV + D_SSM + N_BRANCHES * D_MODEL
DT_MIN = 1e-3
DT_MAX = 1e-1
EPS = 1e-6

kernel_name = "hybrid_meta_conformer_conv_s5_gated_macaron"


def rms_norm(x, g):
    xf = x.astype(jnp.float32)
    y = xf * lax.rsqrt(jnp.mean(xf * xf, axis=-1, keepdims=True) + EPS)
    return (y * g.astype(jnp.float32)).astype(x.dtype)


def swiglu_ffn(x, w1, w3, w2):
    return (jax.nn.silu(x @ w1) * (x @ w3)) @ w2


def conformer_conv_branch(a, dw, dw_b, ln_g, ln_b, w_proj):
    v, g = jnp.split(a, 2, axis=-1)
    z = v * jax.nn.sigmoid(g)
    z = lax.conv_general_dilated(
        z, dw[:, None, :].astype(z.dtype), window_strides=(1,),
        padding=((CONV_WIDTH - 1, 0),),
        dimension_numbers=("NWC", "WIO", "NWC"),
        feature_group_count=D_CONV) + dw_b
    zf = z.astype(jnp.float32)
    mu = jnp.mean(zf, axis=-1, keepdims=True)
    var = jnp.mean(jnp.square(zf - mu), axis=-1, keepdims=True)
    zf = (zf - mu) * lax.rsqrt(var + EPS) * ln_g.astype(jnp.float32) + ln_b.astype(jnp.float32)
    z = jax.nn.silu(zf).astype(a.dtype)
    return z @ w_proj


def s5_branch(u, lam_re, lam_im, log_dt, b_re, b_im, c_re, c_im, d_skip, w_v, w_g):
    bsz, seq_len, _ = u.shape
    uf = u.astype(jnp.float32).reshape(bsz, seq_len, N_SSM_GROUPS, SSM_GROUP)
    lam = lax.complex(lam_re.astype(jnp.float32), lam_im.astype(jnp.float32))
    dt = jnp.exp(log_dt.astype(jnp.float32))[:, None]
    lam_bar = jnp.exp(lam * dt)
    b = lax.complex(b_re.astype(jnp.float32), b_im.astype(jnp.float32))
    b_bar = ((lam_bar - 1.0) / lam)[..., None] * b
    bu = jnp.einsum("blgh,gph->blgp", uf.astype(jnp.complex64), b_bar)
    a = jnp.broadcast_to(lam_bar, bu.shape)

    def combine(e1, e2):
        a1, s1 = e1
        a2, s2 = e2
        return a1 * a2, a2 * s1 + s2

    _, states = lax.associative_scan(combine, (a, bu), axis=1)
    c = lax.complex(c_re.astype(jnp.float32), c_im.astype(jnp.float32))
    y = jnp.einsum("blgp,ghp->blgh", states, c).real \
        + d_skip.astype(jnp.float32).reshape(N_SSM_GROUPS, SSM_GROUP) * uf
    y = jax.nn.gelu(y.reshape(bsz, seq_len, D_SSM)).astype(u.dtype)
    return (y @ w_v) * jax.nn.sigmoid(y @ w_g)


def setup_inputs(seed: int = 0) -> dict:
    key = jax.random.key(seed)
    ks = jax.random.split(key, 32)
    f32 = jnp.float32
    nrm = lambda k, shape, scale: jax.random.normal(k, shape, f32) * scale
    gain = lambda k, shape: 1.0 + 0.02 * jax.random.normal(k, shape, f32)
    G, P, H = N_SSM_GROUPS, SSM_STATE, SSM_GROUP
    n_idx = jnp.arange(P, dtype=f32)
    return {
        "x": nrm(ks[0], (BATCH, SEQ, D_MODEL), 1.0),
        "meta_tokens": nrm(ks[1], (N_META, D_MODEL), 1.0),
        "ffn1_norm": gain(ks[2], (DEPTH, D_MODEL)),
        "ffn1_w1": nrm(ks[3], (DEPTH, D_MODEL, D_FF), D_MODEL ** -0.5),
        "ffn1_w3": nrm(ks[4], (DEPTH, D_MODEL, D_FF), D_MODEL ** -0.5),
        "ffn1_w2": nrm(ks[5], (DEPTH, D_FF, D_MODEL), D_FF ** -0.5),
        "mix_norm": gain(ks[6], (DEPTH, D_MODEL)),
        "w_in": nrm(ks[7], (DEPTH, D_MODEL, D_IN), D_MODEL ** -0.5),
        "b_gate": nrm(ks[8], (DEPTH, N_BRANCHES * D_MODEL), 0.01),
        "conv_dw": nrm(ks[9], (DEPTH, CONV_WIDTH, D_CONV), CONV_WIDTH ** -0.5),
        "conv_dw_b": nrm(ks[10], (DEPTH, D_CONV), 0.01),
        "conv_ln_g": gain(ks[11], (DEPTH, D_CONV)),
        "conv_ln_b": nrm(ks[12], (DEPTH, D_CONV), 0.01),
        "conv_proj": nrm(ks[13], (DEPTH, D_CONV, D_MODEL), D_CONV ** -0.5),
        "ssm_lam_re": -0.5 + 0.01 * jax.random.normal(ks[14], (DEPTH, G, P), f32),
        "ssm_lam_im": math.pi * n_idx + 0.01 * jax.random.normal(ks[15], (DEPTH, G, P), f32),
        "ssm_log_dt": jax.random.uniform(ks[16], (DEPTH, G), f32,
                                         minval=math.log(DT_MIN), maxval=math.log(DT_MAX)),
        "ssm_b_re": nrm(ks[17], (DEPTH, G, P, H), (2.0 * H) ** -0.5),
        "ssm_b_im": nrm(ks[18], (DEPTH, G, P, H), (2.0 * H) ** -0.5),
        "ssm_c_re": nrm(ks[19], (DEPTH, G, H, P), (2.0 * P) ** -0.5),
        "ssm_c_im": nrm(ks[20], (DEPTH, G, H, P), (2.0 * P) ** -0.5),
        "ssm_d": nrm(ks[21], (DEPTH, D_SSM), 1.0),
        "ssm_w_v": nrm(ks[22], (DEPTH, D_SSM, D_MODEL), D_SSM ** -0.5),
        "ssm_w_g": nrm(ks[23], (DEPTH, D_SSM, D_MODEL), D_SSM ** -0.5),
        "w_out": nrm(ks[24], (DEPTH, D_MODEL, D_MODEL), D_MODEL ** -0.5),
        "ffn2_norm": gain(ks[25], (DEPTH, D_MODEL)),
        "ffn2_w1": nrm(ks[26], (DEPTH, D_MODEL, D_FF), D_MODEL ** -0.5),
        "ffn2_w3": nrm(ks[27], (DEPTH, D_MODEL, D_FF), D_MODEL ** -0.5),
        "ffn2_w2": nrm(ks[28], (DEPTH, D_FF, D_MODEL), D_FF ** -0.5),
        "final_norm": gain(ks[29], (D_MODEL,)),
    }


def reference(x, meta_tokens, ffn1_norm, ffn1_w1, ffn1_w3, ffn1_w2, mix_norm, w_in, b_gate,
              conv_dw, conv_dw_b, conv_ln_g, conv_ln_b, conv_proj,
              ssm_lam_re, ssm_lam_im, ssm_log_dt, ssm_b_re, ssm_b_im, ssm_c_re, ssm_c_im,
              ssm_d, ssm_w_v, ssm_w_g, w_out, ffn2_norm, ffn2_w1, ffn2_w3, ffn2_w2, final_norm):
    bsz = x.shape[0]
    meta = jnp.broadcast_to(meta_tokens[None].astype(x.dtype), (bsz, N_META, D_MODEL))
    h = jnp.concatenate([meta, x], axis=1)
    for l in range(DEPTH):
        h = h + 0.5 * swiglu_ffn(rms_norm(h, ffn1_norm[l]), ffn1_w1[l], ffn1_w3[l], ffn1_w2[l])
        u = rms_norm(h, mix_norm[l])
        proj = u @ w_in[l]
        conv_in, ssm_in, gate_in = jnp.split(proj, [2 * D_CONV, 2 * D_CONV + D_SSM], axis=-1)
        y_conv = conformer_conv_branch(conv_in, conv_dw[l], conv_dw_b[l], conv_ln_g[l],
                                       conv_ln_b[l], conv_proj[l])
        y_ssm = s5_branch(ssm_in, ssm_lam_re[l], ssm_lam_im[l], ssm_log_dt[l], ssm_b_re[l],
                          ssm_b_im[l], ssm_c_re[l], ssm_c_im[l], ssm_d[l], ssm_w_v[l], ssm_w_g[l])
        g_conv, g_ssm = jnp.split(jax.nn.sigmoid(gate_in + b_gate[l]), 2, axis=-1)
        h = h + (g_conv * y_conv + g_ssm * y_ssm) @ w_out[l]
        h = h + 0.5 * swiglu_ffn(rms_norm(h, ffn2_norm[l]), ffn2_w1[l], ffn2_w3[l], ffn2_w2[l])
    h = rms_norm(h, final_norm)
    return h[:, N_META:]
```

```python
import functools
import math

import jax
import jax.numpy as jnp
from jax.experimental import pallas as pl
from jax.experimental.pallas import tpu as pltpu

D_MODEL = 1024
N_META = 16
D_FF = 2816
D_CONV = 512
CONV_WIDTH = 31
D_SSM = 512
SSM_GROUP = 16
N_SSM_GROUPS = D_SSM // SSM_GROUP
SSM_STATE = 64
EPS = 1e-6

SSM_CHUNK = 64
CHUNK_LANES = SSM_CHUNK * SSM_GROUP
STATE_LANES = 2 * SSM_STATE
CONV_HALO = 32
FF_SPLITS = ((0, 1024), (1024, 2048), (2048, 2816))
ROW_TILE = 512
VMEM_LIMIT = 60 * 1024 * 1024

_bf16 = jnp.bfloat16
_f32 = jnp.float32


def _dot(a, b):
    return jnp.dot(a, b, preferred_element_type=_f32)


def _rms_norm(x, g):
    return x * jax.lax.rsqrt(jnp.mean(x * x, axis=-1, keepdims=True) + EPS) * g


def _swiglu(n, w1_ref, w3_ref, w2_ref):
    acc = None
    for lo, hi in FF_SPLITS:
        a = _dot(n, w1_ref[:, lo:hi])
        b = _dot(n, w3_ref[:, lo:hi])
        f = (a * jax.nn.sigmoid(a) * b).astype(_bf16)
        part = _dot(f, w2_ref[lo:hi, :])
        acc = part if acc is None else acc + part
    return acc


def _front_kernel(x_ref, g1_ref, w1_ref, w3_ref, w2_ref, g2_ref, win_ref, bg_ref,
                  h1_ref, z_ref, u_ref, gate_ref):
    x = x_ref[...]
    n = _rms_norm(x, g1_ref[...]).astype(_bf16)
    h1 = x + 0.5 * _swiglu(n, w1_ref, w3_ref, w2_ref)
    h1_ref[...] = h1
    u = _rms_norm(h1, g2_ref[...]).astype(_bf16)
    v = _dot(u, win_ref[:, 0:D_CONV])
    g = _dot(u, win_ref[:, D_CONV:2 * D_CONV])
    z_ref[...] = (v * jax.nn.sigmoid(g)).astype(_bf16)
    u_ref[...] = _dot(u, win_ref[:, 2 * D_CONV:2 * D_CONV + D_SSM]).astype(_bf16)
    gi = _dot(u, win_ref[:, 2 * D_CONV + D_SSM:]) + bg_ref[...]
    gate_ref[...] = jax.nn.sigmoid(gi).astype(_bf16)


def _resident(shape):
    return pl.BlockSpec(shape, lambda *_: (0,) * len(shape), pipeline_mode=pl.Buffered(1))


def _front(x2d, g1, w1, w3, w2, g2, win, bg, tile):
    rows = x2d.shape[0]
    d_in = win.shape[1]
    n_gate = d_in - 2 * D_CONV - D_SSM
    row = lambda w: pl.BlockSpec((tile, w), lambda i: (i, 0))
    return pl.pallas_call(
        _front_kernel,
        grid=(rows // tile,),
        in_specs=[row(D_MODEL), _resident((1, D_MODEL)), _resident((D_MODEL, D_FF)),
                  _resident((D_MODEL, D_FF)), _resident((D_FF, D_MODEL)),
                  _resident((1, D_MODEL)), _resident((D_MODEL, d_in)), _resident((1, n_gate))],
        out_specs=[row(D_MODEL), row(D_CONV), row(D_SSM), row(n_gate)],
        out_shape=[jax.ShapeDtypeStruct((rows, D_MODEL), _f32),
                   jax.ShapeDtypeStruct((rows, D_CONV), _bf16),
                   jax.ShapeDtypeStruct((rows, D_SSM), _bf16),
                   jax.ShapeDtypeStruct((rows, n_gate), _bf16)],
        compiler_params=pltpu.CompilerParams(dimension_semantics=("arbitrary",),
                                             vmem_limit_bytes=VMEM_LIMIT),
        name="front",
    )(x2d, g1, w1, w3, w2, g2, win, bg)


def _ssm_kernel(u_ref, um_ref, toep_ref, ws_ref, wc_ref, a1_ref, a2_ref, y_ref, *, chunks_per_seq):
    u = u_ref[0]
    rows = u.shape[0]
    ws = ws_ref[0]
    s = _dot(u, ws)
    sm = _dot(um_ref[0], ws)[0:1, :]
    a1 = a1_ref[0]
    a2 = a2_ref[0]

    def cmul(v, lvl):
        return a1[lvl:lvl + 1, :] * v + a2[lvl:lvl + 1, :] * pltpu.roll(v, SSM_STATE, axis=1)

    cidx = jax.lax.broadcasted_iota(jnp.int32, (rows, STATE_LANES), 0) % chunks_per_seq
    first = cidx == 0
    st = s + jnp.where(first, cmul(jnp.broadcast_to(sm, s.shape), 0), 0.0)
    shift = 1
    lvl = 0
    while shift < chunks_per_seq:
        prev = jnp.where(cidx >= shift, pltpu.roll(st, shift, axis=0), 0.0)
        st = st + cmul(prev, lvl)
        shift *= 2
        lvl += 1
    st_in = jnp.where(first, sm, pltpu.roll(st, 1, axis=0))
    y = _dot(u, toep_ref[0]) + _dot(st_in.astype(_bf16), wc_ref[0])
    y_ref[0] = jax.nn.gelu(y).astype(_bf16)


def _ssm(u_t, um_t, toep, ws, wc, a1, a2, chunks_per_seq):
    groups, rows, _ = u_t.shape
    levels = a1.shape[1]
    per_group = lambda r, c: pl.BlockSpec((1, r, c), lambda g: (g, 0, 0))
    return pl.pallas_call(
        functools.partial(_ssm_kernel, chunks_per_seq=chunks_per_seq),
        grid=(groups,),
        in_specs=[per_group(rows, CHUNK_LANES), per_group(16, CHUNK_LANES),
                  per_group(CHUNK_LANES, CHUNK_LANES), per_group(CHUNK_LANES, STATE_LANES),
                  per_group(STATE_LANES, CHUNK_LANES), per_group(levels, STATE_LANES),
                  per_group(levels, STATE_LANES)],
        out_specs=per_group(rows, CHUNK_LANES),
        out_shape=jax.ShapeDtypeStruct((groups, rows, CHUNK_LANES), _bf16),
        compiler_params=pltpu.CompilerParams(dimension_semantics=("arbitrary",),
                                             vmem_limit_bytes=VMEM_LIMIT),
        name="ssm",
    )(u_t, um_t, toep, ws, wc, a1, a2)


def _back_kernel(h1_ref, z_ref, y_ref, gate_ref, zmeta_ref, dw_ref, dwb_ref, lng_ref, lnb_ref,
                 wcp_ref, wv_ref, wg_ref, wout_ref, g3_ref, w1_ref, w3_ref, w2_ref, gf_ref,
                 out_ref, zpad_ref):
    tile = z_ref.shape[0]

    @pl.when(pl.program_id(1) == 0)
    def _():
        zpad_ref[0:CONV_HALO, :] = zmeta_ref[...]

    zpad_ref[CONV_HALO:, :] = z_ref[...].astype(_f32)
    base = CONV_HALO - (CONV_WIDTH - 1)
    conv = None
    for k in range(CONV_WIDTH):
        term = dw_ref[k:k + 1, :] * zpad_ref[base + k:base + k + tile, :]
        conv = term if conv is None else conv + term
    conv = conv + dwb_ref[...]
    zpad_ref[0:CONV_HALO, :] = zpad_ref[tile:tile + CONV_HALO, :]

    mu = jnp.mean(conv, axis=-1, keepdims=True)
    cen = conv - mu
    var = jnp.mean(cen * cen, axis=-1, keepdims=True)
    zn = cen * jax.lax.rsqrt(var + EPS) * lng_ref[...] + lnb_ref[...]
    y_conv = _dot((zn * jax.nn.sigmoid(zn)).astype(_bf16), wcp_ref[...])

    ys = y_ref[...]
    y_ssm = _dot(ys, wv_ref[...]) * jax.nn.sigmoid(_dot(ys, wg_ref[...]))

    mix = (gate_ref[:, 0:D_MODEL].astype(_f32) * y_conv
           + gate_ref[:, D_MODEL:].astype(_f32) * y_ssm).astype(_bf16)
    h2 = h1_ref[...] + _dot(mix, wout_ref[...])
    n = _rms_norm(h2, g3_ref[...]).astype(_bf16)
    h3 = h2 + 0.5 * _swiglu(n, w1_ref, w3_ref, w2_ref)
    out_ref[...] = _rms_norm(h3, gf_ref[...])


def _back(h1, z, y, gate, zmeta, dw, dwb, lng, lnb, wcp, wv, wg, wout, g3, w1, w3, w2, gf,
          batch, seq, tile):
    tiles = seq // tile
    row = lambda w: pl.BlockSpec((tile, w), lambda b, i: (b * tiles + i, 0))
    return pl.pallas_call(
        _back_kernel,
        grid=(batch, tiles),
        in_specs=[row(D_MODEL), row(D_CONV), row(D_SSM), row(2 * D_MODEL),
                  _resident((CONV_HALO, D_CONV)), _resident((CONV_WIDTH, D_CONV)),
                  _resident((1, D_CONV)), _resident((1, D_CONV)), _resident((1, D_CONV)),
                  _resident((D_CONV, D_MODEL)), _resident((D_SSM, D_MODEL)),
                  _resident((D_SSM, D_MODEL)), _resident((D_MODEL, D_MODEL)),
                  _resident((1, D_MODEL)), _resident((D_MODEL, D_FF)), _resident((D_MODEL, D_FF)),
                  _resident((D_FF, D_MODEL)), _resident((1, D_MODEL))],
        out_specs=row(D_MODEL),
        out_shape=jax.ShapeDtypeStruct((batch * seq, D_MODEL), _f32),
        scratch_shapes=[pltpu.VMEM((CONV_HALO + tile, D_CONV), _f32)],
        compiler_params=pltpu.CompilerParams(dimension_semantics=("arbitrary", "arbitrary"),
                                             vmem_limit_bytes=VMEM_LIMIT),
        name="back",
    )(h1, z, y, gate, zmeta, dw, dwb, lng, lnb, wcp, wv, wg, wout, g3, w1, w3, w2, gf)


def _ssm_operators(lam_re, lam_im, log_dt, b_re, b_im, c_re, c_im, d_skip, levels):
    T, H, P, G = SSM_CHUNK, SSM_GROUP, SSM_STATE, N_SSM_GROUPS
    lam = jax.lax.complex(lam_re, lam_im)
    dt = jnp.exp(log_dt)[:, None]
    lam_bar = jnp.exp(lam * dt)
    b_bar = ((lam_bar - 1.0) / lam)[..., None] * jax.lax.complex(b_re, b_im)
    c = jax.lax.complex(c_re, c_im)
    k = jnp.arange(T + 1, dtype=_f32)
    pw = jnp.exp((lam * dt)[..., None] * k)
    kern = jnp.einsum("ghp,gpk,gpi->gkhi", c, pw[..., :T], b_bar).real
    kern = kern.at[:, 0].add(jax.vmap(jnp.diag)(d_skip.reshape(G, H)))
    lag = jnp.arange(T)[None, :] - jnp.arange(T)[:, None]
    toep = jnp.where((lag >= 0)[None, :, :, None, None], kern[:, jnp.clip(lag, 0)], 0.0)
    toep = toep.transpose(0, 1, 4, 2, 3).reshape(G, T * H, T * H)
    wsc = pw[..., T - 1 - jnp.arange(T)][..., None] * b_bar[:, :, None, :]
    wsc = wsc.transpose(0, 2, 3, 1).reshape(G, T * H, P)
    ws = jnp.concatenate([wsc.real, wsc.imag], axis=-1)
    wcc = c.transpose(0, 2, 1)[:, :, None, :] * pw[..., 1:T + 1][..., None]
    wcc = wcc.reshape(G, P, T * H)
    wc = jnp.concatenate([wcc.real, -wcc.imag], axis=1)
    step = jnp.exp((lam * dt)[:, None, :] * (T * 2.0 ** jnp.arange(levels, dtype=_f32))[None, :, None])
    a1 = jnp.concatenate([step.real, step.real], axis=-1)
    a2 = jnp.concatenate([-step.imag, step.imag], axis=-1)
    return toep.astype(_bf16), ws.astype(_bf16), wc.astype(_bf16), a1, a2


def kernel(x, meta_tokens, ffn1_norm, ffn1_w1, ffn1_w3, ffn1_w2, mix_norm, w_in, b_gate, conv_dw, conv_dw_b, conv_ln_g, conv_ln_b, conv_proj, ssm_lam_re, ssm_lam_im, ssm_log_dt, ssm_b_re, ssm_b_im, ssm_c_re, ssm_c_im, ssm_d, ssm_w_v, ssm_w_g, w_out, ffn2_norm, ffn2_w1, ffn2_w3, ffn2_w2, final_norm):
    batch, seq, _ = x.shape
    assert ffn1_norm.shape[0] == 1 and seq % (ROW_TILE) == 0 and seq % SSM_CHUNK == 0
    T, H, G = SSM_CHUNK, SSM_GROUP, N_SSM_GROUPS
    chunks = seq // T
    levels = max(1, (chunks - 1).bit_length())
    row = lambda v: v.reshape(1, -1)
    bf = lambda w: w[0].astype(_bf16)

    front_w = (row(ffn1_norm[0]), bf(ffn1_w1), bf(ffn1_w3), bf(ffn1_w2), row(mix_norm[0]),
               bf(w_in), row(b_gate[0]))
    h1, z, u, gate = _front(x.reshape(batch * seq, D_MODEL), *front_w, tile=ROW_TILE)
    _, z_m, u_m, _ = _front(meta_tokens, *front_w, tile=N_META)

    toep, ws, wc, a1, a2 = _ssm_operators(ssm_lam_re[0], ssm_lam_im[0], ssm_log_dt[0], ssm_b_re[0],
                                          ssm_b_im[0], ssm_c_re[0], ssm_c_im[0], ssm_d[0], levels)
    u_t = u.reshape(batch * chunks, T, G, H).transpose(2, 0, 1, 3).reshape(G, batch * chunks, T * H)
    um_t = jnp.zeros((T, G, H), _bf16).at[T - N_META:].set(u_m.reshape(N_META, G, H))
    um_t = jnp.zeros((G, 16, T * H), _bf16).at[:, 0].set(um_t.transpose(1, 0, 2).reshape(G, T * H))
    y_t = _ssm(u_t, um_t, toep, ws, wc, a1, a2, chunks)
    y = y_t.reshape(G, batch * chunks, T, H).transpose(1, 2, 0, 3).reshape(batch * seq, D_SSM)

    z_halo = jnp.zeros((CONV_HALO, D_CONV), _f32).at[CONV_HALO - N_META:].set(z_m.astype(_f32))
    out = _back(h1, z, y, gate, z_halo, conv_dw[0], row(conv_dw_b[0]), row(conv_ln_g[0]),
                row(conv_ln_b[0]), bf(conv_proj), bf(ssm_w_v), bf(ssm_w_g), bf(w_out),
                row(ffn2_norm[0]), bf(ffn2_w1), bf(ffn2_w3), bf(ffn2_w2), row(final_norm),
                batch, seq, ROW_TILE)
    return out.reshape(batch, seq, D_MODEL)
```

```python
import functools

import jax
import jax.numpy as jnp
from jax.experimental import pallas as pl
from jax.experimental.pallas import tpu as pltpu

D_MODEL = 1024
N_META = 16
D_FF = 2816
D_CONV = 512
CONV_WIDTH = 31
D_SSM = 512
SSM_GROUP = 16
N_SSM_GROUPS = D_SSM // SSM_GROUP
SSM_STATE = 64
EPS = 1e-6

LANES = 128
SUBLANES = 8
SSM_CHUNK = 64
CHUNK_LANES = SSM_CHUNK * SSM_GROUP
STATE_LANES = 2 * SSM_STATE
GROUPS_PER_VREG = LANES // SSM_GROUP
CONV_HALO = 32
FF_SPLITS = ((0, 1024), (1024, 2048), (2048, 2816))
ROW_TILE = 512
VMEM_LIMIT = 60 * 1024 * 1024

assert STATE_LANES == LANES and GROUPS_PER_VREG == SUBLANES

_bf16 = jnp.bfloat16
_f32 = jnp.float32


def _dot(a, b):
    return jnp.dot(a, b, preferred_element_type=_f32)


def _dot_nt(a, b, precision=None):
    return jax.lax.dot_general(a, b, (((1,), (1,)), ((), ())), precision=precision,
                               preferred_element_type=_f32)


def _rms_norm(x, g):
    return x * jax.lax.rsqrt(jnp.mean(x * x, axis=-1, keepdims=True) + EPS) * g


def _swiglu(n, w1_ref, w3_ref, w2_ref):
    acc = None
    for lo, hi in FF_SPLITS:
        a = _dot(n, w1_ref[:, lo:hi])
        b = _dot(n, w3_ref[:, lo:hi])
        f = (a * jax.nn.sigmoid(a) * b).astype(_bf16)
        part = _dot(f, w2_ref[lo:hi, :])
        acc = part if acc is None else acc + part
    return acc


def _block_transpose8(v):
    blk = jax.lax.broadcasted_iota(jnp.int32, v[0].shape, 1) // SSM_GROUP
    for d in (4, 2, 1):
        hi = (blk & d) != 0
        shift = d * SSM_GROUP
        new = list(v)
        for i in range(GROUPS_PER_VREG):
            if i & d:
                continue
            new[i] = jnp.where(hi, pltpu.roll(v[i + d], shift, axis=1), v[i])
            new[i + d] = jnp.where(hi, v[i + d], pltpu.roll(v[i], LANES - shift, axis=1))
        v = new
    return v


def _front_kernel(x_ref, g1_ref, w1_ref, w3_ref, w2_ref, g2_ref, win_ref, bg_ref,
                  h1_ref, z_ref, u_ref, gate_ref, *scratch, group_major):
    x = x_ref[...]
    n = _rms_norm(x, g1_ref[...]).astype(_bf16)
    h1 = x + 0.5 * _swiglu(n, w1_ref, w3_ref, w2_ref)
    h1_ref[...] = h1
    u = _rms_norm(h1, g2_ref[...]).astype(_bf16)
    v = _dot(u, win_ref[:, 0:D_CONV])
    g = _dot(u, win_ref[:, D_CONV:2 * D_CONV])
    z_ref[...] = (v * jax.nn.sigmoid(g)).astype(_bf16)
    gi = _dot(u, win_ref[:, 2 * D_CONV + D_SSM:]) + bg_ref[...]
    gate_ref[...] = jax.nn.sigmoid(gi).astype(_bf16)
    us = _dot(u, win_ref[:, 2 * D_CONV:2 * D_CONV + D_SSM])
    if not group_major:
        u_ref[...] = us
        return
    us_ref, = scratch
    n_chunks = us.shape[0] // SSM_CHUNK
    for j in range(D_SSM // LANES):
        us_ref[j] = us[:, j * LANES:(j + 1) * LANES]
        for q in range(SSM_CHUNK // SUBLANES):
            rows = [us_ref[j, pl.ds(SUBLANES * q + r, n_chunks, stride=SSM_CHUNK), :]
                    for r in range(SUBLANES)]
            for a, blk in enumerate(_block_transpose8(rows)):
                u_ref[GROUPS_PER_VREG * j + a, :, q * LANES:(q + 1) * LANES] = blk


def _resident(shape):
    return pl.BlockSpec(shape, lambda *_: (0,) * len(shape), pipeline_mode=pl.Buffered(1))


def _front(x2d, g1, w1, w3, w2, g2, win, bg, tile, group_major):
    rows = x2d.shape[0]
    d_in = win.shape[1]
    n_gate = d_in - 2 * D_CONV - D_SSM
    row = lambda w: pl.BlockSpec((tile, w), lambda i: (i, 0))
    if group_major:
        u_spec = pl.BlockSpec((N_SSM_GROUPS, tile // SSM_CHUNK, CHUNK_LANES), lambda i: (0, i, 0))
        u_shape = jax.ShapeDtypeStruct((N_SSM_GROUPS, rows // SSM_CHUNK, CHUNK_LANES), _f32)
        scratch = [pltpu.VMEM((D_SSM // LANES, tile, LANES), _f32)]
    else:
        u_spec, u_shape, scratch = row(D_SSM), jax.ShapeDtypeStruct((rows, D_SSM), _f32), []
    return pl.pallas_call(
        functools.partial(_front_kernel, group_major=group_major),
        grid=(rows // tile,),
        in_specs=[row(D_MODEL), _resident((1, D_MODEL)), _resident((D_MODEL, D_FF)),
                  _resident((D_MODEL, D_FF)), _resident((D_FF, D_MODEL)),
                  _resident((1, D_MODEL)), _resident((D_MODEL, d_in)), _resident((1, n_gate))],
        out_specs=[row(D_MODEL), row(D_CONV), u_spec, row(n_gate)],
        out_shape=[jax.ShapeDtypeStruct((rows, D_MODEL), _f32),
                   jax.ShapeDtypeStruct((rows, D_CONV), _bf16),
                   u_shape,
                   jax.ShapeDtypeStruct((rows, n_gate), _bf16)],
        scratch_shapes=scratch,
        compiler_params=pltpu.CompilerParams(dimension_semantics=("arbitrary",),
                                             vmem_limit_bytes=VMEM_LIMIT),
        name="front" if group_major else "front_meta",
    )(x2d, g1, w1, w3, w2, g2, win, bg)


def _ssm_kernel(u_ref, um_ref, lam_ref, ldt_ref, bt_ref, c_ref, d_ref, y_ref, toep_ref,
                *, chunks_per_seq, levels):
    T, H, P = SSM_CHUNK, SSM_GROUP, SSM_STATE
    lane = jax.lax.broadcasted_iota(jnp.int32, (1, STATE_LANES), 1)
    lo = lane < P
    sgn = jnp.where(lo, -1.0, 1.0)
    lam_re = lam_ref[0, 0:1, :]
    lam_im = lam_ref[0, 1:2, :]
    dt = jnp.exp(ldt_ref[0])
    ar, ai = lam_re * dt, lam_im * dt

    def powers(k):
        mag, ang = jnp.exp(ar * k), ai * k
        cs, sn = jnp.cos(ang), jnp.sin(ang)
        return mag * jnp.where(lo, cs, sn), mag * jnp.where(lo, sn, cs)

    def cmul3(m1, m2, pw, pws):
        return (m1[None] * pw[:, None, :] + m2[None] * pws[:, None, :]).reshape(T * H, STATE_LANES)

    one = jnp.ones((1, 1), _f32)
    lb, lbs = powers(one)
    lb_re, lb_im = jnp.where(lo, lb, lbs), jnp.where(lo, lbs, lb)
    den = lam_re * lam_re + lam_im * lam_im
    co_re = ((lb_re - 1.0) * lam_re + lb_im * lam_im) / den
    co_im = (lb_im * lam_re - (lb_re - 1.0) * lam_im) / den
    b_re, b_im = bt_ref[0, 0], bt_ref[0, 1]
    bb_re = co_re * b_re - co_im * b_im
    bb_im = co_re * b_im + co_im * b_re
    c1, c2 = c_ref[0, 0], sgn * c_ref[0, 1]

    t_col = jax.lax.broadcasted_iota(jnp.int32, (T, 1), 0).astype(_f32)
    ws = cmul3(bb_re, sgn * bb_im, *powers(T - 1.0 - t_col)).astype(_bf16)
    wct = (cmul3(c1, c2, *powers(t_col + 1.0)) * (-sgn)).astype(_bf16)
    e = cmul3(c1, c2, *powers(t_col))
    r0 = _dot_nt(jnp.where(lo, bb_re, -bb_im), e, precision=jax.lax.Precision.HIGHEST)
    lane_c = jax.lax.broadcasted_iota(jnp.int32, (H, CHUNK_LANES), 1)
    row_c = jax.lax.broadcasted_iota(jnp.int32, (H, CHUNK_LANES), 0)
    r0 = r0 + jnp.where(lane_c == row_c, d_ref[0], 0.0)
    for r in range(SUBLANES):
        rr = pltpu.roll(r0, r * H, axis=1) if r else r0
        for a in range(T // SUBLANES):
            s = SUBLANES * a + r
            blk = pltpu.roll(rr, a * LANES, axis=1) if a else rr
            toep_ref[s * H:(s + 1) * H, :] = jnp.where(lane_c >= s * H, blk, 0.0).astype(_bf16)

    u = u_ref[0].astype(_bf16)
    rows = u.shape[0]
    s_loc = _dot(u, ws)
    sm = _dot(um_ref[0].astype(_bf16), ws)[0:1, :]

    lvl_col = jax.lax.broadcasted_iota(jnp.int32, (SUBLANES, 1), 0)
    step, step_s = powers(jnp.left_shift(T, lvl_col).astype(_f32))
    a1 = jnp.where(lo, step, step_s)
    a2 = sgn * jnp.where(lo, step_s, step)

    def cmul(v, lvl):
        return a1[lvl:lvl + 1, :] * v + a2[lvl:lvl + 1, :] * pltpu.roll(v, P, axis=1)

    cidx = jax.lax.broadcasted_iota(jnp.int32, (rows, STATE_LANES), 0) % chunks_per_seq
    first = cidx == 0
    st = s_loc + jnp.where(first, cmul(jnp.broadcast_to(sm, s_loc.shape), 0), 0.0)
    for lvl in range(levels):
        shift = 1 << lvl
        prev = jnp.where(cidx >= shift, pltpu.roll(st, shift, axis=0), 0.0)
        st = st + cmul(prev, lvl)
    st_in = jnp.where(first, sm, pltpu.roll(st, 1, axis=0))
    y = _dot(u, toep_ref[...]) + _dot_nt(st_in.astype(_bf16), wct)
    y_ref[0] = jax.nn.gelu(y)


def _ssm(u_t, um_t, lam2, ldt, bt2, c2, d_col, chunks_per_seq):
    groups, rows, _ = u_t.shape
    levels = max(1, (chunks_per_seq - 1).bit_length())
    assert levels <= SUBLANES
    per_group = lambda *s: pl.BlockSpec((1,) + s, lambda g: (g,) + (0,) * len(s))
    return pl.pallas_call(
        functools.partial(_ssm_kernel, chunks_per_seq=chunks_per_seq, levels=levels),
        grid=(groups,),
        in_specs=[per_group(rows, CHUNK_LANES), per_group(um_t.shape[1], CHUNK_LANES),
                  per_group(2, STATE_LANES), per_group(1, 1),
                  per_group(2, SSM_GROUP, STATE_LANES), per_group(2, SSM_GROUP, STATE_LANES),
                  per_group(SSM_GROUP, 1)],
        out_specs=per_group(rows, CHUNK_LANES),
        out_shape=jax.ShapeDtypeStruct((groups, rows, CHUNK_LANES), _f32),
        scratch_shapes=[pltpu.VMEM((CHUNK_LANES, CHUNK_LANES), _bf16)],
        compiler_params=pltpu.CompilerParams(dimension_semantics=("arbitrary",),
                                             vmem_limit_bytes=VMEM_LIMIT),
        name="ssm",
    )(u_t, um_t, lam2, ldt, bt2, c2, d_col)


def _back_kernel(h1_ref, z_ref, yt_ref, gate_ref, zmeta_ref, dw_ref, dwb_ref, lng_ref, lnb_ref,
                 wcp_ref, wv_ref, wg_ref, wout_ref, g3_ref, w1_ref, w3_ref, w2_ref, gf_ref,
                 out_ref, zpad_ref, ys_ref):
    tile = z_ref.shape[0]

    @pl.when(pl.program_id(1) == 0)
    def _():
        zpad_ref[0:CONV_HALO, :] = zmeta_ref[...]

    zpad_ref[CONV_HALO:, :] = z_ref[...].astype(_f32)
    base = CONV_HALO - (CONV_WIDTH - 1)
    conv = None
    for k in range(CONV_WIDTH):
        term = dw_ref[k:k + 1, :] * zpad_ref[base + k:base + k + tile, :]
        conv = term if conv is None else conv + term
    conv = conv + dwb_ref[...]
    zpad_ref[0:CONV_HALO, :] = zpad_ref[tile:tile + CONV_HALO, :]

    mu = jnp.mean(conv, axis=-1, keepdims=True)
    cen = conv - mu
    var = jnp.mean(cen * cen, axis=-1, keepdims=True)
    zn = cen * jax.lax.rsqrt(var + EPS) * lng_ref[...] + lnb_ref[...]
    y_conv = _dot((zn * jax.nn.sigmoid(zn)).astype(_bf16), wcp_ref[...])

    n_chunks = tile // SSM_CHUNK
    for j in range(D_SSM // LANES):
        for q in range(SSM_CHUNK // SUBLANES):
            blks = [yt_ref[GROUPS_PER_VREG * j + a, :, q * LANES:(q + 1) * LANES]
                    for a in range(GROUPS_PER_VREG)]
            for r, rows in enumerate(_block_transpose8(blks)):
                ys_ref[j, pl.ds(SUBLANES * q + r, n_chunks, stride=SSM_CHUNK), :] = rows
    ys = jnp.concatenate([ys_ref[j] for j in range(D_SSM // LANES)], axis=1).astype(_bf16)
    y_ssm = _dot(ys, wv_ref[...]) * jax.nn.sigmoid(_dot(ys, wg_ref[...]))

    mix = (gate_ref[:, 0:D_MODEL].astype(_f32) * y_conv
           + gate_ref[:, D_MODEL:].astype(_f32) * y_ssm).astype(_bf16)
    h2 = h1_ref[...] + _dot(mix, wout_ref[...])
    n = _rms_norm(h2, g3_ref[...]).astype(_bf16)
    h3 = h2 + 0.5 * _swiglu(n, w1_ref, w3_ref, w2_ref)
    out_ref[...] = _rms_norm(h3, gf_ref[...])


def _back(h1, z, y_t, gate, zmeta, dw, dwb, lng, lnb, wcp, wv, wg, wout, g3, w1, w3, w2, gf,
          batch, seq, tile):
    tiles = seq // tile
    row = lambda w: pl.BlockSpec((tile, w), lambda b, i: (b * tiles + i, 0))
    yt_spec = pl.BlockSpec((N_SSM_GROUPS, tile // SSM_CHUNK, CHUNK_LANES),
                           lambda b, i: (0, b * tiles + i, 0))
    return pl.pallas_call(
        _back_kernel,
        grid=(batch, tiles),
        in_specs=[row(D_MODEL), row(D_CONV), yt_spec, row(2 * D_MODEL),
                  _resident((CONV_HALO, D_CONV)), _resident((CONV_WIDTH, D_CONV)),
                  _resident((1, D_CONV)), _resident((1, D_CONV)), _resident((1, D_CONV)),
                  _resident((D_CONV, D_MODEL)), _resident((D_SSM, D_MODEL)),
                  _resident((D_SSM, D_MODEL)), _resident((D_MODEL, D_MODEL)),
                  _resident((1, D_MODEL)), _resident((D_MODEL, D_FF)), _resident((D_MODEL, D_FF)),
                  _resident((D_FF, D_MODEL)), _resident((1, D_MODEL))],
        out_specs=row(D_MODEL),
        out_shape=jax.ShapeDtypeStruct((batch * seq, D_MODEL), _f32),
        scratch_shapes=[pltpu.VMEM((CONV_HALO + tile, D_CONV), _f32),
                        pltpu.VMEM((D_SSM // LANES, tile, LANES), _f32)],
        compiler_params=pltpu.CompilerParams(dimension_semantics=("arbitrary", "arbitrary"),
                                             vmem_limit_bytes=VMEM_LIMIT),
        name="back",
    )(h1, z, y_t, gate, zmeta, dw, dwb, lng, lnb, wcp, wv, wg, wout, g3, w1, w3, w2, gf)


def kernel(x, meta_tokens, ffn1_norm, ffn1_w1, ffn1_w3, ffn1_w2, mix_norm, w_in, b_gate, conv_dw, conv_dw_b, conv_ln_g, conv_ln_b, conv_proj, ssm_lam_re, ssm_lam_im, ssm_log_dt, ssm_b_re, ssm_b_im, ssm_c_re, ssm_c_im, ssm_d, ssm_w_v, ssm_w_g, w_out, ffn2_norm, ffn2_w1, ffn2_w3, ffn2_w2, final_norm):
    batch, seq, _ = x.shape
    assert ffn1_norm.shape[0] == 1 and seq % ROW_TILE == 0 and ROW_TILE % (SUBLANES * SSM_CHUNK) == 0
    assert N_META <= SSM_CHUNK and N_META <= CONV_HALO
    T, H, G = SSM_CHUNK, SSM_GROUP, N_SSM_GROUPS
    row = lambda v: v.reshape(1, -1)
    bf = lambda w: w[0].astype(_bf16)
    dup = lambda a: jnp.concatenate([a, a], axis=-1)

    front_w = (row(ffn1_norm[0]), bf(ffn1_w1), bf(ffn1_w3), bf(ffn1_w2), row(mix_norm[0]),
               bf(w_in), row(b_gate[0]))
    h1, z, u_t, gate = _front(x.reshape(batch * seq, D_MODEL), *front_w, tile=ROW_TILE,
                              group_major=True)
    meta_chunk = jnp.zeros((T, D_MODEL), _f32).at[T - N_META:].set(meta_tokens)
    _, z_m, u_m, _ = _front(meta_chunk, *front_w, tile=T, group_major=False)
    um_t = u_m.reshape(T, G, H).transpose(1, 0, 2).reshape(G, 1, T * H)
    um_t = jnp.concatenate([um_t, jnp.zeros((G, 2 * SUBLANES - 1, T * H), _f32)], axis=1)

    lam2 = jnp.stack([dup(ssm_lam_re[0]), dup(ssm_lam_im[0])], axis=1)
    bt2 = jnp.stack([dup(ssm_b_re[0].transpose(0, 2, 1)), dup(ssm_b_im[0].transpose(0, 2, 1))], axis=1)
    c2 = jnp.stack([dup(ssm_c_re[0]), dup(ssm_c_im[0])], axis=1)
    y_t = _ssm(u_t, um_t, lam2, ssm_log_dt[0].reshape(G, 1, 1), bt2, c2,
               ssm_d[0].reshape(G, H, 1), seq // T)

    z_halo = z_m[T - CONV_HALO:].astype(_f32)
    out = _back(h1, z, y_t, gate, z_halo, conv_dw[0], row(conv_dw_b[0]), row(conv_ln_g[0]),
                row(conv_ln_b[0]), bf(conv_proj), bf(ssm_w_v), bf(ssm_w_g), bf(w_out),
                row(ffn2_norm[0]), bf(ffn2_w1), bf(ffn2_w3), bf(ffn2_w2), row(final_norm),
                batch, seq, ROW_TILE)
    return out.reshape(batch, seq, D_MODEL)
```

```python
import functools

import jax
import jax.numpy as jnp
from jax.experimental import pallas as pl
from jax.experimental.pallas import tpu as pltpu

D_MODEL = 1024
N_META = 16
D_FF = 2816
D_CONV = 512
CONV_WIDTH = 31
D_SSM = 512
SSM_GROUP = 16
N_SSM_GROUPS = D_SSM // SSM_GROUP
SSM_STATE = 64
EPS = 1e-6

LANES = 128
SUBLANES = 8
SSM_CHUNK = 64
CHUNK_LANES = SSM_CHUNK * SSM_GROUP
STATE_LANES = 2 * SSM_STATE
GROUPS_PER_VREG = LANES // SSM_GROUP
CONV_HALO = 32
FF_SPLITS = ((0, 1024), (1024, 2048), (2048, 2816))
ROW_TILE = 512
VMEM_LIMIT = 60 * 1024 * 1024

assert STATE_LANES == LANES and GROUPS_PER_VREG == SUBLANES

_bf16 = jnp.bfloat16
_f32 = jnp.float32


def _dot(a, b):
    return jnp.dot(a, b, preferred_element_type=_f32)


def _dot_nt(a, b, precision=None):
    return jax.lax.dot_general(a, b, (((1,), (1,)), ((), ())), precision=precision,
                               preferred_element_type=_f32)


def _rms_norm(x, g):
    return x * jax.lax.rsqrt(jnp.mean(x * x, axis=-1, keepdims=True) + EPS) * g


def _swiglu(n, w1_ref, w3_ref, w2_ref):
    acc = None
    for lo, hi in FF_SPLITS:
        a = _dot(n, w1_ref[:, lo:hi])
        b = _dot(n, w3_ref[:, lo:hi])
        f = (a * jax.nn.sigmoid(a) * b).astype(_bf16)
        part = _dot(f, w2_ref[lo:hi, :])
        acc = part if acc is None else acc + part
    return acc


def _block_transpose8(v):
    blk = jax.lax.broadcasted_iota(jnp.int32, v[0].shape, 1) // SSM_GROUP
    for d in (4, 2, 1):
        hi = (blk & d) != 0
        shift = d * SSM_GROUP
        new = list(v)
        for i in range(GROUPS_PER_VREG):
            if i & d:
                continue
            new[i] = jnp.where(hi, pltpu.roll(v[i + d], shift, axis=1), v[i])
            new[i + d] = jnp.where(hi, v[i + d], pltpu.roll(v[i], LANES - shift, axis=1))
        v = new
    return v


def _front_kernel(x_ref, g1_ref, w1_ref, w3_ref, w2_ref, g2_ref, win_ref, bg_ref,
                  h1_ref, z_ref, u_ref, gate_ref, *scratch, group_major):
    x = x_ref[...]
    n = _rms_norm(x, g1_ref[...]).astype(_bf16)
    h1 = x + 0.5 * _swiglu(n, w1_ref, w3_ref, w2_ref)
    h1_ref[...] = h1
    u = _rms_norm(h1, g2_ref[...]).astype(_bf16)
    v = _dot(u, win_ref[:, 0:D_CONV])
    g = _dot(u, win_ref[:, D_CONV:2 * D_CONV])
    z_ref[...] = (v * jax.nn.sigmoid(g)).astype(_bf16)
    gi = _dot(u, win_ref[:, 2 * D_CONV + D_SSM:]) + bg_ref[...]
    gate_ref[...] = jax.nn.sigmoid(gi).astype(_bf16)
    us = _dot(u, win_ref[:, 2 * D_CONV:2 * D_CONV + D_SSM])
    if not group_major:
        u_ref[...] = us
        return
    us_ref, = scratch
    n_chunks = us.shape[0] // SSM_CHUNK
    for j in range(D_SSM // LANES):
        us_ref[j] = us[:, j * LANES:(j + 1) * LANES]
        for q in range(SSM_CHUNK // SUBLANES):
            rows = [us_ref[j, pl.ds(SUBLANES * q + r, n_chunks, stride=SSM_CHUNK), :]
                    for r in range(SUBLANES)]
            for a, blk in enumerate(_block_transpose8(rows)):
                u_ref[GROUPS_PER_VREG * j + a, :, q * LANES:(q + 1) * LANES] = blk


def _resident(shape):
    return pl.BlockSpec(shape, lambda *_: (0,) * len(shape), pipeline_mode=pl.Buffered(1))


def _front(x2d, g1, w1, w3, w2, g2, win, bg, tile, group_major):
    rows = x2d.shape[0]
    d_in = win.shape[1]
    n_gate = d_in - 2 * D_CONV - D_SSM
    row = lambda w: pl.BlockSpec((tile, w), lambda i: (i, 0))
    if group_major:
        u_spec = pl.BlockSpec((N_SSM_GROUPS, tile // SSM_CHUNK, CHUNK_LANES), lambda i: (0, i, 0))
        u_shape = jax.ShapeDtypeStruct((N_SSM_GROUPS, rows // SSM_CHUNK, CHUNK_LANES), _f32)
        scratch = [pltpu.VMEM((D_SSM // LANES, tile, LANES), _f32)]
    else:
        u_spec, u_shape, scratch = row(D_SSM), jax.ShapeDtypeStruct((rows, D_SSM), _f32), []
    return pl.pallas_call(
        functools.partial(_front_kernel, group_major=group_major),
        grid=(rows // tile,),
        in_specs=[row(D_MODEL), _resident((1, D_MODEL)), _resident((D_MODEL, D_FF)),
                  _resident((D_MODEL, D_FF)), _resident((D_FF, D_MODEL)),
                  _resident((1, D_MODEL)), _resident((D_MODEL, d_in)), _resident((1, n_gate))],
        out_specs=[row(D_MODEL), row(D_CONV), u_spec, row(n_gate)],
        out_shape=[jax.ShapeDtypeStruct((rows, D_MODEL), _f32),
                   jax.ShapeDtypeStruct((rows, D_CONV), _bf16),
                   u_shape,
                   jax.ShapeDtypeStruct((rows, n_gate), _bf16)],
        scratch_shapes=scratch,
        compiler_params=pltpu.CompilerParams(dimension_semantics=("arbitrary",),
                                             vmem_limit_bytes=VMEM_LIMIT),
        name="front" if group_major else "front_meta",
    )(x2d, g1, w1, w3, w2, g2, win, bg)


def _ssm_kernel(u_ref, um_ref, lam_ref, ldt_ref, bt_ref, c_ref, d_ref, y_ref, toep_ref,
                *, chunks_per_seq, levels):
    T, H, P = SSM_CHUNK, SSM_GROUP, SSM_STATE
    lane = jax.lax.broadcasted_iota(jnp.int32, (1, STATE_LANES), 1)
    lo = lane < P
    sgn = jnp.where(lo, -1.0, 1.0)
    lam_re = lam_ref[0, 0:1, :]
    lam_im = lam_ref[0, 1:2, :]
    dt = jnp.exp(ldt_ref[0])
    ar, ai = lam_re * dt, lam_im * dt

    def powers(k):
        mag, ang = jnp.exp(ar * k), ai * k
        cs, sn = jnp.cos(ang), jnp.sin(ang)
        return mag * jnp.where(lo, cs, sn), mag * jnp.where(lo, sn, cs)

    def cmul3(m1, m2, pw, pws):
        return (m1[None] * pw[:, None, :] + m2[None] * pws[:, None, :]).reshape(T * H, STATE_LANES)

    one = jnp.ones((1, 1), _f32)
    lb, lbs = powers(one)
    lb_re, lb_im = jnp.where(lo, lb, lbs), jnp.where(lo, lbs, lb)
    den = lam_re * lam_re + lam_im * lam_im
    co_re = ((lb_re - 1.0) * lam_re + lb_im * lam_im) / den
    co_im = (lb_im * lam_re - (lb_re - 1.0) * lam_im) / den
    b_re, b_im = bt_ref[0, 0], bt_ref[0, 1]
    bb_re = co_re * b_re - co_im * b_im
    bb_im = co_re * b_im + co_im * b_re
    c1, c2 = c_ref[0, 0], sgn * c_ref[0, 1]

    t_col = jax.lax.broadcasted_iota(jnp.int32, (T, 1), 0).astype(_f32)
    ws = cmul3(bb_re, sgn * bb_im, *powers(T - 1.0 - t_col)).astype(_bf16)
    wct = (cmul3(c1, c2, *powers(t_col + 1.0)) * (-sgn)).astype(_bf16)
    e = cmul3(c1, c2, *powers(t_col))
    r0 = _dot_nt(jnp.where(lo, bb_re, -bb_im), e, precision=jax.lax.Precision.HIGHEST)
    lane_c = jax.lax.broadcasted_iota(jnp.int32, (H, CHUNK_LANES), 1)
    row_c = jax.lax.broadcasted_iota(jnp.int32, (H, CHUNK_LANES), 0)
    r0 = r0 + jnp.where(lane_c == row_c, d_ref[0], 0.0)
    for r in range(SUBLANES):
        rr = pltpu.roll(r0, r * H, axis=1) if r else r0
        for a in range(T // SUBLANES):
            s = SUBLANES * a + r
            blk = pltpu.roll(rr, a * LANES, axis=1) if a else rr
            toep_ref[s * H:(s + 1) * H, :] = jnp.where(lane_c >= s * H, blk, 0.0).astype(_bf16)

    u = u_ref[0].astype(_bf16)
    rows = u.shape[0]
    s_loc = _dot(u, ws)
    sm = _dot(um_ref[0].astype(_bf16), ws)[0:1, :]

    lvl_col = jax.lax.broadcasted_iota(jnp.int32, (SUBLANES, 1), 0)
    step, step_s = powers(jnp.left_shift(T, lvl_col).astype(_f32))
    a1 = jnp.where(lo, step, step_s)
    a2 = sgn * jnp.where(lo, step_s, step)

    def cmul(v, lvl):
        return a1[lvl:lvl + 1, :] * v + a2[lvl:lvl + 1, :] * pltpu.roll(v, P, axis=1)

    cidx = jax.lax.broadcasted_iota(jnp.int32, (rows, STATE_LANES), 0) % chunks_per_seq
    first = cidx == 0
    st = s_loc + jnp.where(first, cmul(jnp.broadcast_to(sm, s_loc.shape), 0), 0.0)
    for lvl in range(levels):
        shift = 1 << lvl
        prev = jnp.where(cidx >= shift, pltpu.roll(st, shift, axis=0), 0.0)
        st = st + cmul(prev, lvl)
    st_in = jnp.where(first, sm, pltpu.roll(st, 1, axis=0))
    y = _dot(u, toep_ref[...]) + _dot_nt(st_in.astype(_bf16), wct)
    y_ref[0] = jax.nn.gelu(y)


def _ssm(u_t, um_t, lam2, ldt, bt2, c2, d_col, chunks_per_seq):
    groups, rows, _ = u_t.shape
    levels = max(1, (chunks_per_seq - 1).bit_length())
    assert levels <= SUBLANES
    per_group = lambda *s: pl.BlockSpec((1,) + s, lambda g: (g,) + (0,) * len(s))
    return pl.pallas_call(
        functools.partial(_ssm_kernel, chunks_per_seq=chunks_per_seq, levels=levels),
        grid=(groups,),
        in_specs=[per_group(rows, CHUNK_LANES), per_group(um_t.shape[1], CHUNK_LANES),
                  per_group(2, STATE_LANES), per_group(1, 1),
                  per_group(2, SSM_GROUP, STATE_LANES), per_group(2, SSM_GROUP, STATE_LANES),
                  per_group(SSM_GROUP, 1)],
        out_specs=per_group(rows, CHUNK_LANES),
        out_shape=jax.ShapeDtypeStruct((groups, rows, CHUNK_LANES), _f32),
        scratch_shapes=[pltpu.VMEM((CHUNK_LANES, CHUNK_LANES), _bf16)],
        compiler_params=pltpu.CompilerParams(dimension_semantics=("arbitrary",),
                                             vmem_limit_bytes=VMEM_LIMIT),
        name="ssm",
    )(u_t, um_t, lam2, ldt, bt2, c2, d_col)


def _conv_stage(z_ref, first_of_seq, zmeta_ref, dw_ref, dwb_ref, lng_ref, lnb_ref,
                zpad_ref, shift_ref, zs_ref):
    tile = z_ref.shape[0]
    padded = CONV_HALO + tile
    if first_of_seq is True:
        zpad_ref[0:CONV_HALO, :] = zmeta_ref[...]
    else:
        zpad_ref[0:CONV_HALO, :] = jnp.where(first_of_seq, zmeta_ref[...], zpad_ref[0:CONV_HALO, :])
    zpad_ref[CONV_HALO:, :] = z_ref[...].astype(_f32)
    base = CONV_HALO - (CONV_WIDTH - 1)
    convs = []
    for j in range(D_CONV // LANES):
        cols = slice(j * LANES, (j + 1) * LANES)
        zp = zpad_ref[:, cols]
        for r in range(1, SUBLANES):
            shift_ref[r - 1] = pltpu.roll(zp, padded - r, axis=0)
        conv = None
        for k in range(CONV_WIDTH):
            a, r = divmod(base + k, SUBLANES)
            rows = slice(a * SUBLANES, a * SUBLANES + tile)
            src = zpad_ref[rows, cols] if r == 0 else shift_ref[r - 1, rows, :]
            term = dw_ref[k:k + 1, cols] * src
            conv = term if conv is None else conv + term
        convs.append(conv + dwb_ref[:, cols])
    conv = jnp.concatenate(convs, axis=1)
    zpad_ref[0:CONV_HALO, :] = zpad_ref[tile:tile + CONV_HALO, :]
    mu = jnp.mean(conv, axis=-1, keepdims=True)
    cen = conv - mu
    var = jnp.mean(cen * cen, axis=-1, keepdims=True)
    zn = cen * jax.lax.rsqrt(var + EPS) * lng_ref[...] + lnb_ref[...]
    zs_ref[...] = (zn * jax.nn.sigmoid(zn)).astype(_bf16)


def _token_major_stage(yt_ref, ys_ref):
    n_chunks = yt_ref.shape[1]
    for j in range(D_SSM // LANES):
        for q in range(SSM_CHUNK // SUBLANES):
            blks = [yt_ref[GROUPS_PER_VREG * j + a, :, q * LANES:(q + 1) * LANES]
                    for a in range(GROUPS_PER_VREG)]
            for r, rows in enumerate(_block_transpose8(blks)):
                ys_ref[j, pl.ds(SUBLANES * q + r, n_chunks, stride=SSM_CHUNK), :] = rows


def _back_kernel(h1_ref, z0_ref, zn_ref, yt0_ref, ytn_ref, gate_ref, zmeta_ref, dw_ref, dwb_ref,
                 lng_ref, lnb_ref, wcp_ref, wv_ref, wg_ref, wout_ref, g3_ref, w1_ref, w3_ref, w2_ref,
                 gf_ref, out_ref, zpad_ref, shift_ref, zs_ref, ys_ref, *, tiles_per_seq):
    n = pl.program_id(0)
    conv_args = (zmeta_ref, dw_ref, dwb_ref, lng_ref, lnb_ref, zpad_ref, shift_ref, zs_ref)

    @pl.when(n == 0)
    def _():
        _conv_stage(z0_ref, True, *conv_args)
        _token_major_stage(yt0_ref, ys_ref)

    zs = zs_ref[...]
    ys = jnp.concatenate([ys_ref[j] for j in range(D_SSM // LANES)], axis=1).astype(_bf16)
    y_conv = _dot(zs, wcp_ref[...])
    y_ssm = _dot(ys, wv_ref[...]) * jax.nn.sigmoid(_dot(ys, wg_ref[...]))

    _conv_stage(zn_ref, (n + 1) % tiles_per_seq == 0, *conv_args)
    _token_major_stage(ytn_ref, ys_ref)

    mix = (gate_ref[:, 0:D_MODEL].astype(_f32) * y_conv
           + gate_ref[:, D_MODEL:].astype(_f32) * y_ssm).astype(_bf16)
    h2 = h1_ref[...] + _dot(mix, wout_ref[...])
    nrm = _rms_norm(h2, g3_ref[...]).astype(_bf16)
    h3 = h2 + 0.5 * _swiglu(nrm, w1_ref, w3_ref, w2_ref)
    out_ref[...] = _rms_norm(h3, gf_ref[...])


def _back(h1, z, y_t, gate, zmeta, dw, dwb, lng, lnb, wcp, wv, wg, wout, g3, w1, w3, w2, gf,
          batch, seq, tile):
    tiles = batch * seq // tile
    nxt = lambda n: jnp.minimum(n + 1, tiles - 1)
    row = lambda w: pl.BlockSpec((tile, w), lambda n: (n, 0))
    yt_block = (N_SSM_GROUPS, tile // SSM_CHUNK, CHUNK_LANES)
    return pl.pallas_call(
        functools.partial(_back_kernel, tiles_per_seq=seq // tile),
        grid=(tiles,),
        in_specs=[row(D_MODEL),
                  pl.BlockSpec((tile, D_CONV), lambda n: (0, 0)),
                  pl.BlockSpec((tile, D_CONV), lambda n: (nxt(n), 0)),
                  pl.BlockSpec(yt_block, lambda n: (0, 0, 0)),
                  pl.BlockSpec(yt_block, lambda n: (0, nxt(n), 0)),
                  row(2 * D_MODEL),
                  _resident((CONV_HALO, D_CONV)), _resident((CONV_WIDTH, D_CONV)),
                  _resident((1, D_CONV)), _resident((1, D_CONV)), _resident((1, D_CONV)),
                  _resident((D_CONV, D_MODEL)), _resident((D_SSM, D_MODEL)),
                  _resident((D_SSM, D_MODEL)), _resident((D_MODEL, D_MODEL)),
                  _resident((1, D_MODEL)), _resident((D_MODEL, D_FF)), _resident((D_MODEL, D_FF)),
                  _resident((D_FF, D_MODEL)), _resident((1, D_MODEL))],
        out_specs=row(D_MODEL),
        out_shape=jax.ShapeDtypeStruct((batch * seq, D_MODEL), _f32),
        scratch_shapes=[pltpu.VMEM((CONV_HALO + tile, D_CONV), _f32),
                        pltpu.VMEM((SUBLANES - 1, CONV_HALO + tile, LANES), _f32),
                        pltpu.VMEM((tile, D_CONV), _bf16),
                        pltpu.VMEM((D_SSM // LANES, tile, LANES), _f32)],
        compiler_params=pltpu.CompilerParams(dimension_semantics=("arbitrary",),
                                             vmem_limit_bytes=VMEM_LIMIT),
        name="back",
    )(h1, z, z, y_t, y_t, gate, zmeta, dw, dwb, lng, lnb, wcp, wv, wg, wout, g3, w1, w3, w2, gf)


def kernel(x, meta_tokens, ffn1_norm, ffn1_w1, ffn1_w3, ffn1_w2, mix_norm, w_in, b_gate, conv_dw, conv_dw_b, conv_ln_g, conv_ln_b, conv_proj, ssm_lam_re, ssm_lam_im, ssm_log_dt, ssm_b_re, ssm_b_im, ssm_c_re, ssm_c_im, ssm_d, ssm_w_v, ssm_w_g, w_out, ffn2_norm, ffn2_w1, ffn2_w3, ffn2_w2, final_norm):
    batch, seq, _ = x.shape
    assert ffn1_norm.shape[0] == 1 and seq % ROW_TILE == 0 and ROW_TILE % (SUBLANES * SSM_CHUNK) == 0
    assert N_META <= SSM_CHUNK and N_META <= CONV_HALO
    T, H, G = SSM_CHUNK, SSM_GROUP, N_SSM_GROUPS
    row = lambda v: v.reshape(1, -1)
    bf = lambda w: w[0].astype(_bf16)
    dup = lambda a: jnp.concatenate([a, a], axis=-1)

    front_w = (row(ffn1_norm[0]), bf(ffn1_w1), bf(ffn1_w3), bf(ffn1_w2), row(mix_norm[0]),
               bf(w_in), row(b_gate[0]))
    h1, z, u_t, gate = _front(x.reshape(batch * seq, D_MODEL), *front_w, tile=ROW_TILE,
                              group_major=True)
    meta_chunk = jnp.zeros((T, D_MODEL), _f32).at[T - N_META:].set(meta_tokens)
    _, z_m, u_m, _ = _front(meta_chunk, *front_w, tile=T, group_major=False)
    um_t = u_m.reshape(T, G, H).transpose(1, 0, 2).reshape(G, 1, T * H)
    um_t = jnp.concatenate([um_t, jnp.zeros((G, 2 * SUBLANES - 1, T * H), _f32)], axis=1)

    lam2 = jnp.stack([dup(ssm_lam_re[0]), dup(ssm_lam_im[0])], axis=1)
    bt2 = jnp.stack([dup(ssm_b_re[0].transpose(0, 2, 1)), dup(ssm_b_im[0].transpose(0, 2, 1))], axis=1)
    c2 = jnp.stack([dup(ssm_c_re[0]), dup(ssm_c_im[0])], axis=1)
    y_t = _ssm(u_t, um_t, lam2, ssm_log_dt[0].reshape(G, 1, 1), bt2, c2,
               ssm_d[0].reshape(G, H, 1), seq // T)

    z_halo = z_m[T - CONV_HALO:].astype(_f32)
    out = _back(h1, z, y_t, gate, z_halo, conv_dw[0], row(conv_dw_b[0]), row(conv_ln_g[0]),
                row(conv_ln_b[0]), bf(conv_proj), bf(ssm_w_v), bf(ssm_w_g), bf(w_out),
                row(ffn2_norm[0]), bf(ffn2_w1), bf(ffn2_w3), bf(ffn2_w2), row(final_norm),
                batch, seq, ROW_TILE)
    return out.reshape(batch, seq, D_MODEL)
```

```python
import functools

import jax
import jax.numpy as jnp
from jax.experimental import pallas as pl
from jax.experimental.pallas import tpu as pltpu

D_MODEL = 1024
N_META = 16
D_FF = 2816
D_CONV = 512
CONV_WIDTH = 31
D_SSM = 512
SSM_GROUP = 16
N_SSM_GROUPS = D_SSM // SSM_GROUP
SSM_STATE = 64
EPS = 1e-6

LANES = 128
SUBLANES = 8
SSM_CHUNK = 64
CHUNK_LANES = SSM_CHUNK * SSM_GROUP
STATE_LANES = 2 * SSM_STATE
GROUPS_PER_VREG = LANES // SSM_GROUP
CONV_HALO = 32
FF_SPLITS = ((0, 1024), (1024, 2048), (2048, 2816))
ROW_TILE = 512
VMEM_LIMIT = 60 * 1024 * 1024

assert STATE_LANES == LANES and GROUPS_PER_VREG == SUBLANES

_bf16 = jnp.bfloat16
_f32 = jnp.float32


def _dot(a, b):
    return jnp.dot(a, b, preferred_element_type=_f32)


def _dot_nt(a, b, precision=None):
    return jax.lax.dot_general(a, b, (((1,), (1,)), ((), ())), precision=precision,
                               preferred_element_type=_f32)


def _rms_norm(x, g):
    return x * jax.lax.rsqrt(jnp.mean(x * x, axis=-1, keepdims=True) + EPS) * g


def _swiglu(n, w1_ref, w3_ref, w2_ref):
    acc = None
    for lo, hi in FF_SPLITS:
        a = _dot(n, w1_ref[:, lo:hi])
        b = _dot(n, w3_ref[:, lo:hi])
        f = (a * jax.nn.sigmoid(a) * b).astype(_bf16)
        part = _dot(f, w2_ref[lo:hi, :])
        acc = part if acc is None else acc + part
    return acc


def _block_transpose8(v):
    blk = jax.lax.broadcasted_iota(jnp.int32, v[0].shape, 1) // SSM_GROUP
    for d in (4, 2, 1):
        hi = (blk & d) != 0
        shift = d * SSM_GROUP
        new = list(v)
        for i in range(GROUPS_PER_VREG):
            if i & d:
                continue
            new[i] = jnp.where(hi, pltpu.roll(v[i + d], shift, axis=1), v[i])
            new[i + d] = jnp.where(hi, v[i + d], pltpu.roll(v[i], LANES - shift, axis=1))
        v = new
    return v


def _front_kernel(x_ref, g1_ref, w1_ref, w3_ref, w2_ref, g2_ref, win_ref, bg_ref,
                  h1_ref, z_ref, u_ref, gate_ref, *scratch, group_major):
    x = x_ref[...]
    n = _rms_norm(x, g1_ref[...]).astype(_bf16)
    h1 = x + 0.5 * _swiglu(n, w1_ref, w3_ref, w2_ref)
    h1_ref[...] = h1
    u = _rms_norm(h1, g2_ref[...]).astype(_bf16)
    v = _dot(u, win_ref[:, 0:D_CONV])
    g = _dot(u, win_ref[:, D_CONV:2 * D_CONV])
    z_ref[...] = (v * jax.nn.sigmoid(g)).astype(_bf16)
    us = _dot(u, win_ref[:, 2 * D_CONV:2 * D_CONV + D_SSM])
    if group_major:
        us_ref, = scratch
        n_chunks = us.shape[0] // SSM_CHUNK
        for j in range(D_SSM // LANES):
            us_ref[j] = us[:, j * LANES:(j + 1) * LANES]
            for q in range(SSM_CHUNK // SUBLANES):
                rows = [us_ref[j, pl.ds(SSM_CHUNK - 1 - (SUBLANES * q + r), n_chunks, stride=SSM_CHUNK), :]
                        for r in range(SUBLANES)]
                for a, blk in enumerate(_block_transpose8(rows)):
                    u_ref[GROUPS_PER_VREG * j + a, :, q * LANES:(q + 1) * LANES] = blk
    else:
        u_ref[...] = us
    gi = _dot(u, win_ref[:, 2 * D_CONV + D_SSM:]) + bg_ref[...]
    gate_ref[...] = jax.nn.sigmoid(gi).astype(_bf16)


def _resident(shape):
    return pl.BlockSpec(shape, lambda *_: (0,) * len(shape), pipeline_mode=pl.Buffered(1))


def _front(x2d, g1, w1, w3, w2, g2, win, bg, tile, group_major):
    rows = x2d.shape[0]
    d_in = win.shape[1]
    n_gate = d_in - 2 * D_CONV - D_SSM
    row = lambda w: pl.BlockSpec((tile, w), lambda i: (i, 0))
    if group_major:
        u_spec = pl.BlockSpec((N_SSM_GROUPS, tile // SSM_CHUNK, CHUNK_LANES), lambda i: (0, i, 0))
        u_shape = jax.ShapeDtypeStruct((N_SSM_GROUPS, rows // SSM_CHUNK, CHUNK_LANES), _f32)
        scratch = [pltpu.VMEM((D_SSM // LANES, tile, LANES), _f32)]
    else:
        u_spec, u_shape, scratch = row(D_SSM), jax.ShapeDtypeStruct((rows, D_SSM), _f32), []
    return pl.pallas_call(
        functools.partial(_front_kernel, group_major=group_major),
        grid=(rows // tile,),
        in_specs=[row(D_MODEL), _resident((1, D_MODEL)), _resident((D_MODEL, D_FF)),
                  _resident((D_MODEL, D_FF)), _resident((D_FF, D_MODEL)),
                  _resident((1, D_MODEL)), _resident((D_MODEL, d_in)), _resident((1, n_gate))],
        out_specs=[row(D_MODEL), row(D_CONV), u_spec, row(n_gate)],
        out_shape=[jax.ShapeDtypeStruct((rows, D_MODEL), _f32),
                   jax.ShapeDtypeStruct((rows, D_CONV), _bf16),
                   u_shape,
                   jax.ShapeDtypeStruct((rows, n_gate), _bf16)],
        scratch_shapes=scratch,
        compiler_params=pltpu.CompilerParams(dimension_semantics=("arbitrary",),
                                             vmem_limit_bytes=VMEM_LIMIT),
        name="front" if group_major else "front_meta",
    )(x2d, g1, w1, w3, w2, g2, win, bg)


def _ssm_kernel(u_ref, um_ref, lam_ref, ldt_ref, bt_ref, c_ref, d_ref, y_ref, toep_ref,
                *, chunks_per_seq, levels):
    T, H, P = SSM_CHUNK, SSM_GROUP, SSM_STATE
    lane = jax.lax.broadcasted_iota(jnp.int32, (1, STATE_LANES), 1)
    lo = lane < P
    sgn = jnp.where(lo, -1.0, 1.0)
    lam_re = lam_ref[0, 0:1, :]
    lam_im = lam_ref[0, 1:2, :]
    dt = jnp.exp(ldt_ref[0])
    ar, ai = lam_re * dt, lam_im * dt

    def powers(k):
        mag, ang = jnp.exp(ar * k), ai * k
        cs, sn = jnp.cos(ang), jnp.sin(ang)
        return mag * jnp.where(lo, cs, sn), mag * jnp.where(lo, sn, cs)

    def cmul3(m1, m2, pw, pws):
        return (m1[None] * pw[:, None, :] + m2[None] * pws[:, None, :]).reshape(T * H, STATE_LANES)

    def cmul_rows(v, vs, m1, m2):
        return v * m1 + vs * m2, vs * m1 - v * m2

    n_dbl = (T // SUBLANES).bit_length() - 1
    assert SUBLANES << n_dbl == T and n_dbl < SUBLANES
    row8 = jax.lax.broadcasted_iota(jnp.int32, (SUBLANES, 1), 0)
    aux, aux_s = powers(jnp.where(row8 < n_dbl, jnp.left_shift(SUBLANES, row8),
                                  jnp.where(row8 == n_dbl, 1, 0)).astype(_f32))
    x1 = jnp.where(lo, aux, aux_s)
    x2 = sgn * jnp.where(lo, aux_s, aux)
    pw, pws = powers(row8.astype(_f32))
    for i in range(n_dbl):
        nxt, nxt_s = cmul_rows(pw, pws, x1[i:i + 1], x2[i:i + 1])
        pw, pws = jnp.concatenate([pw, nxt], axis=0), jnp.concatenate([pws, nxt_s], axis=0)
    pw1, pw1s = cmul_rows(pw, pws, x1[n_dbl:n_dbl + 1], x2[n_dbl:n_dbl + 1])

    lb_re, lb_im = x1[n_dbl:n_dbl + 1], sgn * x2[n_dbl:n_dbl + 1]
    den = lam_re * lam_re + lam_im * lam_im
    co_re = ((lb_re - 1.0) * lam_re + lb_im * lam_im) / den
    co_im = (lb_im * lam_re - (lb_re - 1.0) * lam_im) / den
    b_re, b_im = bt_ref[0, 0], bt_ref[0, 1]
    bb_re = co_re * b_re - co_im * b_im
    bb_im = co_re * b_im + co_im * b_re
    c1, c2 = c_ref[0, 0], sgn * c_ref[0, 1]

    ws = cmul3(bb_re, sgn * bb_im, pw, pws).astype(_bf16)
    wct = (cmul3(c1, c2, pw1, pw1s) * (-sgn)).astype(_bf16)
    e = cmul3(c1, c2, pw, pws)
    r0 = _dot_nt(jnp.where(lo, bb_re, -bb_im), e, precision=jax.lax.Precision.HIGHEST)
    lane_c = jax.lax.broadcasted_iota(jnp.int32, (H, CHUNK_LANES), 1)
    row_c = jax.lax.broadcasted_iota(jnp.int32, (H, CHUNK_LANES), 0)
    r0 = r0 + jnp.where(lane_c == row_c, d_ref[0], 0.0)
    for r in range(SUBLANES):
        rr = pltpu.roll(r0, r * H, axis=1) if r else r0
        for a in range(T // SUBLANES):
            s = SUBLANES * a + r
            blk = pltpu.roll(rr, a * LANES, axis=1) if a else rr
            toep_ref[(T - 1 - s) * H:(T - s) * H, :] = jnp.where(lane_c >= s * H, blk, 0.0).astype(_bf16)

    u = u_ref[0].astype(_bf16)
    rows = u.shape[0]
    s_loc = _dot(u, ws)
    sm = _dot(um_ref[0].astype(_bf16), ws)[0:1, :]

    lvl_col = jax.lax.broadcasted_iota(jnp.int32, (SUBLANES, 1), 0)
    step, step_s = powers(jnp.left_shift(T, lvl_col).astype(_f32))
    a1 = jnp.where(lo, step, step_s)
    a2 = sgn * jnp.where(lo, step_s, step)

    def cmul(v, lvl):
        return a1[lvl:lvl + 1, :] * v + a2[lvl:lvl + 1, :] * pltpu.roll(v, P, axis=1)

    cidx = jax.lax.broadcasted_iota(jnp.int32, (rows, STATE_LANES), 0) % chunks_per_seq
    first = cidx == 0
    st = s_loc + jnp.where(first, cmul(jnp.broadcast_to(sm, s_loc.shape), 0), 0.0)
    for lvl in range(levels):
        shift = 1 << lvl
        prev = jnp.where(cidx >= shift, pltpu.roll(st, shift, axis=0), 0.0)
        st = st + cmul(prev, lvl)
    st_in = jnp.where(first, sm, pltpu.roll(st, 1, axis=0))
    y = _dot(u, toep_ref[...]) + _dot_nt(st_in.astype(_bf16), wct)
    y_ref[0] = jax.nn.gelu(y)


def _ssm(u_t, um_t, lam2, ldt, bt2, c2, d_col, chunks_per_seq):
    groups, rows, _ = u_t.shape
    levels = max(1, (chunks_per_seq - 1).bit_length())
    assert levels <= SUBLANES
    per_group = lambda *s: pl.BlockSpec((1,) + s, lambda g: (g,) + (0,) * len(s))
    return pl.pallas_call(
        functools.partial(_ssm_kernel, chunks_per_seq=chunks_per_seq, levels=levels),
        grid=(groups,),
        in_specs=[per_group(rows, CHUNK_LANES), per_group(um_t.shape[1], CHUNK_LANES),
                  per_group(2, STATE_LANES), per_group(1, 1),
                  per_group(2, SSM_GROUP, STATE_LANES), per_group(2, SSM_GROUP, STATE_LANES),
                  per_group(SSM_GROUP, 1)],
        out_specs=per_group(rows, CHUNK_LANES),
        out_shape=jax.ShapeDtypeStruct((groups, rows, CHUNK_LANES), _f32),
        scratch_shapes=[pltpu.VMEM((CHUNK_LANES, CHUNK_LANES), _bf16)],
        compiler_params=pltpu.CompilerParams(dimension_semantics=("arbitrary",),
                                             vmem_limit_bytes=VMEM_LIMIT),
        name="ssm",
    )(u_t, um_t, lam2, ldt, bt2, c2, d_col)


def _conv_stage(z_ref, first_of_seq, zmeta_ref, dw_ref, dwb_ref, lng_ref, lnb_ref,
                zpad_ref, shift_ref, zs_ref):
    tile = z_ref.shape[0]
    padded = CONV_HALO + tile
    if first_of_seq is True:
        zpad_ref[0:CONV_HALO, :] = zmeta_ref[...]
    else:
        zpad_ref[0:CONV_HALO, :] = jnp.where(first_of_seq, zmeta_ref[...], zpad_ref[0:CONV_HALO, :])
    zpad_ref[CONV_HALO:, :] = z_ref[...].astype(_f32)
    base = CONV_HALO - (CONV_WIDTH - 1)
    convs = []
    for j in range(D_CONV // LANES):
        cols = slice(j * LANES, (j + 1) * LANES)
        zp = zpad_ref[:, cols]
        for r in range(1, SUBLANES):
            shift_ref[r - 1] = pltpu.roll(zp, padded - r, axis=0)
        conv = None
        for k in range(CONV_WIDTH):
            a, r = divmod(base + k, SUBLANES)
            rows = slice(a * SUBLANES, a * SUBLANES + tile)
            src = zpad_ref[rows, cols] if r == 0 else shift_ref[r - 1, rows, :]
            term = dw_ref[k:k + 1, cols] * src
            conv = term if conv is None else conv + term
        convs.append(conv + dwb_ref[:, cols])
    conv = jnp.concatenate(convs, axis=1)
    zpad_ref[0:CONV_HALO, :] = zpad_ref[tile:tile + CONV_HALO, :]
    mu = jnp.mean(conv, axis=-1, keepdims=True)
    cen = conv - mu
    var = jnp.mean(cen * cen, axis=-1, keepdims=True)
    zn = cen * jax.lax.rsqrt(var + EPS) * lng_ref[...] + lnb_ref[...]
    zs_ref[...] = (zn * jax.nn.sigmoid(zn)).astype(_bf16)


def _token_major_stage(yt_ref, ys_ref):
    n_chunks = yt_ref.shape[1]
    for j in range(D_SSM // LANES):
        for q in range(SSM_CHUNK // SUBLANES):
            blks = [yt_ref[GROUPS_PER_VREG * j + a, :, q * LANES:(q + 1) * LANES]
                    for a in range(GROUPS_PER_VREG)]
            for r, rows in enumerate(_block_transpose8(blks)):
                ys_ref[j, pl.ds(SUBLANES * q + r, n_chunks, stride=SSM_CHUNK), :] = rows


def _back_kernel(h1_ref, z0_ref, zn_ref, yt0_ref, ytn_ref, gate_ref, zmeta_ref, dw_ref, dwb_ref,
                 lng_ref, lnb_ref, wcp_ref, wv_ref, wg_ref, wout_ref, g3_ref, w1_ref, w3_ref, w2_ref,
                 gf_ref, out_ref, zpad_ref, shift_ref, zs_ref, ys_ref, *, tiles_per_seq):
    n = pl.program_id(0)
    conv_args = (zmeta_ref, dw_ref, dwb_ref, lng_ref, lnb_ref, zpad_ref, shift_ref, zs_ref)

    @pl.when(n == 0)
    def _():
        _conv_stage(z0_ref, True, *conv_args)
        _token_major_stage(yt0_ref, ys_ref)

    zs = zs_ref[...]
    ys = jnp.concatenate([ys_ref[j] for j in range(D_SSM // LANES)], axis=1).astype(_bf16)
    y_conv = _dot(zs, wcp_ref[...])
    y_ssm = _dot(ys, wv_ref[...]) * jax.nn.sigmoid(_dot(ys, wg_ref[...]))

    _conv_stage(zn_ref, (n + 1) % tiles_per_seq == 0, *conv_args)
    _token_major_stage(ytn_ref, ys_ref)

    mix = (gate_ref[:, 0:D_MODEL].astype(_f32) * y_conv
           + gate_ref[:, D_MODEL:].astype(_f32) * y_ssm).astype(_bf16)
    h2 = h1_ref[...] + _dot(mix, wout_ref[...])
    nrm = _rms_norm(h2, g3_ref[...]).astype(_bf16)
    h3 = h2 + 0.5 * _swiglu(nrm, w1_ref, w3_ref, w2_ref)
    out_ref[...] = _rms_norm(h3, gf_ref[...])


def _back(h1, z, y_t, gate, zmeta, dw, dwb, lng, lnb, wcp, wv, wg, wout, g3, w1, w3, w2, gf,
          batch, seq, tile):
    tiles = batch * seq // tile
    nxt = lambda n: jnp.minimum(n + 1, tiles - 1)
    row = lambda w: pl.BlockSpec((tile, w), lambda n: (n, 0))
    yt_block = (N_SSM_GROUPS, tile // SSM_CHUNK, CHUNK_LANES)
    return pl.pallas_call(
        functools.partial(_back_kernel, tiles_per_seq=seq // tile),
        grid=(tiles,),
        in_specs=[row(D_MODEL),
                  pl.BlockSpec((tile, D_CONV), lambda n: (0, 0)),
                  pl.BlockSpec((tile, D_CONV), lambda n: (nxt(n), 0)),
                  pl.BlockSpec(yt_block, lambda n: (0, 0, 0)),
                  pl.BlockSpec(yt_block, lambda n: (0, nxt(n), 0)),
                  row(2 * D_MODEL),
                  _resident((CONV_HALO, D_CONV)), _resident((CONV_WIDTH, D_CONV)),
                  _resident((1, D_CONV)), _resident((1, D_CONV)), _resident((1, D_CONV)),
                  _resident((D_CONV, D_MODEL)), _resident((D_SSM, D_MODEL)),
                  _resident((D_SSM, D_MODEL)), _resident((D_MODEL, D_MODEL)),
                  _resident((1, D_MODEL)), _resident((D_MODEL, D_FF)), _resident((D_MODEL, D_FF)),
                  _resident((D_FF, D_MODEL)), _resident((1, D_MODEL))],
        out_specs=row(D_MODEL),
        out_shape=jax.ShapeDtypeStruct((batch * seq, D_MODEL), _f32),
        scratch_shapes=[pltpu.VMEM((CONV_HALO + tile, D_CONV), _f32),
                        pltpu.VMEM((SUBLANES - 1, CONV_HALO + tile, LANES), _f32),
                        pltpu.VMEM((tile, D_CONV), _bf16),
                        pltpu.VMEM((D_SSM // LANES, tile, LANES), _f32)],
        compiler_params=pltpu.CompilerParams(dimension_semantics=("arbitrary",),
                                             vmem_limit_bytes=VMEM_LIMIT),
        name="back",
    )(h1, z, z, y_t, y_t, gate, zmeta, dw, dwb, lng, lnb, wcp, wv, wg, wout, g3, w1, w3, w2, gf)


def kernel(x, meta_tokens, ffn1_norm, ffn1_w1, ffn1_w3, ffn1_w2, mix_norm, w_in, b_gate, conv_dw, conv_dw_b, conv_ln_g, conv_ln_b, conv_proj, ssm_lam_re, ssm_lam_im, ssm_log_dt, ssm_b_re, ssm_b_im, ssm_c_re, ssm_c_im, ssm_d, ssm_w_v, ssm_w_g, w_out, ffn2_norm, ffn2_w1, ffn2_w3, ffn2_w2, final_norm):
    batch, seq, _ = x.shape
    assert ffn1_norm.shape[0] == 1 and seq % ROW_TILE == 0 and ROW_TILE % (SUBLANES * SSM_CHUNK) == 0
    assert N_META <= SSM_CHUNK and N_META <= CONV_HALO
    T, H, G = SSM_CHUNK, SSM_GROUP, N_SSM_GROUPS
    row = lambda v: v.reshape(1, -1)
    bf = lambda w: w[0].astype(_bf16)
    dup = lambda a: jnp.concatenate([a, a], axis=-1)

    front_w = (row(ffn1_norm[0]), bf(ffn1_w1), bf(ffn1_w3), bf(ffn1_w2), row(mix_norm[0]),
               bf(w_in), row(b_gate[0]))
    h1, z, u_t, gate = _front(x.reshape(batch * seq, D_MODEL), *front_w, tile=ROW_TILE,
                              group_major=True)
    meta_chunk = jnp.zeros((T, D_MODEL), _f32).at[T - N_META:].set(meta_tokens)
    _, z_m, u_m, _ = _front(meta_chunk, *front_w, tile=T, group_major=False)
    um_t = u_m[::-1].reshape(T, G, H).transpose(1, 0, 2).reshape(G, 1, T * H)
    um_t = jnp.concatenate([um_t, jnp.zeros((G, 2 * SUBLANES - 1, T * H), _f32)], axis=1)

    lam2 = jnp.stack([dup(ssm_lam_re[0]), dup(ssm_lam_im[0])], axis=1)
    bt2 = jnp.stack([dup(ssm_b_re[0].transpose(0, 2, 1)), dup(ssm_b_im[0].transpose(0, 2, 1))], axis=1)
    c2 = jnp.stack([dup(ssm_c_re[0]), dup(ssm_c_im[0])], axis=1)
    y_t = _ssm(u_t, um_t, lam2, ssm_log_dt[0].reshape(G, 1, 1), bt2, c2,
               ssm_d[0].reshape(G, H, 1), seq // T)

    z_halo = z_m[T - CONV_HALO:].astype(_f32)
    out = _back(h1, z, y_t, gate, z_halo, conv_dw[0], row(conv_dw_b[0]), row(conv_ln_g[0]),
                row(conv_ln_b[0]), bf(conv_proj), bf(ssm_w_v), bf(ssm_w_g), bf(w_out),
                row(ffn2_norm[0]), bf(ffn2_w1), bf(ffn2_w3), bf(ffn2_w2), row(final_norm),
                batch, seq, ROW_TILE)
    return out.reshape(batch, seq, D_MODEL)
```

```python
import functools

import jax
import jax.numpy as jnp
from jax.experimental import pallas as pl
from jax.experimental.pallas import tpu as pltpu

D_MODEL = 1024
N_META = 16
D_FF = 2816
D_CONV = 512
CONV_WIDTH = 31
D_SSM = 512
SSM_GROUP = 16
N_SSM_GROUPS = D_SSM // SSM_GROUP
SSM_STATE = 64
EPS = 1e-6

LANES = 128
SUBLANES = 8
SSM_CHUNK = 64
CHUNK_LANES = SSM_CHUNK * SSM_GROUP
STATE_LANES = 2 * SSM_STATE
GROUPS_PER_VREG = LANES // SSM_GROUP
CONV_HALO = 32
CONV_PIECE_ROWS = 16
SSM_GROUPS_PER_STEP = 2
FF_SPLITS = ((0, 1024), (1024, 2048), (2048, 2816))
ROW_TILE = 512
VMEM_LIMIT = 60 * 1024 * 1024

assert STATE_LANES == LANES and GROUPS_PER_VREG == SUBLANES

_bf16 = jnp.bfloat16
_f32 = jnp.float32


def _dot(a, b):
    return jnp.dot(a, b, preferred_element_type=_f32)


def _dot_nt(a, b, precision=None):
    return jax.lax.dot_general(a, b, (((1,), (1,)), ((), ())), precision=precision,
                               preferred_element_type=_f32)


def _rms_norm(x, g):
    return x * jax.lax.rsqrt(jnp.mean(x * x, axis=-1, keepdims=True) + EPS) * g


def _swiglu(n, w1_ref, w3_ref, w2_ref):
    acc = None
    for lo, hi in FF_SPLITS:
        a = _dot(n, w1_ref[:, lo:hi])
        b = _dot(n, w3_ref[:, lo:hi])
        f = (a * jax.nn.sigmoid(a) * b).astype(_bf16)
        part = _dot(f, w2_ref[lo:hi, :])
        acc = part if acc is None else acc + part
    return acc


def _block_transpose8(v):
    blk = jax.lax.broadcasted_iota(jnp.int32, v[0].shape, 1) // SSM_GROUP
    for d in (4, 2, 1):
        hi = (blk & d) != 0
        shift = d * SSM_GROUP
        new = list(v)
        for i in range(GROUPS_PER_VREG):
            if i & d:
                continue
            new[i] = jnp.where(hi, pltpu.roll(v[i + d], shift, axis=1), v[i])
            new[i + d] = jnp.where(hi, v[i + d], pltpu.roll(v[i], LANES - shift, axis=1))
        v = new
    return v


def _front_kernel(x_ref, g1_ref, w1_ref, w3_ref, w2_ref, g2_ref, win_ref, bg_ref,
                  h1_ref, z_ref, u_ref, gate_ref, *scratch, group_major):
    x = x_ref[...]
    n = _rms_norm(x, g1_ref[...]).astype(_bf16)
    h1 = x + 0.5 * _swiglu(n, w1_ref, w3_ref, w2_ref)
    h1_ref[...] = h1
    u = _rms_norm(h1, g2_ref[...]).astype(_bf16)
    v = _dot(u, win_ref[:, 0:D_CONV])
    g = _dot(u, win_ref[:, D_CONV:2 * D_CONV])
    z_ref[...] = (v * jax.nn.sigmoid(g)).astype(_bf16)
    us = _dot(u, win_ref[:, 2 * D_CONV:2 * D_CONV + D_SSM])
    if group_major:
        us_ref, = scratch
        n_chunks = us.shape[0] // SSM_CHUNK
        for j in range(D_SSM // LANES):
            us_ref[j] = us[:, j * LANES:(j + 1) * LANES]
            for q in range(SSM_CHUNK // SUBLANES):
                rows = [us_ref[j, pl.ds(SSM_CHUNK - 1 - (SUBLANES * q + r), n_chunks, stride=SSM_CHUNK), :]
                        for r in range(SUBLANES)]
                for a, blk in enumerate(_block_transpose8(rows)):
                    u_ref[GROUPS_PER_VREG * j + a, :, q * LANES:(q + 1) * LANES] = blk
    else:
        u_ref[...] = us
    gi = _dot(u, win_ref[:, 2 * D_CONV + D_SSM:]) + bg_ref[...]
    gate_ref[...] = jax.nn.sigmoid(gi).astype(_bf16)


def _resident(shape):
    return pl.BlockSpec(shape, lambda *_: (0,) * len(shape), pipeline_mode=pl.Buffered(1))


def _front(x2d, g1, w1, w3, w2, g2, win, bg, tile, group_major):
    rows = x2d.shape[0]
    d_in = win.shape[1]
    n_gate = d_in - 2 * D_CONV - D_SSM
    row = lambda w: pl.BlockSpec((tile, w), lambda i: (i, 0))
    if group_major:
        u_spec = pl.BlockSpec((N_SSM_GROUPS, tile // SSM_CHUNK, CHUNK_LANES), lambda i: (0, i, 0))
        u_shape = jax.ShapeDtypeStruct((N_SSM_GROUPS, rows // SSM_CHUNK, CHUNK_LANES), _f32)
        scratch = [pltpu.VMEM((D_SSM // LANES, tile, LANES), _f32)]
    else:
        u_spec, u_shape, scratch = row(D_SSM), jax.ShapeDtypeStruct((rows, D_SSM), _f32), []
    return pl.pallas_call(
        functools.partial(_front_kernel, group_major=group_major),
        grid=(rows // tile,),
        in_specs=[row(D_MODEL), _resident((1, D_MODEL)), _resident((D_MODEL, D_FF)),
                  _resident((D_MODEL, D_FF)), _resident((D_FF, D_MODEL)),
                  _resident((1, D_MODEL)), _resident((D_MODEL, d_in)), _resident((1, n_gate))],
        out_specs=[row(D_MODEL), row(D_CONV), u_spec, row(n_gate)],
        out_shape=[jax.ShapeDtypeStruct((rows, D_MODEL), _f32),
                   jax.ShapeDtypeStruct((rows, D_CONV), _bf16),
                   u_shape,
                   jax.ShapeDtypeStruct((rows, n_gate), _bf16)],
        scratch_shapes=scratch,
        compiler_params=pltpu.CompilerParams(dimension_semantics=("arbitrary",),
                                             vmem_limit_bytes=VMEM_LIMIT),
        name="front" if group_major else "front_meta",
    )(x2d, g1, w1, w3, w2, g2, win, bg)


def _ssm_kernel(*refs, chunks_per_seq, levels, groups_per_step):
    for gi in range(groups_per_step):
        _ssm_group(*[r.at[gi] for r in refs], chunks_per_seq=chunks_per_seq, levels=levels)


def _ssm_group(u_ref, um_ref, lam_ref, ldt_ref, bt_ref, c_ref, d_ref, y_ref, toep_ref,
               *, chunks_per_seq, levels):
    T, H, P = SSM_CHUNK, SSM_GROUP, SSM_STATE
    lane = jax.lax.broadcasted_iota(jnp.int32, (1, STATE_LANES), 1)
    lo = lane < P
    sgn = jnp.where(lo, -1.0, 1.0)
    lam_re = lam_ref[0:1, :]
    lam_im = lam_ref[1:2, :]
    dt = jnp.exp(ldt_ref[...])
    ar, ai = lam_re * dt, lam_im * dt

    def powers(k):
        mag, ang = jnp.exp(ar * k), ai * k
        cs, sn = jnp.cos(ang), jnp.sin(ang)
        return mag * jnp.where(lo, cs, sn), mag * jnp.where(lo, sn, cs)

    def cmul3(m1, m2, pw, pws):
        return (m1[None] * pw[:, None, :] + m2[None] * pws[:, None, :]).reshape(T * H, STATE_LANES)

    def cmul_rows(v, vs, m1, m2):
        return v * m1 + vs * m2, vs * m1 - v * m2

    n_dbl = (T // SUBLANES).bit_length() - 1
    assert SUBLANES << n_dbl == T and n_dbl < SUBLANES
    row8 = jax.lax.broadcasted_iota(jnp.int32, (SUBLANES, 1), 0)
    aux, aux_s = powers(jnp.where(row8 < n_dbl, jnp.left_shift(SUBLANES, row8),
                                  jnp.where(row8 == n_dbl, 1, 0)).astype(_f32))
    x1 = jnp.where(lo, aux, aux_s)
    x2 = sgn * jnp.where(lo, aux_s, aux)
    pw, pws = powers(row8.astype(_f32))
    for i in range(n_dbl):
        nxt, nxt_s = cmul_rows(pw, pws, x1[i:i + 1], x2[i:i + 1])
        pw, pws = jnp.concatenate([pw, nxt], axis=0), jnp.concatenate([pws, nxt_s], axis=0)
    pw1, pw1s = cmul_rows(pw, pws, x1[n_dbl:n_dbl + 1], x2[n_dbl:n_dbl + 1])

    lb_re, lb_im = x1[n_dbl:n_dbl + 1], sgn * x2[n_dbl:n_dbl + 1]
    den = lam_re * lam_re + lam_im * lam_im
    co_re = ((lb_re - 1.0) * lam_re + lb_im * lam_im) / den
    co_im = (lb_im * lam_re - (lb_re - 1.0) * lam_im) / den
    b_re, b_im = bt_ref[0], bt_ref[1]
    bb_re = co_re * b_re - co_im * b_im
    bb_im = co_re * b_im + co_im * b_re
    c1, c2 = c_ref[0], sgn * c_ref[1]

    ws = cmul3(bb_re, sgn * bb_im, pw, pws).astype(_bf16)
    wct = (cmul3(c1, c2, pw1, pw1s) * (-sgn)).astype(_bf16)
    e = cmul3(c1, c2, pw, pws)
    r0 = _dot_nt(jnp.where(lo, bb_re, -bb_im), e, precision=jax.lax.Precision.HIGHEST)
    lane_c = jax.lax.broadcasted_iota(jnp.int32, (H, CHUNK_LANES), 1)
    row_c = jax.lax.broadcasted_iota(jnp.int32, (H, CHUNK_LANES), 0)
    r0 = r0 + jnp.where(lane_c == row_c, d_ref[...], 0.0)
    for r in range(SUBLANES):
        rr = pltpu.roll(r0, r * H, axis=1) if r else r0
        for a in range(T // SUBLANES):
            s = SUBLANES * a + r
            blk = pltpu.roll(rr, a * LANES, axis=1) if a else rr
            toep_ref[(T - 1 - s) * H:(T - s) * H, :] = jnp.where(lane_c >= s * H, blk, 0.0).astype(_bf16)

    u = u_ref[...].astype(_bf16)
    rows = u.shape[0]
    s_loc = _dot(u, ws)
    sm = _dot(um_ref[...].astype(_bf16), ws)[0:1, :]

    lvl_col = jax.lax.broadcasted_iota(jnp.int32, (SUBLANES, 1), 0)
    step, step_s = powers(jnp.left_shift(T, lvl_col).astype(_f32))
    a1 = jnp.where(lo, step, step_s)
    a2 = sgn * jnp.where(lo, step_s, step)

    def cmul(v, lvl):
        return a1[lvl:lvl + 1, :] * v + a2[lvl:lvl + 1, :] * pltpu.roll(v, P, axis=1)

    cidx = jax.lax.broadcasted_iota(jnp.int32, (rows, STATE_LANES), 0) % chunks_per_seq
    first = cidx == 0
    st = s_loc + jnp.where(first, cmul(jnp.broadcast_to(sm, s_loc.shape), 0), 0.0)
    for lvl in range(levels):
        shift = 1 << lvl
        prev = jnp.where(cidx >= shift, pltpu.roll(st, shift, axis=0), 0.0)
        st = st + cmul(prev, lvl)
    st_in = jnp.where(first, sm, pltpu.roll(st, 1, axis=0))
    y = _dot(u, toep_ref[...]) + _dot_nt(st_in.astype(_bf16), wct)
    y_ref[...] = jax.nn.gelu(y)


def _ssm(u_t, um_t, lam2, ldt, bt2, c2, d_col, chunks_per_seq):
    groups, rows, _ = u_t.shape
    levels = max(1, (chunks_per_seq - 1).bit_length())
    assert levels <= SUBLANES
    gps = SSM_GROUPS_PER_STEP
    per_group = lambda *s: pl.BlockSpec((gps,) + s, lambda g: (g,) + (0,) * len(s))
    return pl.pallas_call(
        functools.partial(_ssm_kernel, chunks_per_seq=chunks_per_seq, levels=levels,
                          groups_per_step=gps),
        grid=(groups // gps,),
        in_specs=[per_group(rows, CHUNK_LANES), per_group(um_t.shape[1], CHUNK_LANES),
                  per_group(2, STATE_LANES), per_group(1, 1),
                  per_group(2, SSM_GROUP, STATE_LANES), per_group(2, SSM_GROUP, STATE_LANES),
                  per_group(SSM_GROUP, 1)],
        out_specs=per_group(rows, CHUNK_LANES),
        out_shape=jax.ShapeDtypeStruct((groups, rows, CHUNK_LANES), _f32),
        scratch_shapes=[pltpu.VMEM((gps, CHUNK_LANES, CHUNK_LANES), _bf16)],
        compiler_params=pltpu.CompilerParams(dimension_semantics=("arbitrary",),
                                             vmem_limit_bytes=VMEM_LIMIT),
        name="ssm",
    )(u_t, um_t, lam2, ldt, bt2, c2, d_col)


def _ordered_after(x, dep):
    return jnp.where(dep != dep, dep, x)


def _conv_stage(z_ref, first_of_seq, zmeta_ref, dw_ref, dwb_ref, lng_ref, lnb_ref,
                zpad_ref, shift_ref, zs_ref):
    tile = z_ref.shape[0]
    padded = CONV_HALO + tile
    if first_of_seq is True:
        zpad_ref[0:CONV_HALO, :] = zmeta_ref[...]
    else:
        zpad_ref[0:CONV_HALO, :] = jnp.where(first_of_seq, zmeta_ref[...], zpad_ref[0:CONV_HALO, :])
    zpad_ref[CONV_HALO:, :] = z_ref[...].astype(_f32)
    base = CONV_HALO - (CONV_WIDTH - 1)
    convs = []
    dep = None
    for j in range(D_CONV // LANES):
        cols = slice(j * LANES, (j + 1) * LANES)
        zp = zpad_ref[:, cols]
        for r in range(1, SUBLANES):
            shift_ref[r - 1] = pltpu.roll(zp, padded - r, axis=0)
        pieces = []
        for p in range(tile // CONV_PIECE_ROWS):
            acc = None
            for k in range(CONV_WIDTH):
                a, r = divmod(base + k, SUBLANES)
                lo_row = a * SUBLANES + p * CONV_PIECE_ROWS
                rows = slice(lo_row, lo_row + CONV_PIECE_ROWS)
                src = zpad_ref[rows, cols] if r == 0 else shift_ref[r - 1, rows, :]
                if acc is None and dep is not None:
                    src = _ordered_after(src, dep)
                term = dw_ref[k:k + 1, cols] * src
                acc = term if acc is None else acc + term
            dep = acc
            pieces.append(acc + dwb_ref[:, cols])
        convs.append(jnp.concatenate(pieces, axis=0))
    conv = jnp.concatenate(convs, axis=1)
    zpad_ref[0:CONV_HALO, :] = zpad_ref[tile:tile + CONV_HALO, :]
    mu = jnp.mean(conv, axis=-1, keepdims=True)
    cen = conv - mu
    var = jnp.mean(cen * cen, axis=-1, keepdims=True)
    zn = cen * jax.lax.rsqrt(var + EPS) * lng_ref[...] + lnb_ref[...]
    zs_ref[...] = (zn * jax.nn.sigmoid(zn)).astype(_bf16)


def _token_major_stage(yt_ref, ys_ref):
    n_chunks = yt_ref.shape[1]
    for j in range(D_SSM // LANES):
        for q in range(SSM_CHUNK // SUBLANES):
            blks = [yt_ref[GROUPS_PER_VREG * j + a, :, q * LANES:(q + 1) * LANES]
                    for a in range(GROUPS_PER_VREG)]
            for r, rows in enumerate(_block_transpose8(blks)):
                ys_ref[j, pl.ds(SUBLANES * q + r, n_chunks, stride=SSM_CHUNK), :] = rows


def _back_kernel(h1_ref, z0_ref, zn_ref, yt0_ref, ytn_ref, gate_ref, zmeta_ref, dw_ref, dwb_ref,
                 lng_ref, lnb_ref, wcp_ref, wv_ref, wg_ref, wout_ref, g3_ref, w1_ref, w3_ref, w2_ref,
                 gf_ref, out_ref, zpad_ref, shift_ref, zs_ref, ys_ref, *, tiles_per_seq):
    n = pl.program_id(0)
    conv_args = (zmeta_ref, dw_ref, dwb_ref, lng_ref, lnb_ref, zpad_ref, shift_ref, zs_ref)

    @pl.when(n == 0)
    def _():
        _conv_stage(z0_ref, True, *conv_args)
        _token_major_stage(yt0_ref, ys_ref)

    zs = zs_ref[...]
    ys = jnp.concatenate([ys_ref[j] for j in range(D_SSM // LANES)], axis=1).astype(_bf16)
    y_conv = _dot(zs, wcp_ref[...])
    y_ssm = _dot(ys, wv_ref[...]) * jax.nn.sigmoid(_dot(ys, wg_ref[...]))

    _conv_stage(zn_ref, (n + 1) % tiles_per_seq == 0, *conv_args)
    _token_major_stage(ytn_ref, ys_ref)

    mix = (gate_ref[:, 0:D_MODEL].astype(_f32) * y_conv
           + gate_ref[:, D_MODEL:].astype(_f32) * y_ssm).astype(_bf16)
    h2 = h1_ref[...] + _dot(mix, wout_ref[...])
    nrm = _rms_norm(h2, g3_ref[...]).astype(_bf16)
    h3 = h2 + 0.5 * _swiglu(nrm, w1_ref, w3_ref, w2_ref)
    out_ref[...] = _rms_norm(h3, gf_ref[...])


def _back(h1, z, y_t, gate, zmeta, dw, dwb, lng, lnb, wcp, wv, wg, wout, g3, w1, w3, w2, gf,
          batch, seq, tile):
    tiles = batch * seq // tile
    nxt = lambda n: jnp.minimum(n + 1, tiles - 1)
    row = lambda w: pl.BlockSpec((tile, w), lambda n: (n, 0))
    yt_block = (N_SSM_GROUPS, tile // SSM_CHUNK, CHUNK_LANES)
    return pl.pallas_call(
        functools.partial(_back_kernel, tiles_per_seq=seq // tile),
        grid=(tiles,),
        in_specs=[row(D_MODEL),
                  pl.BlockSpec((tile, D_CONV), lambda n: (0, 0)),
                  pl.BlockSpec((tile, D_CONV), lambda n: (nxt(n), 0)),
                  pl.BlockSpec(yt_block, lambda n: (0, 0, 0)),
                  pl.BlockSpec(yt_block, lambda n: (0, nxt(n), 0)),
                  row(2 * D_MODEL),
                  _resident((CONV_HALO, D_CONV)), _resident((CONV_WIDTH, D_CONV)),
                  _resident((1, D_CONV)), _resident((1, D_CONV)), _resident((1, D_CONV)),
                  _resident((D_CONV, D_MODEL)), _resident((D_SSM, D_MODEL)),
                  _resident((D_SSM, D_MODEL)), _resident((D_MODEL, D_MODEL)),
                  _resident((1, D_MODEL)), _resident((D_MODEL, D_FF)), _resident((D_MODEL, D_FF)),
                  _resident((D_FF, D_MODEL)), _resident((1, D_MODEL))],
        out_specs=row(D_MODEL),
        out_shape=jax.ShapeDtypeStruct((batch * seq, D_MODEL), _f32),
        scratch_shapes=[pltpu.VMEM((CONV_HALO + tile, D_CONV), _f32),
                        pltpu.VMEM((SUBLANES - 1, CONV_HALO + tile, LANES), _f32),
                        pltpu.VMEM((tile, D_CONV), _bf16),
                        pltpu.VMEM((D_SSM // LANES, tile, LANES), _f32)],
        compiler_params=pltpu.CompilerParams(dimension_semantics=("arbitrary",),
                                             vmem_limit_bytes=VMEM_LIMIT),
        name="back",
    )(h1, z, z, y_t, y_t, gate, zmeta, dw, dwb, lng, lnb, wcp, wv, wg, wout, g3, w1, w3, w2, gf)


def kernel(x, meta_tokens, ffn1_norm, ffn1_w1, ffn1_w3, ffn1_w2, mix_norm, w_in, b_gate, conv_dw, conv_dw_b, conv_ln_g, conv_ln_b, conv_proj, ssm_lam_re, ssm_lam_im, ssm_log_dt, ssm_b_re, ssm_b_im, ssm_c_re, ssm_c_im, ssm_d, ssm_w_v, ssm_w_g, w_out, ffn2_norm, ffn2_w1, ffn2_w3, ffn2_w2, final_norm):
    batch, seq, _ = x.shape
    assert ffn1_norm.shape[0] == 1 and seq % ROW_TILE == 0 and ROW_TILE % (SUBLANES * SSM_CHUNK) == 0
    assert N_META <= SSM_CHUNK and N_META <= CONV_HALO
    T, H, G = SSM_CHUNK, SSM_GROUP, N_SSM_GROUPS
    row = lambda v: v.reshape(1, -1)
    bf = lambda w: w[0].astype(_bf16)
    dup = lambda a: jnp.concatenate([a, a], axis=-1)

    front_w = (row(ffn1_norm[0]), bf(ffn1_w1), bf(ffn1_w3), bf(ffn1_w2), row(mix_norm[0]),
               bf(w_in), row(b_gate[0]))
    h1, z, u_t, gate = _front(x.reshape(batch * seq, D_MODEL), *front_w, tile=ROW_TILE,
                              group_major=True)
    meta_chunk = jnp.zeros((T, D_MODEL), _f32).at[T - N_META:].set(meta_tokens)
    _, z_m, u_m, _ = _front(meta_chunk, *front_w, tile=T, group_major=False)
    um_t = u_m[::-1].reshape(T, G, H).transpose(1, 0, 2).reshape(G, 1, T * H)
    um_t = jnp.concatenate([um_t, jnp.zeros((G, 2 * SUBLANES - 1, T * H), _f32)], axis=1)

    lam2 = jnp.stack([dup(ssm_lam_re[0]), dup(ssm_lam_im[0])], axis=1)
    bt2 = jnp.stack([dup(ssm_b_re[0].transpose(0, 2, 1)), dup(ssm_b_im[0].transpose(0, 2, 1))], axis=1)
    c2 = jnp.stack([dup(ssm_c_re[0]), dup(ssm_c_im[0])], axis=1)
    y_t = _ssm(u_t, um_t, lam2, ssm_log_dt[0].reshape(G, 1, 1), bt2, c2,
               ssm_d[0].reshape(G, H, 1), seq // T)

    z_halo = z_m[T - CONV_HALO:].astype(_f32)
    out = _back(h1, z, y_t, gate, z_halo, conv_dw[0], row(conv_dw_b[0]), row(conv_ln_g[0]),
                row(conv_ln_b[0]), bf(conv_proj), bf(ssm_w_v), bf(ssm_w_g), bf(w_out),
                row(ffn2_norm[0]), bf(ffn2_w1), bf(ffn2_w3), bf(ffn2_w2), row(final_norm),
                batch, seq, ROW_TILE)
    return out.reshape(batch, seq, D_MODEL)
```

```python
import functools

import jax
import jax.numpy as jnp
from jax.experimental import pallas as pl
from jax.experimental.pallas import tpu as pltpu

D_MODEL = 1024
N_META = 16
D_FF = 2816
D_CONV = 512
CONV_WIDTH = 31
D_SSM = 512
SSM_GROUP = 16
N_SSM_GROUPS = D_SSM // SSM_GROUP
SSM_STATE = 64
EPS = 1e-6

LANES = 128
SUBLANES = 8
SSM_CHUNK = 64
CHUNK_LANES = SSM_CHUNK * SSM_GROUP
STATE_LANES = 2 * SSM_STATE
GROUPS_PER_VREG = LANES // SSM_GROUP
CONV_HALO = 32
CONV_PIECE_ROWS = 16
SSM_GROUPS_PER_STEP = 2
CHUNK_PITCH = SSM_CHUNK + SUBLANES
FF_SPLITS = ((0, 1024), (1024, 2048), (2048, 2816))
ROW_TILE = 512
VMEM_LIMIT = 60 * 1024 * 1024

assert STATE_LANES == LANES and GROUPS_PER_VREG == SUBLANES

_bf16 = jnp.bfloat16
_f32 = jnp.float32


def _dot(a, b):
    return jnp.dot(a, b, preferred_element_type=_f32)


def _dot_nt(a, b, precision=None):
    return jax.lax.dot_general(a, b, (((1,), (1,)), ((), ())), precision=precision,
                               preferred_element_type=_f32)


def _rms_norm(x, g):
    return x * jax.lax.rsqrt(jnp.mean(x * x, axis=-1, keepdims=True) + EPS) * g


def _swiglu(n, w1_ref, w3_ref, w2_ref):
    acc = None
    for lo, hi in FF_SPLITS:
        a = _dot(n, w1_ref[:, lo:hi])
        b = _dot(n, w3_ref[:, lo:hi])
        f = (a * jax.nn.sigmoid(a) * b).astype(_bf16)
        part = _dot(f, w2_ref[lo:hi, :])
        acc = part if acc is None else acc + part
    return acc


def _block_transpose8(v):
    blk = jax.lax.broadcasted_iota(jnp.int32, v[0].shape, 1) // SSM_GROUP
    for d in (4, 2, 1):
        hi = (blk & d) != 0
        shift = d * SSM_GROUP
        new = list(v)
        for i in range(GROUPS_PER_VREG):
            if i & d:
                continue
            new[i] = jnp.where(hi, pltpu.roll(v[i + d], shift, axis=1), v[i])
            new[i + d] = jnp.where(hi, v[i + d], pltpu.roll(v[i], LANES - shift, axis=1))
        v = new
    return v


def _front_kernel(x_ref, g1_ref, w1_ref, w3_ref, w2_ref, g2_ref, win_ref, bg_ref,
                  h1_ref, z_ref, u_ref, gate_ref, *scratch, group_major):
    x = x_ref[...]
    n = _rms_norm(x, g1_ref[...]).astype(_bf16)
    h1 = x + 0.5 * _swiglu(n, w1_ref, w3_ref, w2_ref)
    h1_ref[...] = h1
    u = _rms_norm(h1, g2_ref[...]).astype(_bf16)
    v = _dot(u, win_ref[:, 0:D_CONV])
    g = _dot(u, win_ref[:, D_CONV:2 * D_CONV])
    z_ref[...] = (v * jax.nn.sigmoid(g)).astype(_bf16)
    us = _dot(u, win_ref[:, 2 * D_CONV:2 * D_CONV + D_SSM])
    if group_major:
        us_ref, = scratch
        n_chunks = us.shape[0] // SSM_CHUNK
        for j in range(D_SSM // LANES):
            for c in range(n_chunks):
                us_ref[j, c * CHUNK_PITCH:c * CHUNK_PITCH + SSM_CHUNK, :] = (
                    us[c * SSM_CHUNK:(c + 1) * SSM_CHUNK, j * LANES:(j + 1) * LANES])
            for q in range(SSM_CHUNK // SUBLANES):
                rows = [us_ref[j, pl.ds(SSM_CHUNK - 1 - (SUBLANES * q + r), n_chunks, stride=CHUNK_PITCH), :]
                        for r in range(SUBLANES)]
                for a, blk in enumerate(_block_transpose8(rows)):
                    u_ref[GROUPS_PER_VREG * j + a, :, q * LANES:(q + 1) * LANES] = blk
    else:
        u_ref[...] = us
    gi = _dot(u, win_ref[:, 2 * D_CONV + D_SSM:]) + bg_ref[...]
    gate_ref[...] = jax.nn.sigmoid(gi).astype(_bf16)


def _resident(shape):
    return pl.BlockSpec(shape, lambda *_: (0,) * len(shape), pipeline_mode=pl.Buffered(1))


def _front(x2d, g1, w1, w3, w2, g2, win, bg, tile, group_major):
    rows = x2d.shape[0]
    d_in = win.shape[1]
    n_gate = d_in - 2 * D_CONV - D_SSM
    row = lambda w: pl.BlockSpec((tile, w), lambda i: (i, 0))
    if group_major:
        u_spec = pl.BlockSpec((N_SSM_GROUPS, tile // SSM_CHUNK, CHUNK_LANES), lambda i: (0, i, 0))
        u_shape = jax.ShapeDtypeStruct((N_SSM_GROUPS, rows // SSM_CHUNK, CHUNK_LANES), _f32)
        scratch = [pltpu.VMEM((D_SSM // LANES, tile // SSM_CHUNK * CHUNK_PITCH, LANES), _f32)]
    else:
        u_spec, u_shape, scratch = row(D_SSM), jax.ShapeDtypeStruct((rows, D_SSM), _f32), []
    return pl.pallas_call(
        functools.partial(_front_kernel, group_major=group_major),
        grid=(rows // tile,),
        in_specs=[row(D_MODEL), _resident((1, D_MODEL)), _resident((D_MODEL, D_FF)),
                  _resident((D_MODEL, D_FF)), _resident((D_FF, D_MODEL)),
                  _resident((1, D_MODEL)), _resident((D_MODEL, d_in)), _resident((1, n_gate))],
        out_specs=[row(D_MODEL), row(D_CONV), u_spec, row(n_gate)],
        out_shape=[jax.ShapeDtypeStruct((rows, D_MODEL), _f32),
                   jax.ShapeDtypeStruct((rows, D_CONV), _bf16),
                   u_shape,
                   jax.ShapeDtypeStruct((rows, n_gate), _bf16)],
        scratch_shapes=scratch,
        compiler_params=pltpu.CompilerParams(dimension_semantics=("arbitrary",),
                                             vmem_limit_bytes=VMEM_LIMIT),
        name="front" if group_major else "front_meta",
    )(x2d, g1, w1, w3, w2, g2, win, bg)


def _ssm_kernel(*refs, chunks_per_seq, levels, groups_per_step, n_cast):
    n_in = len(refs) - 2 * n_cast - 2
    ins, w_f32 = refs[:n_in], refs[n_in:n_in + n_cast]
    y_ref, w_bf16, toep_ref = refs[n_in + n_cast], refs[n_in + n_cast + 1:-1], refs[-1]
    for src, dst in zip(w_f32, w_bf16):
        dst[...] = src[...].astype(_bf16)
    for gi in range(groups_per_step):
        _ssm_group(*[r.at[gi] for r in (*ins, y_ref, toep_ref)],
                   chunks_per_seq=chunks_per_seq, levels=levels)


def _ssm_group(u_ref, um_ref, lam_ref, ldt_ref, bt_ref, c_ref, d_ref, y_ref, toep_ref,
               *, chunks_per_seq, levels):
    T, H, P = SSM_CHUNK, SSM_GROUP, SSM_STATE
    lane = jax.lax.broadcasted_iota(jnp.int32, (1, STATE_LANES), 1)
    lo = lane < P
    sgn = jnp.where(lo, -1.0, 1.0)
    lam_re = lam_ref[0:1, :]
    lam_im = lam_ref[1:2, :]
    dt = jnp.exp(ldt_ref[...])
    ar, ai = lam_re * dt, lam_im * dt

    def powers(k):
        mag, ang = jnp.exp(ar * k), ai * k
        cs, sn = jnp.cos(ang), jnp.sin(ang)
        return mag * jnp.where(lo, cs, sn), mag * jnp.where(lo, sn, cs)

    def cmul3(m1, m2, pw, pws):
        return (m1[None] * pw[:, None, :] + m2[None] * pws[:, None, :]).reshape(T * H, STATE_LANES)

    def cmul_rows(v, vs, m1, m2):
        return v * m1 + vs * m2, vs * m1 - v * m2

    n_dbl = (T // SUBLANES).bit_length() - 1
    assert SUBLANES << n_dbl == T and n_dbl < SUBLANES
    row8 = jax.lax.broadcasted_iota(jnp.int32, (SUBLANES, 1), 0)
    aux, aux_s = powers(jnp.where(row8 < n_dbl, jnp.left_shift(SUBLANES, row8),
                                  jnp.where(row8 == n_dbl, 1, 0)).astype(_f32))
    x1 = jnp.where(lo, aux, aux_s)
    x2 = sgn * jnp.where(lo, aux_s, aux)
    pw, pws = powers(row8.astype(_f32))
    for i in range(n_dbl):
        nxt, nxt_s = cmul_rows(pw, pws, x1[i:i + 1], x2[i:i + 1])
        pw, pws = jnp.concatenate([pw, nxt], axis=0), jnp.concatenate([pws, nxt_s], axis=0)
    pw1, pw1s = cmul_rows(pw, pws, x1[n_dbl:n_dbl + 1], x2[n_dbl:n_dbl + 1])

    lb_re, lb_im = x1[n_dbl:n_dbl + 1], sgn * x2[n_dbl:n_dbl + 1]
    den = lam_re * lam_re + lam_im * lam_im
    co_re = ((lb_re - 1.0) * lam_re + lb_im * lam_im) / den
    co_im = (lb_im * lam_re - (lb_re - 1.0) * lam_im) / den
    b_re, b_im = bt_ref[0], bt_ref[1]
    bb_re = co_re * b_re - co_im * b_im
    bb_im = co_re * b_im + co_im * b_re
    c1, c2 = c_ref[0], sgn * c_ref[1]

    ws = cmul3(bb_re, sgn * bb_im, pw, pws).astype(_bf16)
    wct = (cmul3(c1, c2, pw1, pw1s) * (-sgn)).astype(_bf16)
    e = cmul3(c1, c2, pw, pws)
    r0 = _dot_nt(jnp.where(lo, bb_re, -bb_im), e, precision=jax.lax.Precision.HIGHEST)
    lane_c = jax.lax.broadcasted_iota(jnp.int32, (H, CHUNK_LANES), 1)
    row_c = jax.lax.broadcasted_iota(jnp.int32, (H, CHUNK_LANES), 0)
    r0 = r0 + jnp.where(lane_c == row_c, d_ref[...], 0.0)
    for r in range(SUBLANES):
        rr = pltpu.roll(r0, r * H, axis=1) if r else r0
        for a in range(T // SUBLANES):
            s = SUBLANES * a + r
            blk = pltpu.roll(rr, a * LANES, axis=1) if a else rr
            toep_ref[(T - 1 - s) * H:(T - s) * H, :] = jnp.where(lane_c >= s * H, blk, 0.0).astype(_bf16)

    u = u_ref[...].astype(_bf16)
    rows = u.shape[0]
    s_loc = _dot(u, ws)
    sm = _dot(um_ref[...].astype(_bf16), ws)[0:1, :]

    lvl_col = jax.lax.broadcasted_iota(jnp.int32, (SUBLANES, 1), 0)
    step, step_s = powers(jnp.left_shift(T, lvl_col).astype(_f32))
    a1 = jnp.where(lo, step, step_s)
    a2 = sgn * jnp.where(lo, step_s, step)

    def cmul(v, lvl):
        return a1[lvl:lvl + 1, :] * v + a2[lvl:lvl + 1, :] * pltpu.roll(v, P, axis=1)

    cidx = jax.lax.broadcasted_iota(jnp.int32, (rows, STATE_LANES), 0) % chunks_per_seq
    first = cidx == 0
    st = s_loc + jnp.where(first, cmul(jnp.broadcast_to(sm, s_loc.shape), 0), 0.0)
    for lvl in range(levels):
        shift = 1 << lvl
        prev = jnp.where(cidx >= shift, pltpu.roll(st, shift, axis=0), 0.0)
        st = st + cmul(prev, lvl)
    st_in = jnp.where(first, sm, pltpu.roll(st, 1, axis=0))
    y = _dot(u, toep_ref[...]) + _dot_nt(st_in.astype(_bf16), wct)
    y_ref[...] = jax.nn.gelu(y)


def _ssm(u_t, um_t, lam2, ldt, bt2, c2, d_col, chunks_per_seq, weights):
    groups, rows, _ = u_t.shape
    levels = max(1, (chunks_per_seq - 1).bit_length())
    assert levels <= SUBLANES
    gps = SSM_GROUPS_PER_STEP
    steps = groups // gps
    per_group = lambda *s: pl.BlockSpec((gps,) + s, lambda g: (g,) + (0,) * len(s))
    row_block = lambda w: pl.BlockSpec((w.shape[0] // steps, w.shape[1]), lambda g: (g, 0))
    assert all(w.shape[0] % (steps * 2 * SUBLANES) == 0 for w in weights)
    outs = pl.pallas_call(
        functools.partial(_ssm_kernel, chunks_per_seq=chunks_per_seq, levels=levels,
                          groups_per_step=gps, n_cast=len(weights)),
        grid=(steps,),
        in_specs=[per_group(rows, CHUNK_LANES), per_group(um_t.shape[1], CHUNK_LANES),
                  per_group(2, STATE_LANES), per_group(1, 1),
                  per_group(2, SSM_GROUP, STATE_LANES), per_group(2, SSM_GROUP, STATE_LANES),
                  per_group(SSM_GROUP, 1)] + [row_block(w) for w in weights],
        out_specs=[per_group(rows, CHUNK_LANES)] + [row_block(w) for w in weights],
        out_shape=[jax.ShapeDtypeStruct((groups, rows, CHUNK_LANES), _f32)]
                  + [jax.ShapeDtypeStruct(w.shape, _bf16) for w in weights],
        scratch_shapes=[pltpu.VMEM((gps, CHUNK_LANES, CHUNK_LANES), _bf16)],
        compiler_params=pltpu.CompilerParams(dimension_semantics=("arbitrary",),
                                             vmem_limit_bytes=VMEM_LIMIT),
        name="ssm",
    )(u_t, um_t, lam2, ldt, bt2, c2, d_col, *weights)
    return outs[0], outs[1:]


def _ordered_after(x, dep):
    return jnp.where(dep != dep, dep, x)


def _conv_stage(z_ref, first_of_seq, zmeta_ref, dw_ref, dwb_ref, lng_ref, lnb_ref,
                zpad_ref, shift_ref, zs_ref):
    tile = z_ref.shape[0]
    padded = CONV_HALO + tile
    if first_of_seq is True:
        zpad_ref[0:CONV_HALO, :] = zmeta_ref[...]
    else:
        zpad_ref[0:CONV_HALO, :] = jnp.where(first_of_seq, zmeta_ref[...], zpad_ref[0:CONV_HALO, :])
    zpad_ref[CONV_HALO:, :] = z_ref[...].astype(_f32)
    base = CONV_HALO - (CONV_WIDTH - 1)
    convs = []
    dep = None
    for j in range(D_CONV // LANES):
        cols = slice(j * LANES, (j + 1) * LANES)
        zp = zpad_ref[:, cols]
        for r in range(1, SUBLANES):
            shift_ref[r - 1] = pltpu.roll(zp, padded - r, axis=0)
        pieces = []
        for p in range(tile // CONV_PIECE_ROWS):
            acc = None
            for k in range(CONV_WIDTH):
                a, r = divmod(base + k, SUBLANES)
                lo_row = a * SUBLANES + p * CONV_PIECE_ROWS
                rows = slice(lo_row, lo_row + CONV_PIECE_ROWS)
                src = zpad_ref[rows, cols] if r == 0 else shift_ref[r - 1, rows, :]
                if acc is None and dep is not None:
                    src = _ordered_after(src, dep)
                term = dw_ref[k:k + 1, cols] * src
                acc = term if acc is None else acc + term
            dep = acc
            pieces.append(acc + dwb_ref[:, cols])
        convs.append(jnp.concatenate(pieces, axis=0))
    conv = jnp.concatenate(convs, axis=1)
    zpad_ref[0:CONV_HALO, :] = zpad_ref[tile:tile + CONV_HALO, :]
    mu = jnp.mean(conv, axis=-1, keepdims=True)
    cen = conv - mu
    var = jnp.mean(cen * cen, axis=-1, keepdims=True)
    zn = cen * jax.lax.rsqrt(var + EPS) * lng_ref[...] + lnb_ref[...]
    zs_ref[...] = (zn * jax.nn.sigmoid(zn)).astype(_bf16)


def _token_major_stage(yt_ref, ys_ref):
    n_chunks = yt_ref.shape[1]
    for j in range(D_SSM // LANES):
        for q in range(SSM_CHUNK // SUBLANES):
            blks = [yt_ref[GROUPS_PER_VREG * j + a, :, q * LANES:(q + 1) * LANES]
                    for a in range(GROUPS_PER_VREG)]
            for r, rows in enumerate(_block_transpose8(blks)):
                ys_ref[j, pl.ds(SUBLANES * q + r, n_chunks, stride=CHUNK_PITCH), :] = rows


def _back_kernel(h1_ref, z0_ref, zn_ref, yt0_ref, ytn_ref, gate_ref, zmeta_ref, dw_ref, dwb_ref,
                 lng_ref, lnb_ref, wcp_ref, wv_ref, wg_ref, wout_ref, g3_ref, w1_ref, w3_ref, w2_ref,
                 gf_ref, out_ref, zpad_ref, shift_ref, zs_ref, ys_ref, *, tiles_per_seq):
    n = pl.program_id(0)
    conv_args = (zmeta_ref, dw_ref, dwb_ref, lng_ref, lnb_ref, zpad_ref, shift_ref, zs_ref)

    @pl.when(n == 0)
    def _():
        _conv_stage(z0_ref, True, *conv_args)
        _token_major_stage(yt0_ref, ys_ref)

    zs = zs_ref[...]
    ys = jnp.concatenate(
        [jnp.concatenate([ys_ref[j, c * CHUNK_PITCH:c * CHUNK_PITCH + SSM_CHUNK, :]
                          for c in range(h1_ref.shape[0] // SSM_CHUNK)], axis=0)
         for j in range(D_SSM // LANES)], axis=1).astype(_bf16)
    y_conv = _dot(zs, wcp_ref[...])
    y_ssm = _dot(ys, wv_ref[...]) * jax.nn.sigmoid(_dot(ys, wg_ref[...]))

    _conv_stage(zn_ref, (n + 1) % tiles_per_seq == 0, *conv_args)
    _token_major_stage(ytn_ref, ys_ref)

    mix = (gate_ref[:, 0:D_MODEL].astype(_f32) * y_conv
           + gate_ref[:, D_MODEL:].astype(_f32) * y_ssm).astype(_bf16)
    h2 = h1_ref[...] + _dot(mix, wout_ref[...])
    nrm = _rms_norm(h2, g3_ref[...]).astype(_bf16)
    h3 = h2 + 0.5 * _swiglu(nrm, w1_ref, w3_ref, w2_ref)
    out_ref[...] = _rms_norm(h3, gf_ref[...])


def _back(h1, z, y_t, gate, zmeta, dw, dwb, lng, lnb, wcp, wv, wg, wout, g3, w1, w3, w2, gf,
          batch, seq, tile):
    tiles = batch * seq // tile
    nxt = lambda n: jnp.minimum(n + 1, tiles - 1)
    row = lambda w: pl.BlockSpec((tile, w), lambda n: (n, 0))
    yt_block = (N_SSM_GROUPS, tile // SSM_CHUNK, CHUNK_LANES)
    return pl.pallas_call(
        functools.partial(_back_kernel, tiles_per_seq=seq // tile),
        grid=(tiles,),
        in_specs=[row(D_MODEL),
                  pl.BlockSpec((tile, D_CONV), lambda n: (0, 0)),
                  pl.BlockSpec((tile, D_CONV), lambda n: (nxt(n), 0)),
                  pl.BlockSpec(yt_block, lambda n: (0, 0, 0)),
                  pl.BlockSpec(yt_block, lambda n: (0, nxt(n), 0)),
                  row(2 * D_MODEL),
                  _resident((CONV_HALO, D_CONV)), _resident((CONV_WIDTH, D_CONV)),
                  _resident((1, D_CONV)), _resident((1, D_CONV)), _resident((1, D_CONV)),
                  _resident((D_CONV, D_MODEL)), _resident((D_SSM, D_MODEL)),
                  _resident((D_SSM, D_MODEL)), _resident((D_MODEL, D_MODEL)),
                  _resident((1, D_MODEL)), _resident((D_MODEL, D_FF)), _resident((D_MODEL, D_FF)),
                  _resident((D_FF, D_MODEL)), _resident((1, D_MODEL))],
        out_specs=row(D_MODEL),
        out_shape=jax.ShapeDtypeStruct((batch * seq, D_MODEL), _f32),
        scratch_shapes=[pltpu.VMEM((CONV_HALO + tile, D_CONV), _f32),
                        pltpu.VMEM((SUBLANES - 1, CONV_HALO + tile, LANES), _f32),
                        pltpu.VMEM((tile, D_CONV), _bf16),
                        pltpu.VMEM((D_SSM // LANES, tile // SSM_CHUNK * CHUNK_PITCH, LANES), _f32)],
        compiler_params=pltpu.CompilerParams(dimension_semantics=("arbitrary",),
                                             vmem_limit_bytes=VMEM_LIMIT),
        name="back",
    )(h1, z, z, y_t, y_t, gate, zmeta, dw, dwb, lng, lnb, wcp, wv, wg, wout, g3, w1, w3, w2, gf)


def kernel(x, meta_tokens, ffn1_norm, ffn1_w1, ffn1_w3, ffn1_w2, mix_norm, w_in, b_gate, conv_dw, conv_dw_b, conv_ln_g, conv_ln_b, conv_proj, ssm_lam_re, ssm_lam_im, ssm_log_dt, ssm_b_re, ssm_b_im, ssm_c_re, ssm_c_im, ssm_d, ssm_w_v, ssm_w_g, w_out, ffn2_norm, ffn2_w1, ffn2_w3, ffn2_w2, final_norm):
    batch, seq, _ = x.shape
    assert ffn1_norm.shape[0] == 1 and seq % ROW_TILE == 0 and ROW_TILE % (SUBLANES * SSM_CHUNK) == 0
    assert N_META <= SSM_CHUNK and N_META <= CONV_HALO
    T, H, G = SSM_CHUNK, SSM_GROUP, N_SSM_GROUPS
    row = lambda v: v.reshape(1, -1)
    bf = lambda w: w[0].astype(_bf16)
    dup = lambda a: jnp.concatenate([a, a], axis=-1)

    front_w = (row(ffn1_norm[0]), bf(ffn1_w1), bf(ffn1_w3), bf(ffn1_w2), row(mix_norm[0]),
               bf(w_in), row(b_gate[0]))
    h1, z, u_t, gate = _front(x.reshape(batch * seq, D_MODEL), *front_w, tile=ROW_TILE,
                              group_major=True)
    meta_chunk = jnp.zeros((T, D_MODEL), _f32).at[T - N_META:].set(meta_tokens)
    _, z_m, u_m, _ = _front(meta_chunk, *front_w, tile=T, group_major=False)
    um_t = u_m[::-1].reshape(T, G, H).transpose(1, 0, 2).reshape(G, 1, T * H)
    um_t = jnp.concatenate([um_t, jnp.zeros((G, 2 * SUBLANES - 1, T * H), _f32)], axis=1)

    lam2 = jnp.stack([dup(ssm_lam_re[0]), dup(ssm_lam_im[0])], axis=1)
    bt2 = jnp.stack([dup(ssm_b_re[0].transpose(0, 2, 1)), dup(ssm_b_im[0].transpose(0, 2, 1))], axis=1)
    c2 = jnp.stack([dup(ssm_c_re[0]), dup(ssm_c_im[0])], axis=1)
    back_w = (conv_proj[0], ssm_w_v[0], ssm_w_g[0], w_out[0], ffn2_w1[0], ffn2_w3[0], ffn2_w2[0])
    y_t, (wcp, wv, wg, wo, w1b, w3b, w2b) = _ssm(
        u_t, um_t, lam2, ssm_log_dt[0].reshape(G, 1, 1), bt2, c2, ssm_d[0].reshape(G, H, 1),
        seq // T, back_w)

    z_halo = z_m[T - CONV_HALO:].astype(_f32)
    out = _back(h1, z, y_t, gate, z_halo, conv_dw[0], row(conv_dw_b[0]), row(conv_ln_g[0]),
                row(conv_ln_b[0]), wcp, wv, wg, wo, row(ffn2_norm[0]), w1b, w3b, w2b,
                row(final_norm), batch, seq, ROW_TILE)
    return out.reshape(batch, seq, D_MODEL)
```

```python
import functools

import jax
import jax.numpy as jnp
from jax.experimental import pallas as pl
from jax.experimental.pallas import tpu as pltpu

D_MODEL = 1024
N_META = 16
D_FF = 2816
D_CONV = 512
CONV_WIDTH = 31
D_SSM = 512
SSM_GROUP = 16
N_SSM_GROUPS = D_SSM // SSM_GROUP
SSM_STATE = 64
EPS = 1e-6

LANES = 128
SUBLANES = 8
SSM_CHUNK = 64
CHUNK_LANES = SSM_CHUNK * SSM_GROUP
STATE_LANES = 2 * SSM_STATE
GROUPS_PER_VREG = LANES // SSM_GROUP
CONV_HALO = 32
CONV_PIECE_ROWS = 16
WEIGHT_LOAD_CHUNKS = 16
SSM_GROUPS_PER_STEP = 2
CHUNK_PITCH = SSM_CHUNK + SUBLANES
FF_SPLITS = ((0, 1024), (1024, 2048), (2048, 2816))
ROW_TILE = 512
VMEM_LIMIT = 60 * 1024 * 1024

assert STATE_LANES == LANES and GROUPS_PER_VREG == SUBLANES

_bf16 = jnp.bfloat16
_f32 = jnp.float32


def _dot(a, b):
    return jnp.dot(a, b, preferred_element_type=_f32)


def _dot_nt(a, b, precision=None):
    return jax.lax.dot_general(a, b, (((1,), (1,)), ((), ())), precision=precision,
                               preferred_element_type=_f32)


def _rms_norm(x, g):
    return x * jax.lax.rsqrt(jnp.mean(x * x, axis=-1, keepdims=True) + EPS) * g


def _swiglu(n, w1_ref, w3_ref, w2_ref):
    acc = None
    for lo, hi in FF_SPLITS:
        a = _dot(n, w1_ref[:, lo:hi])
        b = _dot(n, w3_ref[:, lo:hi])
        f = (a * jax.nn.sigmoid(a) * b).astype(_bf16)
        part = _dot(f, w2_ref[lo:hi, :])
        acc = part if acc is None else acc + part
    return acc


def _block_transpose8(v):
    blk = jax.lax.broadcasted_iota(jnp.int32, v[0].shape, 1) // SSM_GROUP
    for d in (4, 2, 1):
        hi = (blk & d) != 0
        shift = d * SSM_GROUP
        new = list(v)
        for i in range(GROUPS_PER_VREG):
            if i & d:
                continue
            new[i] = jnp.where(hi, pltpu.roll(v[i + d], shift, axis=1), v[i])
            new[i + d] = jnp.where(hi, v[i + d], pltpu.roll(v[i], LANES - shift, axis=1))
        v = new
    return v


def _load_narrowed(w_hbm, w_vmem, stage_ref, sem):
    chunk = stage_ref.shape[1]
    n_chunks = w_hbm.shape[0] // chunk
    assert n_chunks * chunk == w_hbm.shape[0]

    def copy(c):
        return pltpu.make_async_copy(w_hbm.at[pl.ds(c * chunk, chunk), :], stage_ref.at[c % 2],
                                     sem.at[c % 2])

    copy(0).start()
    for c in range(n_chunks):
        if c + 1 < n_chunks:
            copy(c + 1).start()
        copy(c).wait()
        w_vmem[c * chunk:(c + 1) * chunk, :] = stage_ref[c % 2].astype(_bf16)


def _front_rows(x, g1_ref, g2_ref, w1_ref, w3_ref, w2_ref, win_ref):
    n = _rms_norm(x, g1_ref[...]).astype(_bf16)
    h1 = x + 0.5 * _swiglu(n, w1_ref, w3_ref, w2_ref)
    u = _rms_norm(h1, g2_ref[...]).astype(_bf16)
    v = _dot(u, win_ref[:, 0:D_CONV])
    g = _dot(u, win_ref[:, D_CONV:2 * D_CONV])
    z = (v * jax.nn.sigmoid(g)).astype(_bf16)
    us = _dot(u, win_ref[:, 2 * D_CONV:2 * D_CONV + D_SSM])
    return h1, z, us, u


def _front_kernel(x_ref, meta_ref, g1_ref, g2_ref, bg_ref, w1_hbm, w3_hbm, w2_hbm, win_hbm,
                  h1_ref, z_ref, u_ref, gate_ref, zm_ref, um_ref,
                  us_ref, w1_ref, w3_ref, w2_ref, win_ref, stage13_ref, stage2_ref, stagein_ref, sem):
    weights = (g1_ref, g2_ref, w1_ref, w3_ref, w2_ref, win_ref)

    @pl.when(pl.program_id(0) == 0)
    def _():
        _load_narrowed(w1_hbm, w1_ref, stage13_ref, sem.at[0])
        _load_narrowed(w3_hbm, w3_ref, stage13_ref, sem.at[1])
        _load_narrowed(w2_hbm, w2_ref, stage2_ref, sem.at[2])
        _load_narrowed(win_hbm, win_ref, stagein_ref, sem.at[3])
        _, zm, usm, _ = _front_rows(meta_ref[...], *weights)
        zm_ref[...] = zm
        um_ref[...] = usm

    h1, z, us, u = _front_rows(x_ref[...], *weights)
    h1_ref[...] = h1
    z_ref[...] = z
    n_chunks = us.shape[0] // SSM_CHUNK
    for j in range(D_SSM // LANES):
        for c in range(n_chunks):
            us_ref[j, c * CHUNK_PITCH:c * CHUNK_PITCH + SSM_CHUNK, :] = (
                us[c * SSM_CHUNK:(c + 1) * SSM_CHUNK, j * LANES:(j + 1) * LANES])
        for q in range(SSM_CHUNK // SUBLANES):
            rows = [us_ref[j, pl.ds(SSM_CHUNK - 1 - (SUBLANES * q + r), n_chunks, stride=CHUNK_PITCH), :]
                    for r in range(SUBLANES)]
            for a, blk in enumerate(_block_transpose8(rows)):
                u_ref[GROUPS_PER_VREG * j + a, :, q * LANES:(q + 1) * LANES] = blk
    gi = _dot(u, win_ref[:, 2 * D_CONV + D_SSM:]) + bg_ref[...]
    gate_ref[...] = jax.nn.sigmoid(gi).astype(_bf16)


def _resident(shape):
    return pl.BlockSpec(shape, lambda *_: (0,) * len(shape), pipeline_mode=pl.Buffered(1))


def _front(x2d, meta_chunk, g1, g2, bg, w1, w3, w2, win, tile):
    rows = x2d.shape[0]
    d_in = win.shape[1]
    n_gate = d_in - 2 * D_CONV - D_SSM
    m_rows = meta_chunk.shape[0]
    row = lambda w: pl.BlockSpec((tile, w), lambda i: (i, 0))
    once = lambda w: pl.BlockSpec((m_rows, w), lambda i: (0, 0))
    hbm = pl.BlockSpec(memory_space=pl.ANY)
    stage_rows = D_MODEL // WEIGHT_LOAD_CHUNKS
    return pl.pallas_call(
        _front_kernel,
        grid=(rows // tile,),
        in_specs=[row(D_MODEL), _resident((m_rows, D_MODEL)), _resident((1, D_MODEL)),
                  _resident((1, D_MODEL)), _resident((1, n_gate)), hbm, hbm, hbm, hbm],
        out_specs=[row(D_MODEL), row(D_CONV),
                   pl.BlockSpec((N_SSM_GROUPS, tile // SSM_CHUNK, CHUNK_LANES), lambda i: (0, i, 0)),
                   row(n_gate), once(D_CONV), once(D_SSM)],
        out_shape=[jax.ShapeDtypeStruct((rows, D_MODEL), _f32),
                   jax.ShapeDtypeStruct((rows, D_CONV), _bf16),
                   jax.ShapeDtypeStruct((N_SSM_GROUPS, rows // SSM_CHUNK, CHUNK_LANES), _f32),
                   jax.ShapeDtypeStruct((rows, n_gate), _bf16),
                   jax.ShapeDtypeStruct((m_rows, D_CONV), _bf16),
                   jax.ShapeDtypeStruct((m_rows, D_SSM), _f32)],
        scratch_shapes=[pltpu.VMEM((D_SSM // LANES, tile // SSM_CHUNK * CHUNK_PITCH, LANES), _f32),
                        pltpu.VMEM((D_MODEL, D_FF), _bf16), pltpu.VMEM((D_MODEL, D_FF), _bf16),
                        pltpu.VMEM((D_FF, D_MODEL), _bf16), pltpu.VMEM((D_MODEL, d_in), _bf16),
                        pltpu.VMEM((2, stage_rows, D_FF), _f32),
                        pltpu.VMEM((2, D_FF // WEIGHT_LOAD_CHUNKS, D_MODEL), _f32),
                        pltpu.VMEM((2, stage_rows, d_in), _f32),
                        pltpu.SemaphoreType.DMA((4, 2))],
        compiler_params=pltpu.CompilerParams(dimension_semantics=("arbitrary",),
                                             vmem_limit_bytes=VMEM_LIMIT),
        name="front",
    )(x2d, meta_chunk, g1, g2, bg, w1, w3, w2, win)


def _ssm_kernel(*refs, chunks_per_seq, levels, groups_per_step, n_cast):
    n_in = len(refs) - 2 * n_cast - 2
    ins, w_f32 = refs[:n_in], refs[n_in:n_in + n_cast]
    y_ref, w_bf16, toep_ref = refs[n_in + n_cast], refs[n_in + n_cast + 1:-1], refs[-1]
    for src, dst in zip(w_f32, w_bf16):
        dst[...] = src[...].astype(_bf16)
    for gi in range(groups_per_step):
        _ssm_group(*[r.at[gi] for r in (*ins, y_ref, toep_ref)],
                   chunks_per_seq=chunks_per_seq, levels=levels)


def _ssm_group(u_ref, um_ref, lam_ref, ldt_ref, bt_ref, c_ref, d_ref, y_ref, toep_ref,
               *, chunks_per_seq, levels):
    T, H, P = SSM_CHUNK, SSM_GROUP, SSM_STATE
    lane = jax.lax.broadcasted_iota(jnp.int32, (1, STATE_LANES), 1)
    lo = lane < P
    sgn = jnp.where(lo, -1.0, 1.0)
    lam_re = lam_ref[0:1, :]
    lam_im = lam_ref[1:2, :]
    dt = jnp.exp(ldt_ref[...])
    ar, ai = lam_re * dt, lam_im * dt

    def powers(k):
        mag, ang = jnp.exp(ar * k), ai * k
        cs, sn = jnp.cos(ang), jnp.sin(ang)
        return mag * jnp.where(lo, cs, sn), mag * jnp.where(lo, sn, cs)

    def cmul3(m1, m2, pw, pws):
        return (m1[None] * pw[:, None, :] + m2[None] * pws[:, None, :]).reshape(T * H, STATE_LANES)

    def cmul_rows(v, vs, m1, m2):
        return v * m1 + vs * m2, vs * m1 - v * m2

    n_dbl = (T // SUBLANES).bit_length() - 1
    assert SUBLANES << n_dbl == T and n_dbl < SUBLANES
    row8 = jax.lax.broadcasted_iota(jnp.int32, (SUBLANES, 1), 0)
    aux, aux_s = powers(jnp.where(row8 < n_dbl, jnp.left_shift(SUBLANES, row8),
                                  jnp.where(row8 == n_dbl, 1, 0)).astype(_f32))
    x1 = jnp.where(lo, aux, aux_s)
    x2 = sgn * jnp.where(lo, aux_s, aux)
    pw, pws = powers(row8.astype(_f32))
    for i in range(n_dbl):
        nxt, nxt_s = cmul_rows(pw, pws, x1[i:i + 1], x2[i:i + 1])
        pw, pws = jnp.concatenate([pw, nxt], axis=0), jnp.concatenate([pws, nxt_s], axis=0)
    pw1, pw1s = cmul_rows(pw, pws, x1[n_dbl:n_dbl + 1], x2[n_dbl:n_dbl + 1])

    lb_re, lb_im = x1[n_dbl:n_dbl + 1], sgn * x2[n_dbl:n_dbl + 1]
    den = lam_re * lam_re + lam_im * lam_im
    co_re = ((lb_re - 1.0) * lam_re + lb_im * lam_im) / den
    co_im = (lb_im * lam_re - (lb_re - 1.0) * lam_im) / den
    b_re, b_im = bt_ref[0], bt_ref[1]
    bb_re = co_re * b_re - co_im * b_im
    bb_im = co_re * b_im + co_im * b_re
    c1, c2 = c_ref[0], sgn * c_ref[1]

    ws = cmul3(bb_re, sgn * bb_im, pw, pws).astype(_bf16)
    wct = (cmul3(c1, c2, pw1, pw1s) * (-sgn)).astype(_bf16)
    e = cmul3(c1, c2, pw, pws)
    r0 = _dot_nt(jnp.where(lo, bb_re, -bb_im), e, precision=jax.lax.Precision.HIGHEST)
    lane_c = jax.lax.broadcasted_iota(jnp.int32, (H, CHUNK_LANES), 1)
    row_c = jax.lax.broadcasted_iota(jnp.int32, (H, CHUNK_LANES), 0)
    r0 = r0 + jnp.where(lane_c == row_c, d_ref[...], 0.0)
    for r in range(SUBLANES):
        rr = pltpu.roll(r0, r * H, axis=1) if r else r0
        for a in range(T // SUBLANES):
            s = SUBLANES * a + r
            blk = pltpu.roll(rr, a * LANES, axis=1) if a else rr
            toep_ref[(T - 1 - s) * H:(T - s) * H, :] = jnp.where(lane_c >= s * H, blk, 0.0).astype(_bf16)

    u = u_ref[...].astype(_bf16)
    rows = u.shape[0]
    s_loc = _dot(u, ws)
    sm = _dot(um_ref[...].astype(_bf16), ws)[0:1, :]

    lvl_col = jax.lax.broadcasted_iota(jnp.int32, (SUBLANES, 1), 0)
    step, step_s = powers(jnp.left_shift(T, lvl_col).astype(_f32))
    a1 = jnp.where(lo, step, step_s)
    a2 = sgn * jnp.where(lo, step_s, step)

    def cmul(v, lvl):
        return a1[lvl:lvl + 1, :] * v + a2[lvl:lvl + 1, :] * pltpu.roll(v, P, axis=1)

    cidx = jax.lax.broadcasted_iota(jnp.int32, (rows, STATE_LANES), 0) % chunks_per_seq
    first = cidx == 0
    st = s_loc + jnp.where(first, cmul(jnp.broadcast_to(sm, s_loc.shape), 0), 0.0)
    for lvl in range(levels):
        shift = 1 << lvl
        prev = jnp.where(cidx >= shift, pltpu.roll(st, shift, axis=0), 0.0)
        st = st + cmul(prev, lvl)
    st_in = jnp.where(first, sm, pltpu.roll(st, 1, axis=0))
    y = _dot(u, toep_ref[...]) + _dot_nt(st_in.astype(_bf16), wct)
    y_ref[...] = jax.nn.gelu(y)


def _ssm(u_t, um_t, lam2, ldt, bt2, c2, d_col, chunks_per_seq, weights):
    groups, rows, _ = u_t.shape
    levels = max(1, (chunks_per_seq - 1).bit_length())
    assert levels <= SUBLANES
    gps = SSM_GROUPS_PER_STEP
    steps = groups // gps
    per_group = lambda *s: pl.BlockSpec((gps,) + s, lambda g: (g,) + (0,) * len(s))
    row_block = lambda w: pl.BlockSpec((w.shape[0] // steps, w.shape[1]), lambda g: (g, 0))
    assert all(w.shape[0] % (steps * 2 * SUBLANES) == 0 for w in weights)
    outs = pl.pallas_call(
        functools.partial(_ssm_kernel, chunks_per_seq=chunks_per_seq, levels=levels,
                          groups_per_step=gps, n_cast=len(weights)),
        grid=(steps,),
        in_specs=[per_group(rows, CHUNK_LANES), per_group(um_t.shape[1], CHUNK_LANES),
                  per_group(2, STATE_LANES), per_group(1, 1),
                  per_group(2, SSM_GROUP, STATE_LANES), per_group(2, SSM_GROUP, STATE_LANES),
                  per_group(SSM_GROUP, 1)] + [row_block(w) for w in weights],
        out_specs=[per_group(rows, CHUNK_LANES)] + [row_block(w) for w in weights],
        out_shape=[jax.ShapeDtypeStruct((groups, rows, CHUNK_LANES), _f32)]
                  + [jax.ShapeDtypeStruct(w.shape, _bf16) for w in weights],
        scratch_shapes=[pltpu.VMEM((gps, CHUNK_LANES, CHUNK_LANES), _bf16)],
        compiler_params=pltpu.CompilerParams(dimension_semantics=("arbitrary",),
                                             vmem_limit_bytes=VMEM_LIMIT),
        name="ssm",
    )(u_t, um_t, lam2, ldt, bt2, c2, d_col, *weights)
    return outs[0], outs[1:]


def _ordered_after(x, dep):
    return jnp.where(dep != dep, dep, x)


def _conv_stage(z_ref, first_of_seq, zmeta_ref, dw_ref, dwb_ref, lng_ref, lnb_ref,
                zpad_ref, shift_ref, zs_ref):
    tile = z_ref.shape[0]
    padded = CONV_HALO + tile
    if first_of_seq is True:
        zpad_ref[0:CONV_HALO, :] = zmeta_ref[...]
    else:
        zpad_ref[0:CONV_HALO, :] = jnp.where(first_of_seq, zmeta_ref[...], zpad_ref[0:CONV_HALO, :])
    zpad_ref[CONV_HALO:, :] = z_ref[...].astype(_f32)
    base = CONV_HALO - (CONV_WIDTH - 1)
    convs = []
    dep = None
    for j in range(D_CONV // LANES):
        cols = slice(j * LANES, (j + 1) * LANES)
        zp = zpad_ref[:, cols]
        for r in range(1, SUBLANES):
            shift_ref[r - 1] = pltpu.roll(zp, padded - r, axis=0)
        pieces = []
        for p in range(tile // CONV_PIECE_ROWS):
            acc = None
            for k in range(CONV_WIDTH):
                a, r = divmod(base + k, SUBLANES)
                lo_row = a * SUBLANES + p * CONV_PIECE_ROWS
                rows = slice(lo_row, lo_row + CONV_PIECE_ROWS)
                src = zpad_ref[rows, cols] if r == 0 else shift_ref[r - 1, rows, :]
                if acc is None and dep is not None:
                    src = _ordered_after(src, dep)
                term = dw_ref[k:k + 1, cols] * src
                acc = term if acc is None else acc + term
            dep = acc
            pieces.append(acc + dwb_ref[:, cols])
        convs.append(jnp.concatenate(pieces, axis=0))
    conv = jnp.concatenate(convs, axis=1)
    zpad_ref[0:CONV_HALO, :] = zpad_ref[tile:tile + CONV_HALO, :]
    mu = jnp.mean(conv, axis=-1, keepdims=True)
    cen = conv - mu
    var = jnp.mean(cen * cen, axis=-1, keepdims=True)
    zn = cen * jax.lax.rsqrt(var + EPS) * lng_ref[...] + lnb_ref[...]
    zs_ref[...] = (zn * jax.nn.sigmoid(zn)).astype(_bf16)


def _token_major_stage(yt_ref, ys_ref):
    n_chunks = yt_ref.shape[1]
    for j in range(D_SSM // LANES):
        for q in range(SSM_CHUNK // SUBLANES):
            blks = [yt_ref[GROUPS_PER_VREG * j + a, :, q * LANES:(q + 1) * LANES]
                    for a in range(GROUPS_PER_VREG)]
            for r, rows in enumerate(_block_transpose8(blks)):
                ys_ref[j, pl.ds(SUBLANES * q + r, n_chunks, stride=CHUNK_PITCH), :] = rows


def _back_kernel(h1_ref, z0_ref, zn_ref, yt0_ref, ytn_ref, gate_ref, zmeta_ref, dw_ref, dwb_ref,
                 lng_ref, lnb_ref, wcp_ref, wv_ref, wg_ref, wout_ref, g3_ref, w1_ref, w3_ref, w2_ref,
                 gf_ref, out_ref, zpad_ref, shift_ref, zs_ref, ys_ref, *, tiles_per_seq):
    n = pl.program_id(0)
    conv_args = (zmeta_ref, dw_ref, dwb_ref, lng_ref, lnb_ref, zpad_ref, shift_ref, zs_ref)

    @pl.when(n == 0)
    def _():
        _conv_stage(z0_ref, True, *conv_args)
        _token_major_stage(yt0_ref, ys_ref)

    zs = zs_ref[...]
    ys = jnp.concatenate(
        [jnp.concatenate([ys_ref[j, c * CHUNK_PITCH:c * CHUNK_PITCH + SSM_CHUNK, :]
                          for c in range(h1_ref.shape[0] // SSM_CHUNK)], axis=0)
         for j in range(D_SSM // LANES)], axis=1).astype(_bf16)
    y_conv = _dot(zs, wcp_ref[...])
    y_ssm = _dot(ys, wv_ref[...]) * jax.nn.sigmoid(_dot(ys, wg_ref[...]))

    _conv_stage(zn_ref, (n + 1) % tiles_per_seq == 0, *conv_args)
    _token_major_stage(ytn_ref, ys_ref)

    mix = (gate_ref[:, 0:D_MODEL].astype(_f32) * y_conv
           + gate_ref[:, D_MODEL:].astype(_f32) * y_ssm).astype(_bf16)
    h2 = h1_ref[...] + _dot(mix, wout_ref[...])
    nrm = _rms_norm(h2, g3_ref[...]).astype(_bf16)
    h3 = h2 + 0.5 * _swiglu(nrm, w1_ref, w3_ref, w2_ref)
    out_ref[...] = _rms_norm(h3, gf_ref[...])


def _back(h1, z, y_t, gate, zmeta, dw, dwb, lng, lnb, wcp, wv, wg, wout, g3, w1, w3, w2, gf,
          batch, seq, tile):
    tiles = batch * seq // tile
    nxt = lambda n: jnp.minimum(n + 1, tiles - 1)
    row = lambda w: pl.BlockSpec((tile, w), lambda n: (n, 0))
    yt_block = (N_SSM_GROUPS, tile // SSM_CHUNK, CHUNK_LANES)
    return pl.pallas_call(
        functools.partial(_back_kernel, tiles_per_seq=seq // tile),
        grid=(tiles,),
        in_specs=[row(D_MODEL),
                  pl.BlockSpec((tile, D_CONV), lambda n: (0, 0)),
                  pl.BlockSpec((tile, D_CONV), lambda n: (nxt(n), 0)),
                  pl.BlockSpec(yt_block, lambda n: (0, 0, 0)),
                  pl.BlockSpec(yt_block, lambda n: (0, nxt(n), 0)),
                  row(2 * D_MODEL),
                  _resident((CONV_HALO, D_CONV)), _resident((CONV_WIDTH, D_CONV)),
                  _resident((1, D_CONV)), _resident((1, D_CONV)), _resident((1, D_CONV)),
                  _resident((D_CONV, D_MODEL)), _resident((D_SSM, D_MODEL)),
                  _resident((D_SSM, D_MODEL)), _resident((D_MODEL, D_MODEL)),
                  _resident((1, D_MODEL)), _resident((D_MODEL, D_FF)), _resident((D_MODEL, D_FF)),
                  _resident((D_FF, D_MODEL)), _resident((1, D_MODEL))],
        out_specs=row(D_MODEL),
        out_shape=jax.ShapeDtypeStruct((batch * seq, D_MODEL), _f32),
        scratch_shapes=[pltpu.VMEM((CONV_HALO + tile, D_CONV), _f32),
                        pltpu.VMEM((SUBLANES - 1, CONV_HALO + tile, LANES), _f32),
                        pltpu.VMEM((tile, D_CONV), _bf16),
                        pltpu.VMEM((D_SSM // LANES, tile // SSM_CHUNK * CHUNK_PITCH, LANES), _f32)],
        compiler_params=pltpu.CompilerParams(dimension_semantics=("arbitrary",),
                                             vmem_limit_bytes=VMEM_LIMIT),
        name="back",
    )(h1, z, z, y_t, y_t, gate, zmeta, dw, dwb, lng, lnb, wcp, wv, wg, wout, g3, w1, w3, w2, gf)


def kernel(x, meta_tokens, ffn1_norm, ffn1_w1, ffn1_w3, ffn1_w2, mix_norm, w_in, b_gate, conv_dw, conv_dw_b, conv_ln_g, conv_ln_b, conv_proj, ssm_lam_re, ssm_lam_im, ssm_log_dt, ssm_b_re, ssm_b_im, ssm_c_re, ssm_c_im, ssm_d, ssm_w_v, ssm_w_g, w_out, ffn2_norm, ffn2_w1, ffn2_w3, ffn2_w2, final_norm):
    batch, seq, _ = x.shape
    assert ffn1_norm.shape[0] == 1 and seq % ROW_TILE == 0 and ROW_TILE % (SUBLANES * SSM_CHUNK) == 0
    assert N_META <= SSM_CHUNK and N_META <= CONV_HALO
    T, H, G = SSM_CHUNK, SSM_GROUP, N_SSM_GROUPS
    row = lambda v: v.reshape(1, -1)
    dup = lambda a: jnp.concatenate([a, a], axis=-1)

    meta_chunk = jnp.zeros((T, D_MODEL), _f32).at[T - N_META:].set(meta_tokens)
    h1, z, u_t, gate, z_m, u_m = _front(
        x.reshape(batch * seq, D_MODEL), meta_chunk, row(ffn1_norm[0]), row(mix_norm[0]),
        row(b_gate[0]), ffn1_w1[0], ffn1_w3[0], ffn1_w2[0], w_in[0], ROW_TILE)
    um_t = u_m[::-1].reshape(T, G, H).transpose(1, 0, 2).reshape(G, 1, T * H)
    um_t = jnp.concatenate([um_t, jnp.zeros((G, 2 * SUBLANES - 1, T * H), _f32)], axis=1)

    lam2 = jnp.stack([dup(ssm_lam_re[0]), dup(ssm_lam_im[0])], axis=1)
    bt2 = jnp.stack([dup(ssm_b_re[0].transpose(0, 2, 1)), dup(ssm_b_im[0].transpose(0, 2, 1))], axis=1)
    c2 = jnp.stack([dup(ssm_c_re[0]), dup(ssm_c_im[0])], axis=1)
    back_w = (conv_proj[0], ssm_w_v[0], ssm_w_g[0], w_out[0], ffn2_w1[0], ffn2_w3[0], ffn2_w2[0])
    y_t, (wcp, wv, wg, wo, w1b, w3b, w2b) = _ssm(
        u_t, um_t, lam2, ssm_log_dt[0].reshape(G, 1, 1), bt2, c2, ssm_d[0].reshape(G, H, 1),
        seq // T, back_w)

    z_halo = z_m[T - CONV_HALO:].astype(_f32)
    out = _back(h1, z, y_t, gate, z_halo, conv_dw[0], row(conv_dw_b[0]), row(conv_ln_g[0]),
                row(conv_ln_b[0]), wcp, wv, wg, wo, row(ffn2_norm[0]), w1b, w3b, w2b,
                row(final_norm), batch, seq, ROW_TILE)
    return out.reshape(batch, seq, D_MODEL)
```

```python
import functools

import jax
import jax.numpy as jnp
from jax.experimental import pallas as pl
from jax.experimental.pallas import tpu as pltpu

D_MODEL = 1024
N_META = 16
D_FF = 2816
D_CONV = 512
CONV_WIDTH = 31
D_SSM = 512
SSM_GROUP = 16
N_SSM_GROUPS = D_SSM // SSM_GROUP
SSM_STATE = 64
EPS = 1e-6

LANES = 128
SUBLANES = 8
SSM_CHUNK = 64
CHUNK_LANES = SSM_CHUNK * SSM_GROUP
STATE_LANES = 2 * SSM_STATE
GROUPS_PER_VREG = LANES // SSM_GROUP
CONV_HALO = 32
CONV_PIECE_ROWS = 16
WEIGHT_LOAD_CHUNKS = 16
SSM_GROUPS_PER_STEP = 2
CHUNK_PITCH = SSM_CHUNK + SUBLANES
FF_SPLITS = ((0, 1024), (1024, 2048), (2048, 2816))
ROW_TILE = 512
VMEM_LIMIT = 60 * 1024 * 1024

assert STATE_LANES == LANES and GROUPS_PER_VREG == SUBLANES

_bf16 = jnp.bfloat16
_f32 = jnp.float32


def _dot(a, b):
    return jnp.dot(a, b, preferred_element_type=_f32)


def _dot_nt(a, b, precision=None):
    return jax.lax.dot_general(a, b, (((1,), (1,)), ((), ())), precision=precision,
                               preferred_element_type=_f32)


def _rms_norm(x, g):
    return x * jax.lax.rsqrt(jnp.mean(x * x, axis=-1, keepdims=True) + EPS) * g


def _swiglu(n, w1_ref, w3_ref, w2_ref):
    acc = None
    for lo, hi in FF_SPLITS:
        a = _dot(n, w1_ref[:, lo:hi])
        b = _dot(n, w3_ref[:, lo:hi])
        f = (a * jax.nn.sigmoid(a) * b).astype(_bf16)
        part = _dot(f, w2_ref[lo:hi, :])
        acc = part if acc is None else acc + part
    return acc


def _block_transpose8(v):
    blk = jax.lax.broadcasted_iota(jnp.int32, v[0].shape, 1) // SSM_GROUP
    for d in (4, 2, 1):
        hi = (blk & d) != 0
        shift = d * SSM_GROUP
        new = list(v)
        for i in range(GROUPS_PER_VREG):
            if i & d:
                continue
            new[i] = jnp.where(hi, pltpu.roll(v[i + d], shift, axis=1), v[i])
            new[i + d] = jnp.where(hi, v[i + d], pltpu.roll(v[i], LANES - shift, axis=1))
        v = new
    return v


def _load_narrowed(jobs, sem):
    def copy(i, c):
        w_hbm, _, stage_ref = jobs[i]
        chunk = stage_ref.shape[1]
        return pltpu.make_async_copy(w_hbm.at[pl.ds(c * chunk, chunk), :], stage_ref.at[c % 2],
                                     sem.at[i, c % 2])

    for w_hbm, _, stage_ref in jobs:
        assert w_hbm.shape[0] == WEIGHT_LOAD_CHUNKS * stage_ref.shape[1]
    for i in range(len(jobs)):
        copy(i, 0).start()
    for c in range(WEIGHT_LOAD_CHUNKS):
        for i, (_, w_vmem, stage_ref) in enumerate(jobs):
            if c + 1 < WEIGHT_LOAD_CHUNKS:
                copy(i, c + 1).start()
            copy(i, c).wait()
            chunk = stage_ref.shape[1]
            w_vmem[c * chunk:(c + 1) * chunk, :] = stage_ref[c % 2].astype(_bf16)


def _front_rows(x, g1_ref, g2_ref, w1_ref, w3_ref, w2_ref, win_ref):
    n = _rms_norm(x, g1_ref[...]).astype(_bf16)
    h1 = x + 0.5 * _swiglu(n, w1_ref, w3_ref, w2_ref)
    u = _rms_norm(h1, g2_ref[...]).astype(_bf16)
    v = _dot(u, win_ref[:, 0:D_CONV])
    g = _dot(u, win_ref[:, D_CONV:2 * D_CONV])
    z = (v * jax.nn.sigmoid(g)).astype(_bf16)
    us = _dot(u, win_ref[:, 2 * D_CONV:2 * D_CONV + D_SSM])
    return h1, z, us, u


def _front_kernel(x_ref, meta_ref, g1_ref, g2_ref, bg_ref, w1_hbm, w3_hbm, w2_hbm, win_hbm,
                  h1_ref, z_ref, u_ref, gate_ref, zm_ref, um_ref,
                  us_ref, w1_ref, w3_ref, w2_ref, win_ref, stage1_ref, stage3_ref, stage2_ref,
                  stagein_ref, sem):
    weights = (g1_ref, g2_ref, w1_ref, w3_ref, w2_ref, win_ref)

    @pl.when(pl.program_id(0) == 0)
    def _():
        _load_narrowed([(w1_hbm, w1_ref, stage1_ref), (w3_hbm, w3_ref, stage3_ref),
                        (w2_hbm, w2_ref, stage2_ref), (win_hbm, win_ref, stagein_ref)], sem)
        _, zm, usm, _ = _front_rows(meta_ref[...], *weights)
        zm_ref[...] = zm
        um_ref[...] = usm

    h1, z, us, u = _front_rows(x_ref[...], *weights)
    h1_ref[...] = h1
    z_ref[...] = z
    n_chunks = us.shape[0] // SSM_CHUNK
    for j in range(D_SSM // LANES):
        for c in range(n_chunks):
            us_ref[j, c * CHUNK_PITCH:c * CHUNK_PITCH + SSM_CHUNK, :] = (
                us[c * SSM_CHUNK:(c + 1) * SSM_CHUNK, j * LANES:(j + 1) * LANES])
        for q in range(SSM_CHUNK // SUBLANES):
            rows = [us_ref[j, pl.ds(SSM_CHUNK - 1 - (SUBLANES * q + r), n_chunks, stride=CHUNK_PITCH), :]
                    for r in range(SUBLANES)]
            for a, blk in enumerate(_block_transpose8(rows)):
                u_ref[GROUPS_PER_VREG * j + a, :, q * LANES:(q + 1) * LANES] = blk
    gi = _dot(u, win_ref[:, 2 * D_CONV + D_SSM:]) + bg_ref[...]
    gate_ref[...] = jax.nn.sigmoid(gi).astype(_bf16)


def _resident(shape):
    return pl.BlockSpec(shape, lambda *_: (0,) * len(shape), pipeline_mode=pl.Buffered(1))


def _front(x2d, meta_chunk, g1, g2, bg, w1, w3, w2, win, tile):
    rows = x2d.shape[0]
    d_in = win.shape[1]
    n_gate = d_in - 2 * D_CONV - D_SSM
    m_rows = meta_chunk.shape[0]
    row = lambda w: pl.BlockSpec((tile, w), lambda i: (i, 0))
    once = lambda w: pl.BlockSpec((m_rows, w), lambda i: (0, 0))
    hbm = pl.BlockSpec(memory_space=pl.ANY)
    stage_rows = D_MODEL // WEIGHT_LOAD_CHUNKS
    return pl.pallas_call(
        _front_kernel,
        grid=(rows // tile,),
        in_specs=[row(D_MODEL), _resident((m_rows, D_MODEL)), _resident((1, D_MODEL)),
                  _resident((1, D_MODEL)), _resident((1, n_gate)), hbm, hbm, hbm, hbm],
        out_specs=[row(D_MODEL), row(D_CONV),
                   pl.BlockSpec((N_SSM_GROUPS, tile // SSM_CHUNK, CHUNK_LANES), lambda i: (0, i, 0)),
                   row(n_gate), once(D_CONV), once(D_SSM)],
        out_shape=[jax.ShapeDtypeStruct((rows, D_MODEL), _f32),
                   jax.ShapeDtypeStruct((rows, D_CONV), _bf16),
                   jax.ShapeDtypeStruct((N_SSM_GROUPS, rows // SSM_CHUNK, CHUNK_LANES), _f32),
                   jax.ShapeDtypeStruct((rows, n_gate), _bf16),
                   jax.ShapeDtypeStruct((m_rows, D_CONV), _bf16),
                   jax.ShapeDtypeStruct((m_rows, D_SSM), _f32)],
        scratch_shapes=[pltpu.VMEM((D_SSM // LANES, tile // SSM_CHUNK * CHUNK_PITCH, LANES), _f32),
                        pltpu.VMEM((D_MODEL, D_FF), _bf16), pltpu.VMEM((D_MODEL, D_FF), _bf16),
                        pltpu.VMEM((D_FF, D_MODEL), _bf16), pltpu.VMEM((D_MODEL, d_in), _bf16),
                        pltpu.VMEM((2, stage_rows, D_FF), _f32), pltpu.VMEM((2, stage_rows, D_FF), _f32),
                        pltpu.VMEM((2, D_FF // WEIGHT_LOAD_CHUNKS, D_MODEL), _f32),
                        pltpu.VMEM((2, stage_rows, d_in), _f32),
                        pltpu.SemaphoreType.DMA((4, 2))],
        compiler_params=pltpu.CompilerParams(dimension_semantics=("arbitrary",),
                                             vmem_limit_bytes=VMEM_LIMIT),
        name="front",
    )(x2d, meta_chunk, g1, g2, bg, w1, w3, w2, win)


def _ssm_kernel(*refs, chunks_per_seq, levels, groups_per_step, n_cast):
    n_in = len(refs) - 2 * n_cast - 2
    ins, w_f32 = refs[:n_in], refs[n_in:n_in + n_cast]
    y_ref, w_bf16, toep_ref = refs[n_in + n_cast], refs[n_in + n_cast + 1:-1], refs[-1]
    for src, dst in zip(w_f32, w_bf16):
        dst[...] = src[...].astype(_bf16)
    for gi in range(groups_per_step):
        _ssm_group(*[r.at[gi] for r in (*ins, y_ref, toep_ref)],
                   chunks_per_seq=chunks_per_seq, levels=levels)


def _ssm_group(u_ref, um_ref, lam_ref, ldt_ref, bt_ref, c_ref, d_ref, y_ref, toep_ref,
               *, chunks_per_seq, levels):
    T, H, P = SSM_CHUNK, SSM_GROUP, SSM_STATE
    lane = jax.lax.broadcasted_iota(jnp.int32, (1, STATE_LANES), 1)
    lo = lane < P
    sgn = jnp.where(lo, -1.0, 1.0)
    lam_re = lam_ref[0:1, :]
    lam_im = lam_ref[1:2, :]
    dt = jnp.exp(ldt_ref[...])
    ar, ai = lam_re * dt, lam_im * dt

    def powers(k):
        mag, ang = jnp.exp(ar * k), ai * k
        cs, sn = jnp.cos(ang), jnp.sin(ang)
        return mag * jnp.where(lo, cs, sn), mag * jnp.where(lo, sn, cs)

    def cmul3(m1, m2, pw, pws):
        return (m1[None] * pw[:, None, :] + m2[None] * pws[:, None, :]).reshape(T * H, STATE_LANES)

    def cmul_rows(v, vs, m1, m2):
        return v * m1 + vs * m2, vs * m1 - v * m2

    n_dbl = (T // SUBLANES).bit_length() - 1
    assert SUBLANES << n_dbl == T and n_dbl < SUBLANES
    row8 = jax.lax.broadcasted_iota(jnp.int32, (SUBLANES, 1), 0)
    aux, aux_s = powers(jnp.where(row8 < n_dbl, jnp.left_shift(SUBLANES, row8),
                                  jnp.where(row8 == n_dbl, 1, 0)).astype(_f32))
    x1 = jnp.where(lo, aux, aux_s)
    x2 = sgn * jnp.where(lo, aux_s, aux)
    pw, pws = powers(row8.astype(_f32))
    for i in range(n_dbl):
        nxt, nxt_s = cmul_rows(pw, pws, x1[i:i + 1], x2[i:i + 1])
        pw, pws = jnp.concatenate([pw, nxt], axis=0), jnp.concatenate([pws, nxt_s], axis=0)
    pw1, pw1s = cmul_rows(pw, pws, x1[n_dbl:n_dbl + 1], x2[n_dbl:n_dbl + 1])

    lb_re, lb_im = x1[n_dbl:n_dbl + 1], sgn * x2[n_dbl:n_dbl + 1]
    den = lam_re * lam_re + lam_im * lam_im
    co_re = ((lb_re - 1.0) * lam_re + lb_im * lam_im) / den
    co_im = (lb_im * lam_re - (lb_re - 1.0) * lam_im) / den
    b_re, b_im = bt_ref[0], bt_ref[1]
    bb_re = co_re * b_re - co_im * b_im
    bb_im = co_re * b_im + co_im * b_re
    c1, c2 = c_ref[0], sgn * c_ref[1]

    ws = cmul3(bb_re, sgn * bb_im, pw, pws).astype(_bf16)
    wct = (cmul3(c1, c2, pw1, pw1s) * (-sgn)).astype(_bf16)
    e = cmul3(c1, c2, pw, pws)
    r0 = _dot_nt(jnp.where(lo, bb_re, -bb_im), e, precision=jax.lax.Precision.HIGHEST)
    lane_c = jax.lax.broadcasted_iota(jnp.int32, (H, CHUNK_LANES), 1)
    row_c = jax.lax.broadcasted_iota(jnp.int32, (H, CHUNK_LANES), 0)
    r0 = r0 + jnp.where(lane_c == row_c, d_ref[...], 0.0)
    for r in range(SUBLANES):
        rr = pltpu.roll(r0, r * H, axis=1) if r else r0
        for a in range(T // SUBLANES):
            s = SUBLANES * a + r
            blk = pltpu.roll(rr, a * LANES, axis=1) if a else rr
            toep_ref[(T - 1 - s) * H:(T - s) * H, :] = jnp.where(lane_c >= s * H, blk, 0.0).astype(_bf16)

    u = u_ref[...].astype(_bf16)
    rows = u.shape[0]
    s_loc = _dot(u, ws)
    sm = _dot(um_ref[...].astype(_bf16), ws)[0:1, :]

    lvl_col = jax.lax.broadcasted_iota(jnp.int32, (SUBLANES, 1), 0)
    step, step_s = powers(jnp.left_shift(T, lvl_col).astype(_f32))
    a1 = jnp.where(lo, step, step_s)
    a2 = sgn * jnp.where(lo, step_s, step)

    def cmul(v, lvl):
        return a1[lvl:lvl + 1, :] * v + a2[lvl:lvl + 1, :] * pltpu.roll(v, P, axis=1)

    cidx = jax.lax.broadcasted_iota(jnp.int32, (rows, STATE_LANES), 0) % chunks_per_seq
    first = cidx == 0
    st = s_loc + jnp.where(first, cmul(jnp.broadcast_to(sm, s_loc.shape), 0), 0.0)
    for lvl in range(levels):
        shift = 1 << lvl
        prev = jnp.where(cidx >= shift, pltpu.roll(st, shift, axis=0), 0.0)
        st = st + cmul(prev, lvl)
    st_in = jnp.where(first, sm, pltpu.roll(st, 1, axis=0))
    y = _dot(u, toep_ref[...]) + _dot_nt(st_in.astype(_bf16), wct)
    y_ref[...] = jax.nn.gelu(y)


def _ssm(u_t, um_t, lam2, ldt, bt2, c2, d_col, chunks_per_seq, weights):
    groups, rows, _ = u_t.shape
    levels = max(1, (chunks_per_seq - 1).bit_length())
    assert levels <= SUBLANES
    gps = SSM_GROUPS_PER_STEP
    steps = groups // gps
    per_group = lambda *s: pl.BlockSpec((gps,) + s, lambda g: (g,) + (0,) * len(s))
    row_block = lambda w: pl.BlockSpec((w.shape[0] // steps, w.shape[1]), lambda g: (g, 0))
    assert all(w.shape[0] % (steps * 2 * SUBLANES) == 0 for w in weights)
    outs = pl.pallas_call(
        functools.partial(_ssm_kernel, chunks_per_seq=chunks_per_seq, levels=levels,
                          groups_per_step=gps, n_cast=len(weights)),
        grid=(steps,),
        in_specs=[per_group(rows, CHUNK_LANES), per_group(um_t.shape[1], CHUNK_LANES),
                  per_group(2, STATE_LANES), per_group(1, 1),
                  per_group(2, SSM_GROUP, STATE_LANES), per_group(2, SSM_GROUP, STATE_LANES),
                  per_group(SSM_GROUP, 1)] + [row_block(w) for w in weights],
        out_specs=[per_group(rows, CHUNK_LANES)] + [row_block(w) for w in weights],
        out_shape=[jax.ShapeDtypeStruct((groups, rows, CHUNK_LANES), _f32)]
                  + [jax.ShapeDtypeStruct(w.shape, _bf16) for w in weights],
        scratch_shapes=[pltpu.VMEM((gps, CHUNK_LANES, CHUNK_LANES), _bf16)],
        compiler_params=pltpu.CompilerParams(dimension_semantics=("arbitrary",),
                                             vmem_limit_bytes=VMEM_LIMIT),
        name="ssm",
    )(u_t, um_t, lam2, ldt, bt2, c2, d_col, *weights)
    return outs[0], outs[1:]


def _ordered_after(x, dep):
    return jnp.where(dep != dep, dep, x)


def _conv_stage(z_ref, first_of_seq, zmeta_ref, dw_ref, dwb_ref, lng_ref, lnb_ref,
                zpad_ref, shift_ref, zs_ref):
    tile = z_ref.shape[0]
    padded = CONV_HALO + tile
    if first_of_seq is True:
        zpad_ref[0:CONV_HALO, :] = zmeta_ref[...]
    else:
        zpad_ref[0:CONV_HALO, :] = jnp.where(first_of_seq, zmeta_ref[...], zpad_ref[0:CONV_HALO, :])
    zpad_ref[CONV_HALO:, :] = z_ref[...].astype(_f32)
    base = CONV_HALO - (CONV_WIDTH - 1)
    convs = []
    dep = None
    for j in range(D_CONV // LANES):
        cols = slice(j * LANES, (j + 1) * LANES)
        zp = zpad_ref[:, cols]
        for r in range(1, SUBLANES):
            shift_ref[r - 1] = pltpu.roll(zp, padded - r, axis=0)
        pieces = []
        for p in range(tile // CONV_PIECE_ROWS):
            acc = None
            for k in range(CONV_WIDTH):
                a, r = divmod(base + k, SUBLANES)
                lo_row = a * SUBLANES + p * CONV_PIECE_ROWS
                rows = slice(lo_row, lo_row + CONV_PIECE_ROWS)
                src = zpad_ref[rows, cols] if r == 0 else shift_ref[r - 1, rows, :]
                if acc is None and dep is not None:
                    src = _ordered_after(src, dep)
                term = dw_ref[k:k + 1, cols] * src
                acc = term if acc is None else acc + term
            dep = acc
            pieces.append(acc + dwb_ref[:, cols])
        convs.append(jnp.concatenate(pieces, axis=0))
    conv = jnp.concatenate(convs, axis=1)
    zpad_ref[0:CONV_HALO, :] = zpad_ref[tile:tile + CONV_HALO, :]
    mu = jnp.mean(conv, axis=-1, keepdims=True)
    cen = conv - mu
    var = jnp.mean(cen * cen, axis=-1, keepdims=True)
    zn = cen * jax.lax.rsqrt(var + EPS) * lng_ref[...] + lnb_ref[...]
    zs_ref[...] = (zn * jax.nn.sigmoid(zn)).astype(_bf16)


def _token_major_stage(yt_ref, ys_ref):
    n_chunks = yt_ref.shape[1]
    for j in range(D_SSM // LANES):
        for q in range(SSM_CHUNK // SUBLANES):
            blks = [yt_ref[GROUPS_PER_VREG * j + a, :, q * LANES:(q + 1) * LANES]
                    for a in range(GROUPS_PER_VREG)]
            for r, rows in enumerate(_block_transpose8(blks)):
                ys_ref[j, pl.ds(SUBLANES * q + r, n_chunks, stride=CHUNK_PITCH), :] = rows


def _back_kernel(h1_ref, z0_ref, zn_ref, yt0_ref, ytn_ref, gate_ref, zmeta_ref, dw_ref, dwb_ref,
                 lng_ref, lnb_ref, wcp_ref, wv_ref, wg_ref, wout_ref, g3_ref, w1_ref, w3_ref, w2_ref,
                 gf_ref, out_ref, zpad_ref, shift_ref, zs_ref, ys_ref, *, tiles_per_seq):
    n = pl.program_id(0)
    conv_args = (zmeta_ref, dw_ref, dwb_ref, lng_ref, lnb_ref, zpad_ref, shift_ref, zs_ref)

    @pl.when(n == 0)
    def _():
        _conv_stage(z0_ref, True, *conv_args)
        _token_major_stage(yt0_ref, ys_ref)

    zs = zs_ref[...]
    ys = jnp.concatenate(
        [jnp.concatenate([ys_ref[j, c * CHUNK_PITCH:c * CHUNK_PITCH + SSM_CHUNK, :]
                          for c in range(h1_ref.shape[0] // SSM_CHUNK)], axis=0)
         for j in range(D_SSM // LANES)], axis=1).astype(_bf16)
    y_conv = _dot(zs, wcp_ref[...])
    y_ssm = _dot(ys, wv_ref[...]) * jax.nn.sigmoid(_dot(ys, wg_ref[...]))

    _conv_stage(zn_ref, (n + 1) % tiles_per_seq == 0, *conv_args)
    _token_major_stage(ytn_ref, ys_ref)

    mix = (gate_ref[:, 0:D_MODEL].astype(_f32) * y_conv
           + gate_ref[:, D_MODEL:].astype(_f32) * y_ssm).astype(_bf16)
    h2 = h1_ref[...] + _dot(mix, wout_ref[...])
    nrm = _rms_norm(h2, g3_ref[...]).astype(_bf16)
    h3 = h2 + 0.5 * _swiglu(nrm, w1_ref, w3_ref, w2_ref)
    out_ref[...] = _rms_norm(h3, gf_ref[...])


def _back(h1, z, y_t, gate, zmeta, dw, dwb, lng, lnb, wcp, wv, wg, wout, g3, w1, w3, w2, gf,
          batch, seq, tile):
    tiles = batch * seq // tile
    nxt = lambda n: jnp.minimum(n + 1, tiles - 1)
    row = lambda w: pl.BlockSpec((tile, w), lambda n: (n, 0))
    yt_block = (N_SSM_GROUPS, tile // SSM_CHUNK, CHUNK_LANES)
    return pl.pallas_call(
        functools.partial(_back_kernel, tiles_per_seq=seq // tile),
        grid=(tiles,),
        in_specs=[row(D_MODEL),
                  pl.BlockSpec((tile, D_CONV), lambda n: (0, 0)),
                  pl.BlockSpec((tile, D_CONV), lambda n: (nxt(n), 0)),
                  pl.BlockSpec(yt_block, lambda n: (0, 0, 0)),
                  pl.BlockSpec(yt_block, lambda n: (0, nxt(n), 0)),
                  row(2 * D_MODEL),
                  _resident((CONV_HALO, D_CONV)), _resident((CONV_WIDTH, D_CONV)),
                  _resident((1, D_CONV)), _resident((1, D_CONV)), _resident((1, D_CONV)),
                  _resident((D_CONV, D_MODEL)), _resident((D_SSM, D_MODEL)),
                  _resident((D_SSM, D_MODEL)), _resident((D_MODEL, D_MODEL)),
                  _resident((1, D_MODEL)), _resident((D_MODEL, D_FF)), _resident((D_MODEL, D_FF)),
                  _resident((D_FF, D_MODEL)), _resident((1, D_MODEL))],
        out_specs=row(D_MODEL),
        out_shape=jax.ShapeDtypeStruct((batch * seq, D_MODEL), _f32),
        scratch_shapes=[pltpu.VMEM((CONV_HALO + tile, D_CONV), _f32),
                        pltpu.VMEM((SUBLANES - 1, CONV_HALO + tile, LANES), _f32),
                        pltpu.VMEM((tile, D_CONV), _bf16),
                        pltpu.VMEM((D_SSM // LANES, tile // SSM_CHUNK * CHUNK_PITCH, LANES), _f32)],
        compiler_params=pltpu.CompilerParams(dimension_semantics=("arbitrary",),
                                             vmem_limit_bytes=VMEM_LIMIT),
        name="back",
    )(h1, z, z, y_t, y_t, gate, zmeta, dw, dwb, lng, lnb, wcp, wv, wg, wout, g3, w1, w3, w2, gf)


def kernel(x, meta_tokens, ffn1_norm, ffn1_w1, ffn1_w3, ffn1_w2, mix_norm, w_in, b_gate, conv_dw, conv_dw_b, conv_ln_g, conv_ln_b, conv_proj, ssm_lam_re, ssm_lam_im, ssm_log_dt, ssm_b_re, ssm_b_im, ssm_c_re, ssm_c_im, ssm_d, ssm_w_v, ssm_w_g, w_out, ffn2_norm, ffn2_w1, ffn2_w3, ffn2_w2, final_norm):
    batch, seq, _ = x.shape
    assert ffn1_norm.shape[0] == 1 and seq % ROW_TILE == 0 and ROW_TILE % (SUBLANES * SSM_CHUNK) == 0
    assert N_META <= SSM_CHUNK and N_META <= CONV_HALO
    T, H, G = SSM_CHUNK, SSM_GROUP, N_SSM_GROUPS
    row = lambda v: v.reshape(1, -1)
    dup = lambda a: jnp.concatenate([a, a], axis=-1)

    meta_chunk = jnp.zeros((T, D_MODEL), _f32).at[T - N_META:].set(meta_tokens)
    h1, z, u_t, gate, z_m, u_m = _front(
        x.reshape(batch * seq, D_MODEL), meta_chunk, row(ffn1_norm[0]), row(mix_norm[0]),
        row(b_gate[0]), ffn1_w1[0], ffn1_w3[0], ffn1_w2[0], w_in[0], ROW_TILE)
    um_t = u_m[::-1].reshape(T, G, H).transpose(1, 0, 2).reshape(G, 1, T * H)
    um_t = jnp.concatenate([um_t, jnp.zeros((G, 2 * SUBLANES - 1, T * H), _f32)], axis=1)

    lam2 = jnp.stack([dup(ssm_lam_re[0]), dup(ssm_lam_im[0])], axis=1)
    bt2 = jnp.stack([dup(ssm_b_re[0].transpose(0, 2, 1)), dup(ssm_b_im[0].transpose(0, 2, 1))], axis=1)
    c2 = jnp.stack([dup(ssm_c_re[0]), dup(ssm_c_im[0])], axis=1)
    back_w = (conv_proj[0], ssm_w_v[0], ssm_w_g[0], w_out[0], ffn2_w1[0], ffn2_w3[0], ffn2_w2[0])
    y_t, (wcp, wv, wg, wo, w1b, w3b, w2b) = _ssm(
        u_t, um_t, lam2, ssm_log_dt[0].reshape(G, 1, 1), bt2, c2, ssm_d[0].reshape(G, H, 1),
        seq // T, back_w)

    z_halo = z_m[T - CONV_HALO:].astype(_f32)
    out = _back(h1, z, y_t, gate, z_halo, conv_dw[0], row(conv_dw_b[0]), row(conv_ln_g[0]),
                row(conv_ln_b[0]), wcp, wv, wg, wo, row(ffn2_norm[0]), w1b, w3b, w2b,
                row(final_norm), batch, seq, ROW_TILE)
    return out.reshape(batch, seq, D_MODEL)
```

```python
import functools

import jax
import jax.numpy as jnp
from jax.experimental import pallas as pl
from jax.experimental.pallas import tpu as pltpu

D_MODEL = 1024
N_META = 16
D_FF = 2816
D_CONV = 512
CONV_WIDTH = 31
D_SSM = 512
SSM_GROUP = 16
N_SSM_GROUPS = D_SSM // SSM_GROUP
SSM_STATE = 64
EPS = 1e-6

LANES = 128
SUBLANES = 8
SSM_CHUNK = 32
CHUNK_LANES = SSM_CHUNK * SSM_GROUP
STATE_LANES = 2 * SSM_STATE
GROUPS_PER_VREG = LANES // SSM_GROUP
CONV_HALO = 32
CONV_PIECE_ROWS = 16
WEIGHT_LOAD_CHUNKS = 16
SSM_GROUPS_PER_STEP = 2
CHUNK_PITCH = SSM_CHUNK + SUBLANES
FF_SPLITS = ((0, 1024), (1024, 2048), (2048, 2816))
ROW_TILE = 512
VMEM_LIMIT = 60 * 1024 * 1024

assert STATE_LANES == LANES and GROUPS_PER_VREG == SUBLANES

_bf16 = jnp.bfloat16
_f32 = jnp.float32


def _dot(a, b):
    return jnp.dot(a, b, preferred_element_type=_f32)


def _dot_nt(a, b, precision=None):
    return jax.lax.dot_general(a, b, (((1,), (1,)), ((), ())), precision=precision,
                               preferred_element_type=_f32)


def _rms_norm(x, g):
    return x * jax.lax.rsqrt(jnp.mean(x * x, axis=-1, keepdims=True) + EPS) * g


def _swiglu(n, w1_ref, w3_ref, w2_ref):
    acc = None
    for lo, hi in FF_SPLITS:
        a = _dot(n, w1_ref[:, lo:hi])
        b = _dot(n, w3_ref[:, lo:hi])
        f = (a * jax.nn.sigmoid(a) * b).astype(_bf16)
        part = _dot(f, w2_ref[lo:hi, :])
        acc = part if acc is None else acc + part
    return acc


def _block_transpose8(v):
    blk = jax.lax.broadcasted_iota(jnp.int32, v[0].shape, 1) // SSM_GROUP
    for d in (4, 2, 1):
        hi = (blk & d) != 0
        shift = d * SSM_GROUP
        new = list(v)
        for i in range(GROUPS_PER_VREG):
            if i & d:
                continue
            new[i] = jnp.where(hi, pltpu.roll(v[i + d], shift, axis=1), v[i])
            new[i + d] = jnp.where(hi, v[i + d], pltpu.roll(v[i], LANES - shift, axis=1))
        v = new
    return v


def _load_narrowed(jobs, sem):
    def copy(i, c):
        w_hbm, _, stage_ref = jobs[i]
        chunk = stage_ref.shape[1]
        return pltpu.make_async_copy(w_hbm.at[pl.ds(c * chunk, chunk), :], stage_ref.at[c % 2],
                                     sem.at[i, c % 2])

    for w_hbm, _, stage_ref in jobs:
        assert w_hbm.shape[0] == WEIGHT_LOAD_CHUNKS * stage_ref.shape[1]
    for i in range(len(jobs)):
        copy(i, 0).start()
    for c in range(WEIGHT_LOAD_CHUNKS):
        for i, (_, w_vmem, stage_ref) in enumerate(jobs):
            if c + 1 < WEIGHT_LOAD_CHUNKS:
                copy(i, c + 1).start()
            copy(i, c).wait()
            chunk = stage_ref.shape[1]
            w_vmem[c * chunk:(c + 1) * chunk, :] = stage_ref[c % 2].astype(_bf16)


def _front_rows(x, g1_ref, g2_ref, w1_ref, w3_ref, w2_ref, win_ref):
    n = _rms_norm(x, g1_ref[...]).astype(_bf16)
    h1 = x + 0.5 * _swiglu(n, w1_ref, w3_ref, w2_ref)
    u = _rms_norm(h1, g2_ref[...]).astype(_bf16)
    v = _dot(u, win_ref[:, 0:D_CONV])
    g = _dot(u, win_ref[:, D_CONV:2 * D_CONV])
    z = (v * jax.nn.sigmoid(g)).astype(_bf16)
    us = _dot(u, win_ref[:, 2 * D_CONV:2 * D_CONV + D_SSM])
    return h1, z, us, u


def _front_kernel(x_ref, meta_ref, g1_ref, g2_ref, bg_ref, w1_hbm, w3_hbm, w2_hbm, win_hbm,
                  h1_ref, z_ref, u_ref, gate_ref, zm_ref, um_ref,
                  us_ref, w1_ref, w3_ref, w2_ref, win_ref, stage1_ref, stage3_ref, stage2_ref,
                  stagein_ref, sem):
    weights = (g1_ref, g2_ref, w1_ref, w3_ref, w2_ref, win_ref)

    @pl.when(pl.program_id(0) == 0)
    def _():
        _load_narrowed([(w1_hbm, w1_ref, stage1_ref), (w3_hbm, w3_ref, stage3_ref),
                        (w2_hbm, w2_ref, stage2_ref), (win_hbm, win_ref, stagein_ref)], sem)
        _, zm, usm, _ = _front_rows(meta_ref[...], *weights)
        zm_ref[...] = zm
        um_ref[...] = usm

    h1, z, us, u = _front_rows(x_ref[...], *weights)
    h1_ref[...] = h1
    z_ref[...] = z
    n_chunks = us.shape[0] // SSM_CHUNK
    for j in range(D_SSM // LANES):
        for c in range(n_chunks):
            us_ref[j, c * CHUNK_PITCH:c * CHUNK_PITCH + SSM_CHUNK, :] = (
                us[c * SSM_CHUNK:(c + 1) * SSM_CHUNK, j * LANES:(j + 1) * LANES])
        for q in range(SSM_CHUNK // SUBLANES):
            rows = [us_ref[j, pl.ds(SSM_CHUNK - 1 - (SUBLANES * q + r), n_chunks, stride=CHUNK_PITCH), :]
                    for r in range(SUBLANES)]
            for a, blk in enumerate(_block_transpose8(rows)):
                u_ref[GROUPS_PER_VREG * j + a, :, q * LANES:(q + 1) * LANES] = blk
    gi = _dot(u, win_ref[:, 2 * D_CONV + D_SSM:]) + bg_ref[...]
    gate_ref[...] = jax.nn.sigmoid(gi).astype(_bf16)


def _resident(shape):
    return pl.BlockSpec(shape, lambda *_: (0,) * len(shape), pipeline_mode=pl.Buffered(1))


def _front(x2d, meta_chunk, g1, g2, bg, w1, w3, w2, win, tile):
    rows = x2d.shape[0]
    d_in = win.shape[1]
    n_gate = d_in - 2 * D_CONV - D_SSM
    m_rows = meta_chunk.shape[0]
    row = lambda w: pl.BlockSpec((tile, w), lambda i: (i, 0))
    once = lambda w: pl.BlockSpec((m_rows, w), lambda i: (0, 0))
    hbm = pl.BlockSpec(memory_space=pl.ANY)
    stage_rows = D_MODEL // WEIGHT_LOAD_CHUNKS
    return pl.pallas_call(
        _front_kernel,
        grid=(rows // tile,),
        in_specs=[row(D_MODEL), _resident((m_rows, D_MODEL)), _resident((1, D_MODEL)),
                  _resident((1, D_MODEL)), _resident((1, n_gate)), hbm, hbm, hbm, hbm],
        out_specs=[row(D_MODEL), row(D_CONV),
                   pl.BlockSpec((N_SSM_GROUPS, tile // SSM_CHUNK, CHUNK_LANES), lambda i: (0, i, 0)),
                   row(n_gate), once(D_CONV), once(D_SSM)],
        out_shape=[jax.ShapeDtypeStruct((rows, D_MODEL), _f32),
                   jax.ShapeDtypeStruct((rows, D_CONV), _bf16),
                   jax.ShapeDtypeStruct((N_SSM_GROUPS, rows // SSM_CHUNK, CHUNK_LANES), _f32),
                   jax.ShapeDtypeStruct((rows, n_gate), _bf16),
                   jax.ShapeDtypeStruct((m_rows, D_CONV), _bf16),
                   jax.ShapeDtypeStruct((m_rows, D_SSM), _f32)],
        scratch_shapes=[pltpu.VMEM((D_SSM // LANES, tile // SSM_CHUNK * CHUNK_PITCH, LANES), _f32),
                        pltpu.VMEM((D_MODEL, D_FF), _bf16), pltpu.VMEM((D_MODEL, D_FF), _bf16),
                        pltpu.VMEM((D_FF, D_MODEL), _bf16), pltpu.VMEM((D_MODEL, d_in), _bf16),
                        pltpu.VMEM((2, stage_rows, D_FF), _f32), pltpu.VMEM((2, stage_rows, D_FF), _f32),
                        pltpu.VMEM((2, D_FF // WEIGHT_LOAD_CHUNKS, D_MODEL), _f32),
                        pltpu.VMEM((2, stage_rows, d_in), _f32),
                        pltpu.SemaphoreType.DMA((4, 2))],
        compiler_params=pltpu.CompilerParams(dimension_semantics=("arbitrary",),
                                             vmem_limit_bytes=VMEM_LIMIT),
        name="front",
    )(x2d, meta_chunk, g1, g2, bg, w1, w3, w2, win)


def _ssm_kernel(*refs, chunks_per_seq, levels, groups_per_step, n_cast):
    n_in = len(refs) - 2 * n_cast - 2
    ins, w_f32 = refs[:n_in], refs[n_in:n_in + n_cast]
    y_ref, w_bf16, toep_ref = refs[n_in + n_cast], refs[n_in + n_cast + 1:-1], refs[-1]
    for src, dst in zip(w_f32, w_bf16):
        dst[...] = src[...].astype(_bf16)
    for gi in range(groups_per_step):
        _ssm_group(*[r.at[gi] for r in (*ins, y_ref, toep_ref)],
                   chunks_per_seq=chunks_per_seq, levels=levels)


def _ssm_group(u_ref, um_ref, lam_ref, ldt_ref, bt_ref, c_ref, d_ref, y_ref, toep_ref,
               *, chunks_per_seq, levels):
    T, H, P = SSM_CHUNK, SSM_GROUP, SSM_STATE
    lane = jax.lax.broadcasted_iota(jnp.int32, (1, STATE_LANES), 1)
    lo = lane < P
    sgn = jnp.where(lo, -1.0, 1.0)
    lam_re = lam_ref[0:1, :]
    lam_im = lam_ref[1:2, :]
    dt = jnp.exp(ldt_ref[...])
    ar, ai = lam_re * dt, lam_im * dt

    def powers(k):
        mag, ang = jnp.exp(ar * k), ai * k
        cs, sn = jnp.cos(ang), jnp.sin(ang)
        return mag * jnp.where(lo, cs, sn), mag * jnp.where(lo, sn, cs)

    def cmul3(m1, m2, pw, pws):
        return (m1[None] * pw[:, None, :] + m2[None] * pws[:, None, :]).reshape(T * H, STATE_LANES)

    def cmul_rows(v, vs, m1, m2):
        return v * m1 + vs * m2, vs * m1 - v * m2

    n_dbl = (T // SUBLANES).bit_length() - 1
    assert SUBLANES << n_dbl == T and n_dbl < SUBLANES
    row8 = jax.lax.broadcasted_iota(jnp.int32, (SUBLANES, 1), 0)
    aux, aux_s = powers(jnp.where(row8 < n_dbl, jnp.left_shift(SUBLANES, row8),
                                  jnp.where(row8 == n_dbl, 1, 0)).astype(_f32))
    x1 = jnp.where(lo, aux, aux_s)
    x2 = sgn * jnp.where(lo, aux_s, aux)
    pw, pws = powers(row8.astype(_f32))
    for i in range(n_dbl):
        nxt, nxt_s = cmul_rows(pw, pws, x1[i:i + 1], x2[i:i + 1])
        pw, pws = jnp.concatenate([pw, nxt], axis=0), jnp.concatenate([pws, nxt_s], axis=0)
    pw1, pw1s = cmul_rows(pw, pws, x1[n_dbl:n_dbl + 1], x2[n_dbl:n_dbl + 1])

    lb_re, lb_im = x1[n_dbl:n_dbl + 1], sgn * x2[n_dbl:n_dbl + 1]
    den = lam_re * lam_re + lam_im * lam_im
    co_re = ((lb_re - 1.0) * lam_re + lb_im * lam_im) / den
    co_im = (lb_im * lam_re - (lb_re - 1.0) * lam_im) / den
    b_re, b_im = bt_ref[0], bt_ref[1]
    bb_re = co_re * b_re - co_im * b_im
    bb_im = co_re * b_im + co_im * b_re
    c1, c2 = c_ref[0], sgn * c_ref[1]

    ws = cmul3(bb_re, sgn * bb_im, pw, pws).astype(_bf16)
    wct = (cmul3(c1, c2, pw1, pw1s) * (-sgn)).astype(_bf16)
    e = cmul3(c1, c2, pw, pws)
    r0 = _dot_nt(jnp.where(lo, bb_re, -bb_im), e, precision=jax.lax.Precision.HIGHEST)
    lane_c = jax.lax.broadcasted_iota(jnp.int32, (H, CHUNK_LANES), 1)
    row_c = jax.lax.broadcasted_iota(jnp.int32, (H, CHUNK_LANES), 0)
    r0 = r0 + jnp.where(lane_c == row_c, d_ref[...], 0.0)
    for r in range(SUBLANES):
        rr = pltpu.roll(r0, r * H, axis=1) if r else r0
        for a in range(T // SUBLANES):
            s = SUBLANES * a + r
            blk = pltpu.roll(rr, a * LANES, axis=1) if a else rr
            toep_ref[(T - 1 - s) * H:(T - s) * H, :] = jnp.where(lane_c >= s * H, blk, 0.0).astype(_bf16)

    u = u_ref[...].astype(_bf16)
    rows = u.shape[0]
    s_loc = _dot(u, ws)
    sm = _dot(um_ref[...].astype(_bf16), ws)[0:1, :]

    lvl_col = jax.lax.broadcasted_iota(jnp.int32, (SUBLANES, 1), 0)
    step, step_s = powers(jnp.left_shift(T, lvl_col).astype(_f32))
    a1 = jnp.where(lo, step, step_s)
    a2 = sgn * jnp.where(lo, step_s, step)

    def cmul(v, lvl):
        return a1[lvl:lvl + 1, :] * v + a2[lvl:lvl + 1, :] * pltpu.roll(v, P, axis=1)

    cidx = jax.lax.broadcasted_iota(jnp.int32, (rows, STATE_LANES), 0) % chunks_per_seq
    first = cidx == 0
    st = s_loc + jnp.where(first, cmul(jnp.broadcast_to(sm, s_loc.shape), 0), 0.0)
    for lvl in range(levels):
        shift = 1 << lvl
        prev = jnp.where(cidx >= shift, pltpu.roll(st, shift, axis=0), 0.0)
        st = st + cmul(prev, lvl)
    st_in = jnp.where(first, sm, pltpu.roll(st, 1, axis=0))
    y = _dot(u, toep_ref[...]) + _dot_nt(st_in.astype(_bf16), wct)
    y_ref[...] = jax.nn.gelu(y)


def _ssm(u_t, um_t, lam2, ldt, bt2, c2, d_col, chunks_per_seq, weights):
    groups, rows, _ = u_t.shape
    levels = max(1, (chunks_per_seq - 1).bit_length())
    assert levels <= SUBLANES
    gps = SSM_GROUPS_PER_STEP
    steps = groups // gps
    per_group = lambda *s: pl.BlockSpec((gps,) + s, lambda g: (g,) + (0,) * len(s))
    row_block = lambda w: pl.BlockSpec((w.shape[0] // steps, w.shape[1]), lambda g: (g, 0))
    assert all(w.shape[0] % (steps * 2 * SUBLANES) == 0 for w in weights)
    outs = pl.pallas_call(
        functools.partial(_ssm_kernel, chunks_per_seq=chunks_per_seq, levels=levels,
                          groups_per_step=gps, n_cast=len(weights)),
        grid=(steps,),
        in_specs=[per_group(rows, CHUNK_LANES), per_group(um_t.shape[1], CHUNK_LANES),
                  per_group(2, STATE_LANES), per_group(1, 1),
                  per_group(2, SSM_GROUP, STATE_LANES), per_group(2, SSM_GROUP, STATE_LANES),
                  per_group(SSM_GROUP, 1)] + [row_block(w) for w in weights],
        out_specs=[per_group(rows, CHUNK_LANES)] + [row_block(w) for w in weights],
        out_shape=[jax.ShapeDtypeStruct((groups, rows, CHUNK_LANES), _f32)]
                  + [jax.ShapeDtypeStruct(w.shape, _bf16) for w in weights],
        scratch_shapes=[pltpu.VMEM((gps, CHUNK_LANES, CHUNK_LANES), _bf16)],
        compiler_params=pltpu.CompilerParams(dimension_semantics=("arbitrary",),
                                             vmem_limit_bytes=VMEM_LIMIT),
        name="ssm",
    )(u_t, um_t, lam2, ldt, bt2, c2, d_col, *weights)
    return outs[0], outs[1:]


def _ordered_after(x, dep):
    return jnp.where(dep != dep, dep, x)


def _conv_stage(z_ref, first_of_seq, zmeta_ref, dw_ref, dwb_ref, lng_ref, lnb_ref,
                zpad_ref, shift_ref, zs_ref):
    tile = z_ref.shape[0]
    padded = CONV_HALO + tile
    if first_of_seq is True:
        zpad_ref[0:CONV_HALO, :] = zmeta_ref[...]
    else:
        zpad_ref[0:CONV_HALO, :] = jnp.where(first_of_seq, zmeta_ref[...], zpad_ref[0:CONV_HALO, :])
    zpad_ref[CONV_HALO:, :] = z_ref[...].astype(_f32)
    base = CONV_HALO - (CONV_WIDTH - 1)
    convs = []
    dep = None
    for j in range(D_CONV // LANES):
        cols = slice(j * LANES, (j + 1) * LANES)
        zp = zpad_ref[:, cols]
        for r in range(1, SUBLANES):
            shift_ref[r - 1] = pltpu.roll(zp, padded - r, axis=0)
        pieces = []
        for p in range(tile // CONV_PIECE_ROWS):
            acc = None
            for k in range(CONV_WIDTH):
                a, r = divmod(base + k, SUBLANES)
                lo_row = a * SUBLANES + p * CONV_PIECE_ROWS
                rows = slice(lo_row, lo_row + CONV_PIECE_ROWS)
                src = zpad_ref[rows, cols] if r == 0 else shift_ref[r - 1, rows, :]
                if acc is None and dep is not None:
                    src = _ordered_after(src, dep)
                term = dw_ref[k:k + 1, cols] * src
                acc = term if acc is None else acc + term
            dep = acc
            pieces.append(acc + dwb_ref[:, cols])
        convs.append(jnp.concatenate(pieces, axis=0))
    conv = jnp.concatenate(convs, axis=1)
    zpad_ref[0:CONV_HALO, :] = zpad_ref[tile:tile + CONV_HALO, :]
    mu = jnp.mean(conv, axis=-1, keepdims=True)
    cen = conv - mu
    var = jnp.mean(cen * cen, axis=-1, keepdims=True)
    zn = cen * jax.lax.rsqrt(var + EPS) * lng_ref[...] + lnb_ref[...]
    zs_ref[...] = (zn * jax.nn.sigmoid(zn)).astype(_bf16)


def _token_major_stage(yt_ref, ys_ref):
    n_chunks = yt_ref.shape[1]
    for j in range(D_SSM // LANES):
        for q in range(SSM_CHUNK // SUBLANES):
            blks = [yt_ref[GROUPS_PER_VREG * j + a, :, q * LANES:(q + 1) * LANES]
                    for a in range(GROUPS_PER_VREG)]
            for r, rows in enumerate(_block_transpose8(blks)):
                ys_ref[j, pl.ds(SUBLANES * q + r, n_chunks, stride=CHUNK_PITCH), :] = rows


def _back_kernel(h1_ref, z0_ref, zn_ref, yt0_ref, ytn_ref, gate_ref, zmeta_ref, dw_ref, dwb_ref,
                 lng_ref, lnb_ref, wcp_ref, wv_ref, wg_ref, wout_ref, g3_ref, w1_ref, w3_ref, w2_ref,
                 gf_ref, out_ref, zpad_ref, shift_ref, zs_ref, ys_ref, *, tiles_per_seq):
    n = pl.program_id(0)
    conv_args = (zmeta_ref, dw_ref, dwb_ref, lng_ref, lnb_ref, zpad_ref, shift_ref, zs_ref)

    @pl.when(n == 0)
    def _():
        _conv_stage(z0_ref, True, *conv_args)
        _token_major_stage(yt0_ref, ys_ref)

    zs = zs_ref[...]
    ys = jnp.concatenate(
        [jnp.concatenate([ys_ref[j, c * CHUNK_PITCH:c * CHUNK_PITCH + SSM_CHUNK, :]
                          for c in range(h1_ref.shape[0] // SSM_CHUNK)], axis=0)
         for j in range(D_SSM // LANES)], axis=1).astype(_bf16)
    y_conv = _dot(zs, wcp_ref[...])
    y_ssm = _dot(ys, wv_ref[...]) * jax.nn.sigmoid(_dot(ys, wg_ref[...]))

    _conv_stage(zn_ref, (n + 1) % tiles_per_seq == 0, *conv_args)
    _token_major_stage(ytn_ref, ys_ref)

    mix = (gate_ref[:, 0:D_MODEL].astype(_f32) * y_conv
           + gate_ref[:, D_MODEL:].astype(_f32) * y_ssm).astype(_bf16)
    h2 = h1_ref[...] + _dot(mix, wout_ref[...])
    nrm = _rms_norm(h2, g3_ref[...]).astype(_bf16)
    h3 = h2 + 0.5 * _swiglu(nrm, w1_ref, w3_ref, w2_ref)
    out_ref[...] = _rms_norm(h3, gf_ref[...])


def _back(h1, z, y_t, gate, zmeta, dw, dwb, lng, lnb, wcp, wv, wg, wout, g3, w1, w3, w2, gf,
          batch, seq, tile):
    tiles = batch * seq // tile
    nxt = lambda n: jnp.minimum(n + 1, tiles - 1)
    row = lambda w: pl.BlockSpec((tile, w), lambda n: (n, 0))
    yt_block = (N_SSM_GROUPS, tile // SSM_CHUNK, CHUNK_LANES)
    return pl.pallas_call(
        functools.partial(_back_kernel, tiles_per_seq=seq // tile),
        grid=(tiles,),
        in_specs=[row(D_MODEL),
                  pl.BlockSpec((tile, D_CONV), lambda n: (0, 0)),
                  pl.BlockSpec((tile, D_CONV), lambda n: (nxt(n), 0)),
                  pl.BlockSpec(yt_block, lambda n: (0, 0, 0)),
                  pl.BlockSpec(yt_block, lambda n: (0, nxt(n), 0)),
                  row(2 * D_MODEL),
                  _resident((CONV_HALO, D_CONV)), _resident((CONV_WIDTH, D_CONV)),
                  _resident((1, D_CONV)), _resident((1, D_CONV)), _resident((1, D_CONV)),
                  _resident((D_CONV, D_MODEL)), _resident((D_SSM, D_MODEL)),
                  _resident((D_SSM, D_MODEL)), _resident((D_MODEL, D_MODEL)),
                  _resident((1, D_MODEL)), _resident((D_MODEL, D_FF)), _resident((D_MODEL, D_FF)),
                  _resident((D_FF, D_MODEL)), _resident((1, D_MODEL))],
        out_specs=row(D_MODEL),
        out_shape=jax.ShapeDtypeStruct((batch * seq, D_MODEL), _f32),
        scratch_shapes=[pltpu.VMEM((CONV_HALO + tile, D_CONV), _f32),
                        pltpu.VMEM((SUBLANES - 1, CONV_HALO + tile, LANES), _f32),
                        pltpu.VMEM((tile, D_CONV), _bf16),
                        pltpu.VMEM((D_SSM // LANES, tile // SSM_CHUNK * CHUNK_PITCH, LANES), _f32)],
        compiler_params=pltpu.CompilerParams(dimension_semantics=("arbitrary",),
                                             vmem_limit_bytes=VMEM_LIMIT),
        name="back",
    )(h1, z, z, y_t, y_t, gate, zmeta, dw, dwb, lng, lnb, wcp, wv, wg, wout, g3, w1, w3, w2, gf)


def kernel(x, meta_tokens, ffn1_norm, ffn1_w1, ffn1_w3, ffn1_w2, mix_norm, w_in, b_gate, conv_dw, conv_dw_b, conv_ln_g, conv_ln_b, conv_proj, ssm_lam_re, ssm_lam_im, ssm_log_dt, ssm_b_re, ssm_b_im, ssm_c_re, ssm_c_im, ssm_d, ssm_w_v, ssm_w_g, w_out, ffn2_norm, ffn2_w1, ffn2_w3, ffn2_w2, final_norm):
    batch, seq, _ = x.shape
    assert ffn1_norm.shape[0] == 1 and seq % ROW_TILE == 0 and ROW_TILE % (SUBLANES * SSM_CHUNK) == 0
    assert N_META <= SSM_CHUNK and N_META <= CONV_HALO
    T, H, G = SSM_CHUNK, SSM_GROUP, N_SSM_GROUPS
    row = lambda v: v.reshape(1, -1)
    dup = lambda a: jnp.concatenate([a, a], axis=-1)

    meta_chunk = jnp.zeros((T, D_MODEL), _f32).at[T - N_META:].set(meta_tokens)
    h1, z, u_t, gate, z_m, u_m = _front(
        x.reshape(batch * seq, D_MODEL), meta_chunk, row(ffn1_norm[0]), row(mix_norm[0]),
        row(b_gate[0]), ffn1_w1[0], ffn1_w3[0], ffn1_w2[0], w_in[0], ROW_TILE)
    um_t = u_m[::-1].reshape(T, G, H).transpose(1, 0, 2).reshape(G, 1, T * H)
    um_t = jnp.concatenate([um_t, jnp.zeros((G, 2 * SUBLANES - 1, T * H), _f32)], axis=1)

    lam2 = jnp.stack([dup(ssm_lam_re[0]), dup(ssm_lam_im[0])], axis=1)
    bt2 = jnp.stack([dup(ssm_b_re[0].transpose(0, 2, 1)), dup(ssm_b_im[0].transpose(0, 2, 1))], axis=1)
    c2 = jnp.stack([dup(ssm_c_re[0]), dup(ssm_c_im[0])], axis=1)
    back_w = (conv_proj[0], ssm_w_v[0], ssm_w_g[0], w_out[0], ffn2_w1[0], ffn2_w3[0], ffn2_w2[0])
    y_t, (wcp, wv, wg, wo, w1b, w3b, w2b) = _ssm(
        u_t, um_t, lam2, ssm_log_dt[0].reshape(G, 1, 1), bt2, c2, ssm_d[0].reshape(G, H, 1),
        seq // T, back_w)

    z_halo = z_m[T - CONV_HALO:].astype(_f32)
    out = _back(h1, z, y_t, gate, z_halo, conv_dw[0], row(conv_dw_b[0]), row(conv_ln_g[0]),
                row(conv_ln_b[0]), wcp, wv, wg, wo, row(ffn2_norm[0]), w1b, w3b, w2b,
                row(final_norm), batch, seq, ROW_TILE)
    return out.reshape(batch, seq, D_MODEL)
```

```python
import functools

import jax
import jax.numpy as jnp
from jax.experimental import pallas as pl
from jax.experimental.pallas import tpu as pltpu

D_MODEL = 1024
N_META = 16
D_FF = 2816
D_CONV = 512
CONV_WIDTH = 31
D_SSM = 512
SSM_GROUP = 16
N_SSM_GROUPS = D_SSM // SSM_GROUP
SSM_STATE = 64
EPS = 1e-6

LANES = 128
SUBLANES = 8
SSM_CHUNK = 32
CHUNK_LANES = SSM_CHUNK * SSM_GROUP
STATE_LANES = 2 * SSM_STATE
GROUPS_PER_VREG = LANES // SSM_GROUP
CONV_HALO = 32
CONV_PIECE_ROWS = (8, 8, 16, 16)
MIX_SLABS = 2
WEIGHT_LOAD_CHUNKS = 16
SSM_GROUPS_PER_STEP = 2
CHUNK_PITCH = SSM_CHUNK + SUBLANES
FF_SPLITS = ((0, 1024), (1024, 2048), (2048, 2816))
ROW_TILE = 512
VMEM_LIMIT = 60 * 1024 * 1024

assert STATE_LANES == LANES and GROUPS_PER_VREG == SUBLANES

_bf16 = jnp.bfloat16
_f32 = jnp.float32


def _dot(a, b):
    return jnp.dot(a, b, preferred_element_type=_f32)


def _dot_nt(a, b, precision=None):
    return jax.lax.dot_general(a, b, (((1,), (1,)), ((), ())), precision=precision,
                               preferred_element_type=_f32)


def _rms_norm(x, g):
    return x * jax.lax.rsqrt(jnp.mean(x * x, axis=-1, keepdims=True) + EPS) * g


def _swiglu(n, w1_ref, w3_ref, w2_ref):
    acc = None
    for lo, hi in FF_SPLITS:
        a = _dot(n, w1_ref[:, lo:hi])
        b = _dot(n, w3_ref[:, lo:hi])
        f = (a * jax.nn.sigmoid(a) * b).astype(_bf16)
        part = _dot(f, w2_ref[lo:hi, :])
        acc = part if acc is None else acc + part
    return acc


def _block_transpose8(v):
    blk = jax.lax.broadcasted_iota(jnp.int32, v[0].shape, 1) // SSM_GROUP
    for d in (4, 2, 1):
        hi = (blk & d) != 0
        shift = d * SSM_GROUP
        new = list(v)
        for i in range(GROUPS_PER_VREG):
            if i & d:
                continue
            new[i] = jnp.where(hi, pltpu.roll(v[i + d], shift, axis=1), v[i])
            new[i + d] = jnp.where(hi, v[i + d], pltpu.roll(v[i], LANES - shift, axis=1))
        v = new
    return v


def _load_narrowed(jobs, sem):
    def copy(i, c):
        w_hbm, _, stage_ref = jobs[i]
        chunk = stage_ref.shape[1]
        return pltpu.make_async_copy(w_hbm.at[pl.ds(c * chunk, chunk), :], stage_ref.at[c % 2],
                                     sem.at[i, c % 2])

    for w_hbm, _, stage_ref in jobs:
        assert w_hbm.shape[0] == WEIGHT_LOAD_CHUNKS * stage_ref.shape[1]
    for i in range(len(jobs)):
        copy(i, 0).start()
    for c in range(WEIGHT_LOAD_CHUNKS):
        for i, (_, w_vmem, stage_ref) in enumerate(jobs):
            if c + 1 < WEIGHT_LOAD_CHUNKS:
                copy(i, c + 1).start()
            copy(i, c).wait()
            chunk = stage_ref.shape[1]
            w_vmem[c * chunk:(c + 1) * chunk, :] = stage_ref[c % 2].astype(_bf16)


def _front_rows(x, g1_ref, g2_ref, w1_ref, w3_ref, w2_ref, win_ref):
    n = _rms_norm(x, g1_ref[...]).astype(_bf16)
    h1 = x + 0.5 * _swiglu(n, w1_ref, w3_ref, w2_ref)
    u = _rms_norm(h1, g2_ref[...]).astype(_bf16)
    v = _dot(u, win_ref[:, 0:D_CONV])
    g = _dot(u, win_ref[:, D_CONV:2 * D_CONV])
    z = (v * jax.nn.sigmoid(g)).astype(_bf16)
    us = _dot(u, win_ref[:, 2 * D_CONV:2 * D_CONV + D_SSM])
    return h1, z, us, u


def _front_kernel(x_ref, meta_ref, g1_ref, g2_ref, bg_ref, w1_hbm, w3_hbm, w2_hbm, win_hbm,
                  h1_ref, z_ref, u_ref, gate_ref, zm_ref, um_ref,
                  us_ref, w1_ref, w3_ref, w2_ref, win_ref, stage1_ref, stage3_ref, stage2_ref,
                  stagein_ref, sem):
    weights = (g1_ref, g2_ref, w1_ref, w3_ref, w2_ref, win_ref)

    @pl.when(pl.program_id(0) == 0)
    def _():
        _load_narrowed([(w1_hbm, w1_ref, stage1_ref), (w3_hbm, w3_ref, stage3_ref),
                        (w2_hbm, w2_ref, stage2_ref), (win_hbm, win_ref, stagein_ref)], sem)
        _, zm, usm, _ = _front_rows(meta_ref[...], *weights)
        zm_ref[...] = zm
        um_ref[...] = usm

    h1, z, us, u = _front_rows(x_ref[...], *weights)
    h1_ref[...] = h1
    z_ref[...] = z
    n_chunks = us.shape[0] // SSM_CHUNK
    for j in range(D_SSM // LANES):
        for c in range(n_chunks):
            us_ref[j, c * CHUNK_PITCH:c * CHUNK_PITCH + SSM_CHUNK, :] = (
                us[c * SSM_CHUNK:(c + 1) * SSM_CHUNK, j * LANES:(j + 1) * LANES])
        for q in range(SSM_CHUNK // SUBLANES):
            rows = [us_ref[j, pl.ds(SSM_CHUNK - 1 - (SUBLANES * q + r), n_chunks, stride=CHUNK_PITCH), :]
                    for r in range(SUBLANES)]
            for a, blk in enumerate(_block_transpose8(rows)):
                u_ref[GROUPS_PER_VREG * j + a, :, q * LANES:(q + 1) * LANES] = blk
    gi = _dot(u, win_ref[:, 2 * D_CONV + D_SSM:]) + bg_ref[...]
    gate_ref[...] = jax.nn.sigmoid(gi).astype(_bf16)


def _resident(shape):
    return pl.BlockSpec(shape, lambda *_: (0,) * len(shape), pipeline_mode=pl.Buffered(1))


def _front(x2d, meta_chunk, g1, g2, bg, w1, w3, w2, win, tile):
    rows = x2d.shape[0]
    d_in = win.shape[1]
    n_gate = d_in - 2 * D_CONV - D_SSM
    m_rows = meta_chunk.shape[0]
    row = lambda w: pl.BlockSpec((tile, w), lambda i: (i, 0))
    once = lambda w: pl.BlockSpec((m_rows, w), lambda i: (0, 0))
    hbm = pl.BlockSpec(memory_space=pl.ANY)
    stage_rows = D_MODEL // WEIGHT_LOAD_CHUNKS
    return pl.pallas_call(
        _front_kernel,
        grid=(rows // tile,),
        in_specs=[row(D_MODEL), _resident((m_rows, D_MODEL)), _resident((1, D_MODEL)),
                  _resident((1, D_MODEL)), _resident((1, n_gate)), hbm, hbm, hbm, hbm],
        out_specs=[row(D_MODEL), row(D_CONV),
                   pl.BlockSpec((N_SSM_GROUPS, tile // SSM_CHUNK, CHUNK_LANES), lambda i: (0, i, 0)),
                   row(n_gate), once(D_CONV), once(D_SSM)],
        out_shape=[jax.ShapeDtypeStruct((rows, D_MODEL), _f32),
                   jax.ShapeDtypeStruct((rows, D_CONV), _bf16),
                   jax.ShapeDtypeStruct((N_SSM_GROUPS, rows // SSM_CHUNK, CHUNK_LANES), _f32),
                   jax.ShapeDtypeStruct((rows, n_gate), _bf16),
                   jax.ShapeDtypeStruct((m_rows, D_CONV), _bf16),
                   jax.ShapeDtypeStruct((m_rows, D_SSM), _f32)],
        scratch_shapes=[pltpu.VMEM((D_SSM // LANES, tile // SSM_CHUNK * CHUNK_PITCH, LANES), _f32),
                        pltpu.VMEM((D_MODEL, D_FF), _bf16), pltpu.VMEM((D_MODEL, D_FF), _bf16),
                        pltpu.VMEM((D_FF, D_MODEL), _bf16), pltpu.VMEM((D_MODEL, d_in), _bf16),
                        pltpu.VMEM((2, stage_rows, D_FF), _f32), pltpu.VMEM((2, stage_rows, D_FF), _f32),
                        pltpu.VMEM((2, D_FF // WEIGHT_LOAD_CHUNKS, D_MODEL), _f32),
                        pltpu.VMEM((2, stage_rows, d_in), _f32),
                        pltpu.SemaphoreType.DMA((4, 2))],
        compiler_params=pltpu.CompilerParams(dimension_semantics=("arbitrary",),
                                             vmem_limit_bytes=VMEM_LIMIT),
        name="front",
    )(x2d, meta_chunk, g1, g2, bg, w1, w3, w2, win)


def _ssm_kernel(*refs, chunks_per_seq, levels, groups_per_step, n_cast):
    n_in = len(refs) - 2 * n_cast - 2
    ins, w_f32 = refs[:n_in], refs[n_in:n_in + n_cast]
    y_ref, w_bf16, toep_ref = refs[n_in + n_cast], refs[n_in + n_cast + 1:-1], refs[-1]
    for src, dst in zip(w_f32, w_bf16):
        dst[...] = src[...].astype(_bf16)
    for gi in range(groups_per_step):
        _ssm_group(*[r.at[gi] for r in (*ins, y_ref, toep_ref)],
                   chunks_per_seq=chunks_per_seq, levels=levels)


def _ssm_group(u_ref, um_ref, lam_ref, ldt_ref, bt_ref, c_ref, d_ref, y_ref, toep_ref,
               *, chunks_per_seq, levels):
    T, H, P = SSM_CHUNK, SSM_GROUP, SSM_STATE
    lane = jax.lax.broadcasted_iota(jnp.int32, (1, STATE_LANES), 1)
    lo = lane < P
    sgn = jnp.where(lo, -1.0, 1.0)
    lam_re = lam_ref[0:1, :]
    lam_im = lam_ref[1:2, :]
    dt = jnp.exp(ldt_ref[...])
    ar, ai = lam_re * dt, lam_im * dt

    def powers(k):
        mag, ang = jnp.exp(ar * k), ai * k
        cs, sn = jnp.cos(ang), jnp.sin(ang)
        return mag * jnp.where(lo, cs, sn), mag * jnp.where(lo, sn, cs)

    def cmul3(m1, m2, pw, pws):
        return (m1[None] * pw[:, None, :] + m2[None] * pws[:, None, :]).reshape(T * H, STATE_LANES)

    def cmul_rows(v, vs, m1, m2):
        return v * m1 + vs * m2, vs * m1 - v * m2

    n_dbl = (T // SUBLANES).bit_length() - 1
    assert SUBLANES << n_dbl == T and n_dbl < SUBLANES
    row8 = jax.lax.broadcasted_iota(jnp.int32, (SUBLANES, 1), 0)
    aux, aux_s = powers(jnp.where(row8 < n_dbl, jnp.left_shift(SUBLANES, row8),
                                  jnp.where(row8 == n_dbl, 1, 0)).astype(_f32))
    x1 = jnp.where(lo, aux, aux_s)
    x2 = sgn * jnp.where(lo, aux_s, aux)
    pw, pws = powers(row8.astype(_f32))
    for i in range(n_dbl):
        nxt, nxt_s = cmul_rows(pw, pws, x1[i:i + 1], x2[i:i + 1])
        pw, pws = jnp.concatenate([pw, nxt], axis=0), jnp.concatenate([pws, nxt_s], axis=0)
    pw1, pw1s = cmul_rows(pw, pws, x1[n_dbl:n_dbl + 1], x2[n_dbl:n_dbl + 1])

    lb_re, lb_im = x1[n_dbl:n_dbl + 1], sgn * x2[n_dbl:n_dbl + 1]
    den = lam_re * lam_re + lam_im * lam_im
    co_re = ((lb_re - 1.0) * lam_re + lb_im * lam_im) / den
    co_im = (lb_im * lam_re - (lb_re - 1.0) * lam_im) / den
    b_re, b_im = bt_ref[0], bt_ref[1]
    bb_re = co_re * b_re - co_im * b_im
    bb_im = co_re * b_im + co_im * b_re
    c1, c2 = c_ref[0], sgn * c_ref[1]

    ws = cmul3(bb_re, sgn * bb_im, pw, pws).astype(_bf16)
    wct = (cmul3(c1, c2, pw1, pw1s) * (-sgn)).astype(_bf16)
    e = cmul3(c1, c2, pw, pws)
    r0 = _dot_nt(jnp.where(lo, bb_re, -bb_im), e, precision=jax.lax.Precision.HIGHEST)
    lane_c = jax.lax.broadcasted_iota(jnp.int32, (H, CHUNK_LANES), 1)
    row_c = jax.lax.broadcasted_iota(jnp.int32, (H, CHUNK_LANES), 0)
    r0 = r0 + jnp.where(lane_c == row_c, d_ref[...], 0.0)
    for r in range(SUBLANES):
        rr = pltpu.roll(r0, r * H, axis=1) if r else r0
        for a in range(T // SUBLANES):
            s = SUBLANES * a + r
            blk = pltpu.roll(rr, a * LANES, axis=1) if a else rr
            toep_ref[(T - 1 - s) * H:(T - s) * H, :] = jnp.where(lane_c >= s * H, blk, 0.0).astype(_bf16)

    u = u_ref[...].astype(_bf16)
    rows = u.shape[0]
    s_loc = _dot(u, ws)
    sm = _dot(um_ref[...].astype(_bf16), ws)[0:1, :]

    lvl_col = jax.lax.broadcasted_iota(jnp.int32, (SUBLANES, 1), 0)
    step, step_s = powers(jnp.left_shift(T, lvl_col).astype(_f32))
    a1 = jnp.where(lo, step, step_s)
    a2 = sgn * jnp.where(lo, step_s, step)

    def cmul(v, lvl):
        return a1[lvl:lvl + 1, :] * v + a2[lvl:lvl + 1, :] * pltpu.roll(v, P, axis=1)

    cidx = jax.lax.broadcasted_iota(jnp.int32, (rows, STATE_LANES), 0) % chunks_per_seq
    first = cidx == 0
    st = s_loc + jnp.where(first, cmul(jnp.broadcast_to(sm, s_loc.shape), 0), 0.0)
    for lvl in range(levels):
        shift = 1 << lvl
        prev = jnp.where(cidx >= shift, pltpu.roll(st, shift, axis=0), 0.0)
        st = st + cmul(prev, lvl)
    st_in = jnp.where(first, sm, pltpu.roll(st, 1, axis=0))
    y = _dot(u, toep_ref[...]) + _dot_nt(st_in.astype(_bf16), wct)
    y_ref[...] = jax.nn.gelu(y)


def _ssm(u_t, um_t, lam2, ldt, bt2, c2, d_col, chunks_per_seq, weights):
    groups, rows, _ = u_t.shape
    levels = max(1, (chunks_per_seq - 1).bit_length())
    assert levels <= SUBLANES
    gps = SSM_GROUPS_PER_STEP
    steps = groups // gps
    per_group = lambda *s: pl.BlockSpec((gps,) + s, lambda g: (g,) + (0,) * len(s))
    row_block = lambda w: pl.BlockSpec((w.shape[0] // steps, w.shape[1]), lambda g: (g, 0))
    assert all(w.shape[0] % (steps * 2 * SUBLANES) == 0 for w in weights)
    outs = pl.pallas_call(
        functools.partial(_ssm_kernel, chunks_per_seq=chunks_per_seq, levels=levels,
                          groups_per_step=gps, n_cast=len(weights)),
        grid=(steps,),
        in_specs=[per_group(rows, CHUNK_LANES), per_group(um_t.shape[1], CHUNK_LANES),
                  per_group(2, STATE_LANES), per_group(1, 1),
                  per_group(2, SSM_GROUP, STATE_LANES), per_group(2, SSM_GROUP, STATE_LANES),
                  per_group(SSM_GROUP, 1)] + [row_block(w) for w in weights],
        out_specs=[per_group(rows, CHUNK_LANES)] + [row_block(w) for w in weights],
        out_shape=[jax.ShapeDtypeStruct((groups, rows, CHUNK_LANES), _f32)]
                  + [jax.ShapeDtypeStruct(w.shape, _bf16) for w in weights],
        scratch_shapes=[pltpu.VMEM((gps, CHUNK_LANES, CHUNK_LANES), _bf16)],
        compiler_params=pltpu.CompilerParams(dimension_semantics=("arbitrary",),
                                             vmem_limit_bytes=VMEM_LIMIT),
        name="ssm",
    )(u_t, um_t, lam2, ldt, bt2, c2, d_col, *weights)
    return outs[0], outs[1:]


def _chain_link(acc):
    link = acc[0:SUBLANES]
    for i in range(SUBLANES, acc.shape[0], SUBLANES):
        nxt = acc[i:i + SUBLANES]
        link = jnp.where(nxt != nxt, nxt, link)
    return link


def _ordered_after(x, link):
    nan = link != link
    return jnp.concatenate([jnp.where(nan, link, x[i:i + SUBLANES])
                            for i in range(0, x.shape[0], SUBLANES)], axis=0)


def _conv_stage(z_ref, first_of_seq, zmeta_ref, dw_ref, dwb_ref, lng_ref, lnb_ref,
                zpad_ref, shift_ref, zs_ref):
    tile = z_ref.shape[0]
    padded = CONV_HALO + tile
    if first_of_seq is True:
        zpad_ref[0:CONV_HALO, :] = zmeta_ref[...]
    else:
        zpad_ref[0:CONV_HALO, :] = jnp.where(first_of_seq, zmeta_ref[...], zpad_ref[0:CONV_HALO, :])
    zpad_ref[CONV_HALO:, :] = z_ref[...].astype(_f32)
    base = CONV_HALO - (CONV_WIDTH - 1)
    convs = []
    dep = None
    for j in range(D_CONV // LANES):
        cols = slice(j * LANES, (j + 1) * LANES)
        zp = zpad_ref[:, cols]
        for r in range(1, SUBLANES):
            shift_ref[r - 1] = pltpu.roll(zp, padded - r, axis=0)
        pieces = []
        piece_rows = CONV_PIECE_ROWS[j]
        for p in range(tile // piece_rows):
            acc = None
            for k in range(CONV_WIDTH):
                a, r = divmod(base + k, SUBLANES)
                lo_row = a * SUBLANES + p * piece_rows
                rows = slice(lo_row, lo_row + piece_rows)
                src = zpad_ref[rows, cols] if r == 0 else shift_ref[r - 1, rows, :]
                if acc is None and dep is not None:
                    src = _ordered_after(src, dep)
                term = dw_ref[k:k + 1, cols] * src
                acc = term if acc is None else acc + term
            dep = _chain_link(acc)
            pieces.append(acc + dwb_ref[:, cols])
        convs.append(jnp.concatenate(pieces, axis=0))
    conv = jnp.concatenate(convs, axis=1)
    zpad_ref[0:CONV_HALO, :] = zpad_ref[tile:tile + CONV_HALO, :]
    mu = jnp.mean(conv, axis=-1, keepdims=True)
    cen = conv - mu
    var = jnp.mean(cen * cen, axis=-1, keepdims=True)
    zn = cen * jax.lax.rsqrt(var + EPS) * lng_ref[...] + lnb_ref[...]
    zs_ref[...] = (zn * jax.nn.sigmoid(zn)).astype(_bf16)


def _token_major_stage(yt_ref, ys_ref):
    n_chunks = yt_ref.shape[1]
    for j in range(D_SSM // LANES):
        for q in range(SSM_CHUNK // SUBLANES):
            blks = [yt_ref[GROUPS_PER_VREG * j + a, :, q * LANES:(q + 1) * LANES]
                    for a in range(GROUPS_PER_VREG)]
            for r, rows in enumerate(_block_transpose8(blks)):
                ys_ref[j, pl.ds(SUBLANES * q + r, n_chunks, stride=CHUNK_PITCH), :] = rows


def _back_kernel(h1_ref, z0_ref, zn_ref, yt0_ref, ytn_ref, gate_ref, zmeta_ref, dw_ref, dwb_ref,
                 lng_ref, lnb_ref, wcp_ref, wv_ref, wg_ref, wout_ref, g3_ref, w1_ref, w3_ref, w2_ref,
                 gf_ref, out_ref, zpad_ref, shift_ref, zs_ref, ys_ref, *, tiles_per_seq):
    n = pl.program_id(0)
    conv_args = (zmeta_ref, dw_ref, dwb_ref, lng_ref, lnb_ref, zpad_ref, shift_ref, zs_ref)

    @pl.when(n == 0)
    def _():
        _conv_stage(z0_ref, True, *conv_args)
        _token_major_stage(yt0_ref, ys_ref)

    slab = h1_ref.shape[0] // MIX_SLABS
    h2_parts, nrm_parts = [], []
    for s in range(MIX_SLABS):
        rows = slice(s * slab, (s + 1) * slab)
        zs = zs_ref[rows, :]
        ys = jnp.concatenate(
            [jnp.concatenate([ys_ref[j, c * CHUNK_PITCH:c * CHUNK_PITCH + SSM_CHUNK, :]
                              for c in range(s * slab // SSM_CHUNK, (s + 1) * slab // SSM_CHUNK)], axis=0)
             for j in range(D_SSM // LANES)], axis=1).astype(_bf16)
        y_conv = _dot(zs, wcp_ref[...])
        y_ssm = _dot(ys, wv_ref[...]) * jax.nn.sigmoid(_dot(ys, wg_ref[...]))
        mix = (gate_ref[rows, 0:D_MODEL].astype(_f32) * y_conv
               + gate_ref[rows, D_MODEL:].astype(_f32) * y_ssm).astype(_bf16)
        h2_parts.append(h1_ref[rows, :] + _dot(mix, wout_ref[...]))
        nrm_parts.append(_rms_norm(h2_parts[-1], g3_ref[...]).astype(_bf16))
    h2 = jnp.concatenate(h2_parts, axis=0)
    nrm = jnp.concatenate(nrm_parts, axis=0)

    _conv_stage(zn_ref, (n + 1) % tiles_per_seq == 0, *conv_args)
    _token_major_stage(ytn_ref, ys_ref)

    h3 = h2 + 0.5 * _swiglu(nrm, w1_ref, w3_ref, w2_ref)
    out_ref[...] = _rms_norm(h3, gf_ref[...])


def _back(h1, z, y_t, gate, zmeta, dw, dwb, lng, lnb, wcp, wv, wg, wout, g3, w1, w3, w2, gf,
          batch, seq, tile):
    tiles = batch * seq // tile
    nxt = lambda n: jnp.minimum(n + 1, tiles - 1)
    row = lambda w: pl.BlockSpec((tile, w), lambda n: (n, 0))
    yt_block = (N_SSM_GROUPS, tile // SSM_CHUNK, CHUNK_LANES)
    return pl.pallas_call(
        functools.partial(_back_kernel, tiles_per_seq=seq // tile),
        grid=(tiles,),
        in_specs=[row(D_MODEL),
                  pl.BlockSpec((tile, D_CONV), lambda n: (0, 0)),
                  pl.BlockSpec((tile, D_CONV), lambda n: (nxt(n), 0)),
                  pl.BlockSpec(yt_block, lambda n: (0, 0, 0)),
                  pl.BlockSpec(yt_block, lambda n: (0, nxt(n), 0)),
                  row(2 * D_MODEL),
                  _resident((CONV_HALO, D_CONV)), _resident((CONV_WIDTH, D_CONV)),
                  _resident((1, D_CONV)), _resident((1, D_CONV)), _resident((1, D_CONV)),
                  _resident((D_CONV, D_MODEL)), _resident((D_SSM, D_MODEL)),
                  _resident((D_SSM, D_MODEL)), _resident((D_MODEL, D_MODEL)),
                  _resident((1, D_MODEL)), _resident((D_MODEL, D_FF)), _resident((D_MODEL, D_FF)),
                  _resident((D_FF, D_MODEL)), _resident((1, D_MODEL))],
        out_specs=row(D_MODEL),
        out_shape=jax.ShapeDtypeStruct((batch * seq, D_MODEL), _f32),
        scratch_shapes=[pltpu.VMEM((CONV_HALO + tile, D_CONV), _f32),
                        pltpu.VMEM((SUBLANES - 1, CONV_HALO + tile, LANES), _f32),
                        pltpu.VMEM((tile, D_CONV), _bf16),
                        pltpu.VMEM((D_SSM // LANES, tile // SSM_CHUNK * CHUNK_PITCH, LANES), _f32)],
        compiler_params=pltpu.CompilerParams(dimension_semantics=("arbitrary",),
                                             vmem_limit_bytes=VMEM_LIMIT),
        name="back",
    )(h1, z, z, y_t, y_t, gate, zmeta, dw, dwb, lng, lnb, wcp, wv, wg, wout, g3, w1, w3, w2, gf)


def kernel(x, meta_tokens, ffn1_norm, ffn1_w1, ffn1_w3, ffn1_w2, mix_norm, w_in, b_gate, conv_dw, conv_dw_b, conv_ln_g, conv_ln_b, conv_proj, ssm_lam_re, ssm_lam_im, ssm_log_dt, ssm_b_re, ssm_b_im, ssm_c_re, ssm_c_im, ssm_d, ssm_w_v, ssm_w_g, w_out, ffn2_norm, ffn2_w1, ffn2_w3, ffn2_w2, final_norm):
    batch, seq, _ = x.shape
    assert ffn1_norm.shape[0] == 1 and seq % ROW_TILE == 0 and ROW_TILE % (SUBLANES * SSM_CHUNK) == 0
    assert N_META <= SSM_CHUNK and N_META <= CONV_HALO
    T, H, G = SSM_CHUNK, SSM_GROUP, N_SSM_GROUPS
    row = lambda v: v.reshape(1, -1)
    dup = lambda a: jnp.concatenate([a, a], axis=-1)

    meta_chunk = jnp.zeros((T, D_MODEL), _f32).at[T - N_META:].set(meta_tokens)
    h1, z, u_t, gate, z_m, u_m = _front(
        x.reshape(batch * seq, D_MODEL), meta_chunk, row(ffn1_norm[0]), row(mix_norm[0]),
        row(b_gate[0]), ffn1_w1[0], ffn1_w3[0], ffn1_w2[0], w_in[0], ROW_TILE)
    um_t = u_m[::-1].reshape(T, G, H).transpose(1, 0, 2).reshape(G, 1, T * H)
    um_t = jnp.concatenate([um_t, jnp.zeros((G, 2 * SUBLANES - 1, T * H), _f32)], axis=1)

    lam2 = jnp.stack([dup(ssm_lam_re[0]), dup(ssm_lam_im[0])], axis=1)
    bt2 = jnp.stack([dup(ssm_b_re[0].transpose(0, 2, 1)), dup(ssm_b_im[0].transpose(0, 2, 1))], axis=1)
    c2 = jnp.stack([dup(ssm_c_re[0]), dup(ssm_c_im[0])], axis=1)
    back_w = (conv_proj[0], ssm_w_v[0], ssm_w_g[0], w_out[0], ffn2_w1[0], ffn2_w3[0], ffn2_w2[0])
    y_t, (wcp, wv, wg, wo, w1b, w3b, w2b) = _ssm(
        u_t, um_t, lam2, ssm_log_dt[0].reshape(G, 1, 1), bt2, c2, ssm_d[0].reshape(G, H, 1),
        seq // T, back_w)

    z_halo = z_m[T - CONV_HALO:].astype(_f32)
    out = _back(h1, z, y_t, gate, z_halo, conv_dw[0], row(conv_dw_b[0]), row(conv_ln_g[0]),
                row(conv_ln_b[0]), wcp, wv, wg, wo, row(ffn2_norm[0]), w1b, w3b, w2b,
                row(final_norm), batch, seq, ROW_TILE)
    return out.reshape(batch, seq, D_MODEL)
```

```python
import functools

import jax
import jax.numpy as jnp
from jax.experimental import pallas as pl
from jax.experimental.pallas import tpu as pltpu

D_MODEL = 1024
N_META = 16
D_FF = 2816
D_CONV = 512
CONV_WIDTH = 31
D_SSM = 512
SSM_GROUP = 16
N_SSM_GROUPS = D_SSM // SSM_GROUP
SSM_STATE = 64
EPS = 1e-6

LANES = 128
SUBLANES = 8
SSM_CHUNK = 32
CHUNK_LANES = SSM_CHUNK * SSM_GROUP
STATE_LANES = 2 * SSM_STATE
GROUPS_PER_VREG = LANES // SSM_GROUP
CONV_HALO = 32
CONV_PIECE_ROWS = (8, 8, 16, 16)
MIX_SLABS = 2
WEIGHT_LOAD_CHUNKS = 8
SSM_GROUPS_PER_STEP = 4
PRM_LAM, PRM_B, PRM_C, PRM_D, PRM_ROWS = 0, 8, 40, 72, 88
CHUNK_PITCH = SSM_CHUNK + SUBLANES
FF_SPLITS = ((0, 1024), (1024, 2048), (2048, 2816))
ROW_TILE = 512
VMEM_LIMIT = 60 * 1024 * 1024

assert STATE_LANES == LANES and GROUPS_PER_VREG == SUBLANES

_bf16 = jnp.bfloat16
_f32 = jnp.float32


def _dot(a, b):
    return jnp.dot(a, b, preferred_element_type=_f32)


def _dot_nt(a, b, precision=None):
    return jax.lax.dot_general(a, b, (((1,), (1,)), ((), ())), precision=precision,
                               preferred_element_type=_f32)


def _rms_norm(x, g):
    return x * jax.lax.rsqrt(jnp.mean(x * x, axis=-1, keepdims=True) + EPS) * g


def _swiglu(n, w1_ref, w3_ref, w2_ref):
    acc = None
    for lo, hi in FF_SPLITS:
        a = _dot(n, w1_ref[:, lo:hi])
        b = _dot(n, w3_ref[:, lo:hi])
        f = (a * jax.nn.sigmoid(a) * b).astype(_bf16)
        part = _dot(f, w2_ref[lo:hi, :])
        acc = part if acc is None else acc + part
    return acc


def _block_transpose8(v):
    blk = jax.lax.broadcasted_iota(jnp.int32, v[0].shape, 1) // SSM_GROUP
    for d in (4, 2, 1):
        hi = (blk & d) != 0
        shift = d * SSM_GROUP
        new = list(v)
        for i in range(GROUPS_PER_VREG):
            if i & d:
                continue
            new[i] = jnp.where(hi, pltpu.roll(v[i + d], shift, axis=1), v[i])
            new[i + d] = jnp.where(hi, v[i + d], pltpu.roll(v[i], LANES - shift, axis=1))
        v = new
    return v


def _load_narrowed(jobs, sem):
    def copy(i, c):
        w_hbm, _, stage_ref = jobs[i]
        chunk = stage_ref.shape[1]
        return pltpu.make_async_copy(w_hbm.at[pl.ds(c * chunk, chunk), :], stage_ref.at[c % 2],
                                     sem.at[i, c % 2])

    for w_hbm, _, stage_ref in jobs:
        assert w_hbm.shape[0] == WEIGHT_LOAD_CHUNKS * stage_ref.shape[1]
    for i in range(len(jobs)):
        copy(i, 0).start()
    for c in range(WEIGHT_LOAD_CHUNKS):
        for i, (_, w_vmem, stage_ref) in enumerate(jobs):
            if c + 1 < WEIGHT_LOAD_CHUNKS:
                copy(i, c + 1).start()
            copy(i, c).wait()
            chunk = stage_ref.shape[1]
            w_vmem[c * chunk:(c + 1) * chunk, :] = stage_ref[c % 2].astype(_bf16)


def _front_rows(x, g1_ref, g2_ref, w1_ref, w3_ref, w2_ref, win_ref):
    n = _rms_norm(x, g1_ref[...]).astype(_bf16)
    h1 = x + 0.5 * _swiglu(n, w1_ref, w3_ref, w2_ref)
    u = _rms_norm(h1, g2_ref[...]).astype(_bf16)
    v = _dot(u, win_ref[:, 0:D_CONV])
    g = _dot(u, win_ref[:, D_CONV:2 * D_CONV])
    z = (v * jax.nn.sigmoid(g)).astype(_bf16)
    us = _dot(u, win_ref[:, 2 * D_CONV:2 * D_CONV + D_SSM])
    return h1, z, us, u


def _front_kernel(x_ref, meta_ref, g1_ref, g2_ref, bg_ref, w1_hbm, w3_hbm, w2_hbm, win_hbm,
                  h1_ref, z_ref, u_ref, gate_ref, zm_ref, um_ref,
                  us_ref, w1_ref, w3_ref, w2_ref, win_ref, stage1_ref, stage3_ref, stage2_ref,
                  stagein_ref, sem):
    weights = (g1_ref, g2_ref, w1_ref, w3_ref, w2_ref, win_ref)

    @pl.when(pl.program_id(0) == 0)
    def _():
        _load_narrowed([(w1_hbm, w1_ref, stage1_ref), (w3_hbm, w3_ref, stage3_ref),
                        (w2_hbm, w2_ref, stage2_ref), (win_hbm, win_ref, stagein_ref)], sem)
        _, zm, usm, _ = _front_rows(meta_ref[...], *weights)
        zm_ref[...] = zm
        um_ref[...] = usm

    h1, z, us, u = _front_rows(x_ref[...], *weights)
    h1_ref[...] = h1
    z_ref[...] = z
    n_chunks = us.shape[0] // SSM_CHUNK
    for j in range(D_SSM // LANES):
        for c in range(n_chunks):
            us_ref[j, c * CHUNK_PITCH:c * CHUNK_PITCH + SSM_CHUNK, :] = (
                us[c * SSM_CHUNK:(c + 1) * SSM_CHUNK, j * LANES:(j + 1) * LANES])
        for q in range(SSM_CHUNK // SUBLANES):
            rows = [us_ref[j, pl.ds(SSM_CHUNK - 1 - (SUBLANES * q + r), n_chunks, stride=CHUNK_PITCH), :]
                    for r in range(SUBLANES)]
            for a, blk in enumerate(_block_transpose8(rows)):
                u_ref[GROUPS_PER_VREG * j + a, :, q * LANES:(q + 1) * LANES] = blk
    gi = _dot(u, win_ref[:, 2 * D_CONV + D_SSM:]) + bg_ref[...]
    gate_ref[...] = jax.nn.sigmoid(gi).astype(_bf16)


def _resident(shape):
    return pl.BlockSpec(shape, lambda *_: (0,) * len(shape), pipeline_mode=pl.Buffered(1))


def _front(x2d, meta_chunk, g1, g2, bg, w1, w3, w2, win, tile):
    rows = x2d.shape[0]
    d_in = win.shape[1]
    n_gate = d_in - 2 * D_CONV - D_SSM
    m_rows = meta_chunk.shape[0]
    row = lambda w: pl.BlockSpec((tile, w), lambda i: (i, 0))
    once = lambda w: pl.BlockSpec((m_rows, w), lambda i: (0, 0))
    hbm = pl.BlockSpec(memory_space=pl.ANY)
    stage_rows = D_MODEL // WEIGHT_LOAD_CHUNKS
    return pl.pallas_call(
        _front_kernel,
        grid=(rows // tile,),
        in_specs=[row(D_MODEL), _resident((m_rows, D_MODEL)), _resident((1, D_MODEL)),
                  _resident((1, D_MODEL)), _resident((1, n_gate)), hbm, hbm, hbm, hbm],
        out_specs=[row(D_MODEL), row(D_CONV),
                   pl.BlockSpec((N_SSM_GROUPS, tile // SSM_CHUNK, CHUNK_LANES), lambda i: (0, i, 0)),
                   row(n_gate), once(D_CONV), once(D_SSM)],
        out_shape=[jax.ShapeDtypeStruct((rows, D_MODEL), _f32),
                   jax.ShapeDtypeStruct((rows, D_CONV), _bf16),
                   jax.ShapeDtypeStruct((N_SSM_GROUPS, rows // SSM_CHUNK, CHUNK_LANES), _f32),
                   jax.ShapeDtypeStruct((rows, n_gate), _bf16),
                   jax.ShapeDtypeStruct((m_rows, D_CONV), _bf16),
                   jax.ShapeDtypeStruct((m_rows, D_SSM), _f32)],
        scratch_shapes=[pltpu.VMEM((D_SSM // LANES, tile // SSM_CHUNK * CHUNK_PITCH, LANES), _f32),
                        pltpu.VMEM((D_MODEL, D_FF), _bf16), pltpu.VMEM((D_MODEL, D_FF), _bf16),
                        pltpu.VMEM((D_FF, D_MODEL), _bf16), pltpu.VMEM((D_MODEL, d_in), _bf16),
                        pltpu.VMEM((2, stage_rows, D_FF), _f32), pltpu.VMEM((2, stage_rows, D_FF), _f32),
                        pltpu.VMEM((2, D_FF // WEIGHT_LOAD_CHUNKS, D_MODEL), _f32),
                        pltpu.VMEM((2, stage_rows, d_in), _f32),
                        pltpu.SemaphoreType.DMA((4, 2))],
        compiler_params=pltpu.CompilerParams(dimension_semantics=("arbitrary",),
                                             vmem_limit_bytes=VMEM_LIMIT),
        name="front",
    )(x2d, meta_chunk, g1, g2, bg, w1, w3, w2, win)


def _ssm_kernel(*refs, chunks_per_seq, levels, groups_per_step, n_cast):
    n_in = len(refs) - 2 * n_cast - 2
    ins, w_f32 = refs[:n_in], refs[n_in:n_in + n_cast]
    y_ref, w_bf16, toep_ref = refs[n_in + n_cast], refs[n_in + n_cast + 1:-1], refs[-1]
    for src, dst in zip(w_f32, w_bf16):
        dst[...] = src[...].astype(_bf16)
    for gi in range(groups_per_step):
        _ssm_group(*[r.at[gi] for r in (*ins, y_ref, toep_ref)],
                   chunks_per_seq=chunks_per_seq, levels=levels)


def _ssm_group(u_ref, um_ref, prm_ref, y_ref, toep_ref, *, chunks_per_seq, levels):
    T, H, P = SSM_CHUNK, SSM_GROUP, SSM_STATE
    lane = jax.lax.broadcasted_iota(jnp.int32, (1, STATE_LANES), 1)
    lo = lane < P
    sgn = jnp.where(lo, -1.0, 1.0)
    lam_re = prm_ref[PRM_LAM:PRM_LAM + 1, :]
    lam_im = prm_ref[PRM_LAM + 1:PRM_LAM + 2, :]
    dt = jnp.exp(prm_ref[PRM_LAM + 2:PRM_LAM + 3, :])
    ar, ai = lam_re * dt, lam_im * dt

    def powers(k):
        mag, ang = jnp.exp(ar * k), ai * k
        cs, sn = jnp.cos(ang), jnp.sin(ang)
        return mag * jnp.where(lo, cs, sn), mag * jnp.where(lo, sn, cs)

    def cmul3(m1, m2, pw, pws):
        return (m1[None] * pw[:, None, :] + m2[None] * pws[:, None, :]).reshape(T * H, STATE_LANES)

    def cmul_rows(v, vs, m1, m2):
        return v * m1 + vs * m2, vs * m1 - v * m2

    n_dbl = (T // SUBLANES).bit_length() - 1
    assert SUBLANES << n_dbl == T and n_dbl < SUBLANES
    row8 = jax.lax.broadcasted_iota(jnp.int32, (SUBLANES, 1), 0)
    aux, aux_s = powers(jnp.where(row8 < n_dbl, jnp.left_shift(SUBLANES, row8),
                                  jnp.where(row8 == n_dbl, 1, 0)).astype(_f32))
    x1 = jnp.where(lo, aux, aux_s)
    x2 = sgn * jnp.where(lo, aux_s, aux)
    pw, pws = powers(row8.astype(_f32))
    for i in range(n_dbl):
        nxt, nxt_s = cmul_rows(pw, pws, x1[i:i + 1], x2[i:i + 1])
        pw, pws = jnp.concatenate([pw, nxt], axis=0), jnp.concatenate([pws, nxt_s], axis=0)
    pw1, pw1s = cmul_rows(pw, pws, x1[n_dbl:n_dbl + 1], x2[n_dbl:n_dbl + 1])

    lb_re, lb_im = x1[n_dbl:n_dbl + 1], sgn * x2[n_dbl:n_dbl + 1]
    den = lam_re * lam_re + lam_im * lam_im
    co_re = ((lb_re - 1.0) * lam_re + lb_im * lam_im) / den
    co_im = (lb_im * lam_re - (lb_re - 1.0) * lam_im) / den
    b_re, b_im = prm_ref[PRM_B:PRM_B + H, :], prm_ref[PRM_B + H:PRM_B + 2 * H, :]
    bb_re = co_re * b_re - co_im * b_im
    bb_im = co_re * b_im + co_im * b_re
    c1, c2 = prm_ref[PRM_C:PRM_C + H, :], sgn * prm_ref[PRM_C + H:PRM_C + 2 * H, :]

    ws = cmul3(bb_re, sgn * bb_im, pw, pws).astype(_bf16)
    wct = (cmul3(c1, c2, pw1, pw1s) * (-sgn)).astype(_bf16)
    e = cmul3(c1, c2, pw, pws)
    r0 = _dot_nt(jnp.where(lo, bb_re, -bb_im), e, precision=jax.lax.Precision.HIGHEST)
    lane_c = jax.lax.broadcasted_iota(jnp.int32, (H, CHUNK_LANES), 1)
    row_c = jax.lax.broadcasted_iota(jnp.int32, (H, CHUNK_LANES), 0)
    r0 = r0 + jnp.where(lane_c == row_c, prm_ref[PRM_D:PRM_D + H, 0:1], 0.0)
    for r in range(SUBLANES):
        rr = pltpu.roll(r0, r * H, axis=1) if r else r0
        for a in range(T // SUBLANES):
            s = SUBLANES * a + r
            blk = pltpu.roll(rr, a * LANES, axis=1) if a else rr
            toep_ref[(T - 1 - s) * H:(T - s) * H, :] = jnp.where(lane_c >= s * H, blk, 0.0).astype(_bf16)

    u = u_ref[...].astype(_bf16)
    rows = u.shape[0]
    s_loc = _dot(u, ws)
    sm = _dot(um_ref[...].astype(_bf16), ws)[0:1, :]

    lvl_col = jax.lax.broadcasted_iota(jnp.int32, (SUBLANES, 1), 0)
    step, step_s = powers(jnp.left_shift(T, lvl_col).astype(_f32))
    a1 = jnp.where(lo, step, step_s)
    a2 = sgn * jnp.where(lo, step_s, step)

    def cmul(v, lvl):
        return a1[lvl:lvl + 1, :] * v + a2[lvl:lvl + 1, :] * pltpu.roll(v, P, axis=1)

    cidx = jax.lax.broadcasted_iota(jnp.int32, (rows, STATE_LANES), 0) % chunks_per_seq
    first = cidx == 0
    st = s_loc + jnp.where(first, cmul(jnp.broadcast_to(sm, s_loc.shape), 0), 0.0)
    for lvl in range(levels):
        shift = 1 << lvl
        prev = jnp.where(cidx >= shift, pltpu.roll(st, shift, axis=0), 0.0)
        st = st + cmul(prev, lvl)
    st_in = jnp.where(first, sm, pltpu.roll(st, 1, axis=0))
    y = _dot(u, toep_ref[...]) + _dot_nt(st_in.astype(_bf16), wct)
    y_ref[...] = jax.nn.gelu(y)


def _ssm(u_t, um_t, prm, chunks_per_seq, weights):
    groups, rows, _ = u_t.shape
    levels = max(1, (chunks_per_seq - 1).bit_length())
    assert levels <= SUBLANES
    gps = SSM_GROUPS_PER_STEP
    steps = groups // gps
    per_group = lambda *s: pl.BlockSpec((gps,) + s, lambda g: (g,) + (0,) * len(s))
    row_block = lambda w: pl.BlockSpec((w.shape[0] // steps, w.shape[1]), lambda g: (g, 0))
    assert all(w.shape[0] % (steps * 2 * SUBLANES) == 0 for w in weights)
    outs = pl.pallas_call(
        functools.partial(_ssm_kernel, chunks_per_seq=chunks_per_seq, levels=levels,
                          groups_per_step=gps, n_cast=len(weights)),
        grid=(steps,),
        in_specs=[per_group(rows, CHUNK_LANES), per_group(um_t.shape[1], CHUNK_LANES),
                  per_group(PRM_ROWS, STATE_LANES)] + [row_block(w) for w in weights],
        out_specs=[per_group(rows, CHUNK_LANES)] + [row_block(w) for w in weights],
        out_shape=[jax.ShapeDtypeStruct((groups, rows, CHUNK_LANES), _f32)]
                  + [jax.ShapeDtypeStruct(w.shape, _bf16) for w in weights],
        scratch_shapes=[pltpu.VMEM((gps, CHUNK_LANES, CHUNK_LANES), _bf16)],
        compiler_params=pltpu.CompilerParams(dimension_semantics=("arbitrary",),
                                             vmem_limit_bytes=VMEM_LIMIT),
        name="ssm",
    )(u_t, um_t, prm, *weights)
    return outs[0], outs[1:]


def _chain_link(acc):
    link = acc[0:SUBLANES]
    for i in range(SUBLANES, acc.shape[0], SUBLANES):
        nxt = acc[i:i + SUBLANES]
        link = jnp.where(nxt != nxt, nxt, link)
    return link


def _ordered_after(x, link):
    nan = link != link
    return jnp.concatenate([jnp.where(nan, link, x[i:i + SUBLANES])
                            for i in range(0, x.shape[0], SUBLANES)], axis=0)


def _conv_stage(z_ref, first_of_seq, zmeta_ref, dw_ref, dwb_ref, lng_ref, lnb_ref,
                zpad_ref, shift_ref, zs_ref):
    tile = z_ref.shape[0]
    padded = CONV_HALO + tile
    if first_of_seq is True:
        zpad_ref[0:CONV_HALO, :] = zmeta_ref[...]
    else:
        zpad_ref[0:CONV_HALO, :] = jnp.where(first_of_seq, zmeta_ref[...], zpad_ref[0:CONV_HALO, :])
    zpad_ref[CONV_HALO:, :] = z_ref[...].astype(_f32)
    base = CONV_HALO - (CONV_WIDTH - 1)
    convs = []
    dep = None
    for j in range(D_CONV // LANES):
        cols = slice(j * LANES, (j + 1) * LANES)
        zp = zpad_ref[:, cols]
        for r in range(1, SUBLANES):
            shift_ref[r - 1] = pltpu.roll(zp, padded - r, axis=0)
        pieces = []
        piece_rows = CONV_PIECE_ROWS[j]
        for p in range(tile // piece_rows):
            acc = None
            for k in range(CONV_WIDTH):
                a, r = divmod(base + k, SUBLANES)
                lo_row = a * SUBLANES + p * piece_rows
                rows = slice(lo_row, lo_row + piece_rows)
                src = zpad_ref[rows, cols] if r == 0 else shift_ref[r - 1, rows, :]
                if acc is None and dep is not None:
                    src = _ordered_after(src, dep)
                term = dw_ref[k:k + 1, cols] * src
                acc = term if acc is None else acc + term
            dep = _chain_link(acc)
            pieces.append(acc + dwb_ref[:, cols])
        convs.append(jnp.concatenate(pieces, axis=0))
    conv = jnp.concatenate(convs, axis=1)
    zpad_ref[0:CONV_HALO, :] = zpad_ref[tile:tile + CONV_HALO, :]
    mu = jnp.mean(conv, axis=-1, keepdims=True)
    cen = conv - mu
    var = jnp.mean(cen * cen, axis=-1, keepdims=True)
    zn = cen * jax.lax.rsqrt(var + EPS) * lng_ref[...] + lnb_ref[...]
    zs_ref[...] = (zn * jax.nn.sigmoid(zn)).astype(_bf16)


def _token_major_stage(yt_ref, ys_ref):
    n_chunks = yt_ref.shape[1]
    for j in range(D_SSM // LANES):
        for q in range(SSM_CHUNK // SUBLANES):
            blks = [yt_ref[GROUPS_PER_VREG * j + a, :, q * LANES:(q + 1) * LANES]
                    for a in range(GROUPS_PER_VREG)]
            for r, rows in enumerate(_block_transpose8(blks)):
                ys_ref[j, pl.ds(SUBLANES * q + r, n_chunks, stride=CHUNK_PITCH), :] = rows


def _back_kernel(h1_ref, z0_ref, zn_ref, yt0_ref, ytn_ref, gate_ref, zmeta_ref, dw_ref, dwb_ref,
                 lng_ref, lnb_ref, wcp_ref, wv_ref, wg_ref, wout_ref, g3_ref, w1_ref, w3_ref, w2_ref,
                 gf_ref, out_ref, zpad_ref, shift_ref, zs_ref, ys_ref, *, tiles_per_seq):
    n = pl.program_id(0)
    conv_args = (zmeta_ref, dw_ref, dwb_ref, lng_ref, lnb_ref, zpad_ref, shift_ref, zs_ref)

    @pl.when(n == 0)
    def _():
        _conv_stage(z0_ref, True, *conv_args)
        _token_major_stage(yt0_ref, ys_ref)

    slab = h1_ref.shape[0] // MIX_SLABS
    h2_parts, nrm_parts = [], []
    for s in range(MIX_SLABS):
        rows = slice(s * slab, (s + 1) * slab)
        zs = zs_ref[rows, :]
        ys = jnp.concatenate(
            [jnp.concatenate([ys_ref[j, c * CHUNK_PITCH:c * CHUNK_PITCH + SSM_CHUNK, :]
                              for c in range(s * slab // SSM_CHUNK, (s + 1) * slab // SSM_CHUNK)], axis=0)
             for j in range(D_SSM // LANES)], axis=1).astype(_bf16)
        y_conv = _dot(zs, wcp_ref[...])
        y_ssm = _dot(ys, wv_ref[...]) * jax.nn.sigmoid(_dot(ys, wg_ref[...]))
        mix = (gate_ref[rows, 0:D_MODEL].astype(_f32) * y_conv
               + gate_ref[rows, D_MODEL:].astype(_f32) * y_ssm).astype(_bf16)
        h2_parts.append(h1_ref[rows, :] + _dot(mix, wout_ref[...]))
        nrm_parts.append(_rms_norm(h2_parts[-1], g3_ref[...]).astype(_bf16))
    h2 = jnp.concatenate(h2_parts, axis=0)
    nrm = jnp.concatenate(nrm_parts, axis=0)

    _conv_stage(zn_ref, (n + 1) % tiles_per_seq == 0, *conv_args)
    _token_major_stage(ytn_ref, ys_ref)

    h3 = h2 + 0.5 * _swiglu(nrm, w1_ref, w3_ref, w2_ref)
    out_ref[...] = _rms_norm(h3, gf_ref[...])


def _back(h1, z, y_t, gate, zmeta, dw, dwb, lng, lnb, wcp, wv, wg, wout, g3, w1, w3, w2, gf,
          batch, seq, tile):
    tiles = batch * seq // tile
    nxt = lambda n: jnp.minimum(n + 1, tiles - 1)
    row = lambda w: pl.BlockSpec((tile, w), lambda n: (n, 0))
    yt_block = (N_SSM_GROUPS, tile // SSM_CHUNK, CHUNK_LANES)
    return pl.pallas_call(
        functools.partial(_back_kernel, tiles_per_seq=seq // tile),
        grid=(tiles,),
        in_specs=[row(D_MODEL),
                  pl.BlockSpec((tile, D_CONV), lambda n: (0, 0)),
                  pl.BlockSpec((tile, D_CONV), lambda n: (nxt(n), 0)),
                  pl.BlockSpec(yt_block, lambda n: (0, 0, 0)),
                  pl.BlockSpec(yt_block, lambda n: (0, nxt(n), 0)),
                  row(2 * D_MODEL),
                  _resident((CONV_HALO, D_CONV)), _resident((CONV_WIDTH, D_CONV)),
                  _resident((1, D_CONV)), _resident((1, D_CONV)), _resident((1, D_CONV)),
                  _resident((D_CONV, D_MODEL)), _resident((D_SSM, D_MODEL)),
                  _resident((D_SSM, D_MODEL)), _resident((D_MODEL, D_MODEL)),
                  _resident((1, D_MODEL)), _resident((D_MODEL, D_FF)), _resident((D_MODEL, D_FF)),
                  _resident((D_FF, D_MODEL)), _resident((1, D_MODEL))],
        out_specs=row(D_MODEL),
        out_shape=jax.ShapeDtypeStruct((batch * seq, D_MODEL), _f32),
        scratch_shapes=[pltpu.VMEM((CONV_HALO + tile, D_CONV), _f32),
                        pltpu.VMEM((SUBLANES - 1, CONV_HALO + tile, LANES), _f32),
                        pltpu.VMEM((tile, D_CONV), _bf16),
                        pltpu.VMEM((D_SSM // LANES, tile // SSM_CHUNK * CHUNK_PITCH, LANES), _f32)],
        compiler_params=pltpu.CompilerParams(dimension_semantics=("arbitrary",),
                                             vmem_limit_bytes=VMEM_LIMIT),
        name="back",
    )(h1, z, z, y_t, y_t, gate, zmeta, dw, dwb, lng, lnb, wcp, wv, wg, wout, g3, w1, w3, w2, gf)


def kernel(x, meta_tokens, ffn1_norm, ffn1_w1, ffn1_w3, ffn1_w2, mix_norm, w_in, b_gate, conv_dw, conv_dw_b, conv_ln_g, conv_ln_b, conv_proj, ssm_lam_re, ssm_lam_im, ssm_log_dt, ssm_b_re, ssm_b_im, ssm_c_re, ssm_c_im, ssm_d, ssm_w_v, ssm_w_g, w_out, ffn2_norm, ffn2_w1, ffn2_w3, ffn2_w2, final_norm):
    batch, seq, _ = x.shape
    assert ffn1_norm.shape[0] == 1 and seq % ROW_TILE == 0 and ROW_TILE % (SUBLANES * SSM_CHUNK) == 0
    assert N_META <= SSM_CHUNK and N_META <= CONV_HALO
    T, H, G = SSM_CHUNK, SSM_GROUP, N_SSM_GROUPS
    row = lambda v: v.reshape(1, -1)
    dup = lambda a: jnp.concatenate([a, a], axis=-1)

    meta_chunk = jnp.zeros((T, D_MODEL), _f32).at[T - N_META:].set(meta_tokens)
    h1, z, u_t, gate, z_m, u_m = _front(
        x.reshape(batch * seq, D_MODEL), meta_chunk, row(ffn1_norm[0]), row(mix_norm[0]),
        row(b_gate[0]), ffn1_w1[0], ffn1_w3[0], ffn1_w2[0], w_in[0], ROW_TILE)
    um_t = u_m[::-1].reshape(T, G, H).transpose(1, 0, 2).reshape(G, 1, T * H)
    um_t = jnp.concatenate([um_t, jnp.zeros((G, 2 * SUBLANES - 1, T * H), _f32)], axis=1)

    lanes = lambda a: jnp.broadcast_to(a[..., None], a.shape + (STATE_LANES,))
    prm = jnp.concatenate(
        [dup(ssm_lam_re[0])[:, None], dup(ssm_lam_im[0])[:, None], lanes(ssm_log_dt[0])[:, None],
         jnp.zeros((G, PRM_B - 3, STATE_LANES), _f32),
         dup(ssm_b_re[0].transpose(0, 2, 1)), dup(ssm_b_im[0].transpose(0, 2, 1)),
         dup(ssm_c_re[0]), dup(ssm_c_im[0]), lanes(ssm_d[0].reshape(G, H))], axis=1)
    assert prm.shape == (G, PRM_ROWS, STATE_LANES)
    back_w = (conv_proj[0], ssm_w_v[0], ssm_w_g[0], w_out[0], ffn2_w1[0], ffn2_w3[0], ffn2_w2[0])
    y_t, (wcp, wv, wg, wo, w1b, w3b, w2b) = _ssm(u_t, um_t, prm, seq // T, back_w)

    z_halo = z_m[T - CONV_HALO:].astype(_f32)
    out = _back(h1, z, y_t, gate, z_halo, conv_dw[0], row(conv_dw_b[0]), row(conv_ln_g[0]),
                row(conv_ln_b[0]), wcp, wv, wg, wo, row(ffn2_norm[0]), w1b, w3b, w2b,
                row(final_norm), batch, seq, ROW_TILE)
    return out.reshape(batch, seq, D_MODEL)
```

```python
import functools

import jax
import jax.numpy as jnp
from jax.experimental import pallas as pl
from jax.experimental.pallas import tpu as pltpu

D_MODEL = 1024
N_META = 16
D_FF = 2816
D_CONV = 512
CONV_WIDTH = 31
D_SSM = 512
SSM_GROUP = 16
N_SSM_GROUPS = D_SSM // SSM_GROUP
SSM_STATE = 64
EPS = 1e-6

LANES = 128
SUBLANES = 8
SSM_CHUNK = 32
CHUNK_LANES = SSM_CHUNK * SSM_GROUP
STATE_LANES = 2 * SSM_STATE
GROUPS_PER_VREG = LANES // SSM_GROUP
CONV_HALO = 32
CONV_LINK_EVERY = 2
MIX_SLABS = 2
WEIGHT_LOAD_CHUNKS = 8
SSM_GROUPS_PER_STEP = 2
CHUNK_PITCH = SSM_CHUNK + SUBLANES
FF_SPLITS = ((0, 1024), (1024, 2048), (2048, 2816))
ROW_TILE = 512
VMEM_LIMIT = 60 * 1024 * 1024

assert STATE_LANES == LANES and GROUPS_PER_VREG == SUBLANES

_bf16 = jnp.bfloat16
_f32 = jnp.float32


def _dot(a, b):
    return jnp.dot(a, b, preferred_element_type=_f32)


def _dot_nt(a, b, precision=None):
    return jax.lax.dot_general(a, b, (((1,), (1,)), ((), ())), precision=precision,
                               preferred_element_type=_f32)


def _rms_norm(x, g):
    return x * jax.lax.rsqrt(jnp.mean(x * x, axis=-1, keepdims=True) + EPS) * g


def _swiglu(n, w1_ref, w3_ref, w2_ref):
    acc = None
    for lo, hi in FF_SPLITS:
        a = _dot(n, w1_ref[:, lo:hi])
        b = _dot(n, w3_ref[:, lo:hi])
        f = (a * jax.nn.sigmoid(a) * b).astype(_bf16)
        part = _dot(f, w2_ref[lo:hi, :])
        acc = part if acc is None else acc + part
    return acc


def _block_transpose8(v):
    blk = jax.lax.broadcasted_iota(jnp.int32, v[0].shape, 1) // SSM_GROUP
    for d in (4, 2, 1):
        hi = (blk & d) != 0
        shift = d * SSM_GROUP
        new = list(v)
        for i in range(GROUPS_PER_VREG):
            if i & d:
                continue
            new[i] = jnp.where(hi, pltpu.roll(v[i + d], shift, axis=1), v[i])
            new[i + d] = jnp.where(hi, v[i + d], pltpu.roll(v[i], LANES - shift, axis=1))
        v = new
    return v


def _load_narrowed(jobs, sem):
    def copy(i, c):
        w_hbm, _, stage_ref = jobs[i]
        chunk = stage_ref.shape[1]
        return pltpu.make_async_copy(w_hbm.at[pl.ds(c * chunk, chunk), :], stage_ref.at[c % 2],
                                     sem.at[i, c % 2])

    for w_hbm, _, stage_ref in jobs:
        assert w_hbm.shape[0] == WEIGHT_LOAD_CHUNKS * stage_ref.shape[1]
    for i in range(len(jobs)):
        copy(i, 0).start()
    for c in range(WEIGHT_LOAD_CHUNKS):
        for i, (_, w_vmem, stage_ref) in enumerate(jobs):
            if c + 1 < WEIGHT_LOAD_CHUNKS:
                copy(i, c + 1).start()
            copy(i, c).wait()
            chunk = stage_ref.shape[1]
            w_vmem[c * chunk:(c + 1) * chunk, :] = stage_ref[c % 2].astype(_bf16)


def _front_rows(x, g1_ref, g2_ref, w1_ref, w3_ref, w2_ref, win_ref):
    n = _rms_norm(x, g1_ref[...]).astype(_bf16)
    h1 = x + 0.5 * _swiglu(n, w1_ref, w3_ref, w2_ref)
    u = _rms_norm(h1, g2_ref[...]).astype(_bf16)
    v = _dot(u, win_ref[:, 0:D_CONV])
    g = _dot(u, win_ref[:, D_CONV:2 * D_CONV])
    z = (v * jax.nn.sigmoid(g)).astype(_bf16)
    us = _dot(u, win_ref[:, 2 * D_CONV:2 * D_CONV + D_SSM])
    return h1, z, us, u


def _front_kernel(x_ref, meta_ref, g1_ref, g2_ref, bg_ref, w1_hbm, w3_hbm, w2_hbm, win_hbm,
                  h1_ref, z_ref, u_ref, gate_ref, zm_ref, um_ref,
                  us_ref, w1_ref, w3_ref, w2_ref, win_ref, stage1_ref, stage3_ref, stage2_ref,
                  stagein_ref, sem):
    weights = (g1_ref, g2_ref, w1_ref, w3_ref, w2_ref, win_ref)

    @pl.when(pl.program_id(0) == 0)
    def _():
        _load_narrowed([(w1_hbm, w1_ref, stage1_ref), (w3_hbm, w3_ref, stage3_ref),
                        (w2_hbm, w2_ref, stage2_ref), (win_hbm, win_ref, stagein_ref)], sem)
        _, zm, usm, _ = _front_rows(meta_ref[...], *weights)
        zm_ref[...] = zm
        um_ref[...] = usm

    h1, z, us, u = _front_rows(x_ref[...], *weights)
    h1_ref[...] = h1
    z_ref[...] = z
    n_chunks = us.shape[0] // SSM_CHUNK
    for j in range(D_SSM // LANES):
        for c in range(n_chunks):
            us_ref[j, c * CHUNK_PITCH:c * CHUNK_PITCH + SSM_CHUNK, :] = (
                us[c * SSM_CHUNK:(c + 1) * SSM_CHUNK, j * LANES:(j + 1) * LANES])
        for q in range(SSM_CHUNK // SUBLANES):
            rows = [us_ref[j, pl.ds(SSM_CHUNK - 1 - (SUBLANES * q + r), n_chunks, stride=CHUNK_PITCH), :]
                    for r in range(SUBLANES)]
            for a, blk in enumerate(_block_transpose8(rows)):
                u_ref[GROUPS_PER_VREG * j + a, :, q * LANES:(q + 1) * LANES] = blk
    gi = _dot(u, win_ref[:, 2 * D_CONV + D_SSM:]) + bg_ref[...]
    gate_ref[...] = jax.nn.sigmoid(gi).astype(_bf16)


def _resident(shape):
    return pl.BlockSpec(shape, lambda *_: (0,) * len(shape), pipeline_mode=pl.Buffered(1))


def _front(x2d, meta_chunk, g1, g2, bg, w1, w3, w2, win, tile):
    rows = x2d.shape[0]
    d_in = win.shape[1]
    n_gate = d_in - 2 * D_CONV - D_SSM
    m_rows = meta_chunk.shape[0]
    row = lambda w: pl.BlockSpec((tile, w), lambda i: (i, 0))
    once = lambda w: pl.BlockSpec((m_rows, w), lambda i: (0, 0))
    hbm = pl.BlockSpec(memory_space=pl.ANY)
    stage_rows = D_MODEL // WEIGHT_LOAD_CHUNKS
    return pl.pallas_call(
        _front_kernel,
        grid=(rows // tile,),
        in_specs=[row(D_MODEL), _resident((m_rows, D_MODEL)), _resident((1, D_MODEL)),
                  _resident((1, D_MODEL)), _resident((1, n_gate)), hbm, hbm, hbm, hbm],
        out_specs=[row(D_MODEL), row(D_CONV),
                   pl.BlockSpec((N_SSM_GROUPS, tile // SSM_CHUNK, CHUNK_LANES), lambda i: (0, i, 0)),
                   row(n_gate), once(D_CONV), once(D_SSM)],
        out_shape=[jax.ShapeDtypeStruct((rows, D_MODEL), _f32),
                   jax.ShapeDtypeStruct((rows, D_CONV), _bf16),
                   jax.ShapeDtypeStruct((N_SSM_GROUPS, rows // SSM_CHUNK, CHUNK_LANES), _f32),
                   jax.ShapeDtypeStruct((rows, n_gate), _bf16),
                   jax.ShapeDtypeStruct((m_rows, D_CONV), _bf16),
                   jax.ShapeDtypeStruct((m_rows, D_SSM), _f32)],
        scratch_shapes=[pltpu.VMEM((D_SSM // LANES, tile // SSM_CHUNK * CHUNK_PITCH, LANES), _f32),
                        pltpu.VMEM((D_MODEL, D_FF), _bf16), pltpu.VMEM((D_MODEL, D_FF), _bf16),
                        pltpu.VMEM((D_FF, D_MODEL), _bf16), pltpu.VMEM((D_MODEL, d_in), _bf16),
                        pltpu.VMEM((2, stage_rows, D_FF), _f32), pltpu.VMEM((2, stage_rows, D_FF), _f32),
                        pltpu.VMEM((2, D_FF // WEIGHT_LOAD_CHUNKS, D_MODEL), _f32),
                        pltpu.VMEM((2, stage_rows, d_in), _f32),
                        pltpu.SemaphoreType.DMA((4, 2))],
        compiler_params=pltpu.CompilerParams(dimension_semantics=("arbitrary",),
                                             vmem_limit_bytes=VMEM_LIMIT),
        name="front",
    )(x2d, meta_chunk, g1, g2, bg, w1, w3, w2, win)


def _ssm_kernel(*refs, chunks_per_seq, levels, groups_per_step, n_cast):
    n_in = len(refs) - 2 * n_cast - 2
    ins, w_f32 = refs[:n_in], refs[n_in:n_in + n_cast]
    y_ref, w_bf16, toep_ref = refs[n_in + n_cast], refs[n_in + n_cast + 1:-1], refs[-1]
    for src, dst in zip(w_f32, w_bf16):
        dst[...] = src[...].astype(_bf16)
    for gi in range(groups_per_step):
        _ssm_group(*[r.at[gi] for r in (*ins, y_ref, toep_ref)],
                   chunks_per_seq=chunks_per_seq, levels=levels)


def _ssm_group(u_ref, um_ref, lam_ref, ldt_ref, bt_ref, c_ref, d_ref, y_ref, toep_ref,
               *, chunks_per_seq, levels):
    T, H, P = SSM_CHUNK, SSM_GROUP, SSM_STATE
    lane = jax.lax.broadcasted_iota(jnp.int32, (1, STATE_LANES), 1)
    lo = lane < P
    sgn = jnp.where(lo, -1.0, 1.0)
    lam_re = lam_ref[0:1, :]
    lam_im = lam_ref[1:2, :]
    dt = jnp.exp(ldt_ref[...])
    ar, ai = lam_re * dt, lam_im * dt

    def powers(k):
        mag, ang = jnp.exp(ar * k), ai * k
        cs, sn = jnp.cos(ang), jnp.sin(ang)
        return mag * jnp.where(lo, cs, sn), mag * jnp.where(lo, sn, cs)

    def cmul3(m1, m2, pw, pws):
        return (m1[None] * pw[:, None, :] + m2[None] * pws[:, None, :]).reshape(T * H, STATE_LANES)

    def cmul_rows(v, vs, m1, m2):
        return v * m1 + vs * m2, vs * m1 - v * m2

    n_dbl = (T // SUBLANES).bit_length() - 1
    assert SUBLANES << n_dbl == T and n_dbl < SUBLANES
    row8 = jax.lax.broadcasted_iota(jnp.int32, (SUBLANES, 1), 0)
    aux, aux_s = powers(jnp.where(row8 < n_dbl, jnp.left_shift(SUBLANES, row8),
                                  jnp.where(row8 == n_dbl, 1, 0)).astype(_f32))
    x1 = jnp.where(lo, aux, aux_s)
    x2 = sgn * jnp.where(lo, aux_s, aux)
    pw, pws = powers(row8.astype(_f32))
    for i in range(n_dbl):
        nxt, nxt_s = cmul_rows(pw, pws, x1[i:i + 1], x2[i:i + 1])
        pw, pws = jnp.concatenate([pw, nxt], axis=0), jnp.concatenate([pws, nxt_s], axis=0)
    pw1, pw1s = cmul_rows(pw, pws, x1[n_dbl:n_dbl + 1], x2[n_dbl:n_dbl + 1])

    lb_re, lb_im = x1[n_dbl:n_dbl + 1], sgn * x2[n_dbl:n_dbl + 1]
    den = lam_re * lam_re + lam_im * lam_im
    co_re = ((lb_re - 1.0) * lam_re + lb_im * lam_im) / den
    co_im = (lb_im * lam_re - (lb_re - 1.0) * lam_im) / den
    b_re, b_im = bt_ref[0], bt_ref[1]
    bb_re = co_re * b_re - co_im * b_im
    bb_im = co_re * b_im + co_im * b_re
    c1, c2 = c_ref[0], sgn * c_ref[1]

    ws = cmul3(bb_re, sgn * bb_im, pw, pws).astype(_bf16)
    wct = (cmul3(c1, c2, pw1, pw1s) * (-sgn)).astype(_bf16)
    e = cmul3(c1, c2, pw, pws)
    r0 = _dot_nt(jnp.where(lo, bb_re, -bb_im), e, precision=jax.lax.Precision.HIGHEST)
    lane_c = jax.lax.broadcasted_iota(jnp.int32, (H, CHUNK_LANES), 1)
    row_c = jax.lax.broadcasted_iota(jnp.int32, (H, CHUNK_LANES), 0)
    r0 = r0 + jnp.where(lane_c == row_c, d_ref[...], 0.0)
    for r in range(SUBLANES):
        rr = pltpu.roll(r0, r * H, axis=1) if r else r0
        for a in range(T // SUBLANES):
            s = SUBLANES * a + r
            blk = pltpu.roll(rr, a * LANES, axis=1) if a else rr
            toep_ref[(T - 1 - s) * H:(T - s) * H, :] = jnp.where(lane_c >= s * H, blk, 0.0).astype(_bf16)

    u = u_ref[...].astype(_bf16)
    rows = u.shape[0]
    s_loc = _dot(u, ws)
    sm = _dot(um_ref[...].astype(_bf16), ws)[0:1, :]

    lvl_col = jax.lax.broadcasted_iota(jnp.int32, (SUBLANES, 1), 0)
    step, step_s = powers(jnp.left_shift(T, lvl_col).astype(_f32))
    a1 = jnp.where(lo, step, step_s)
    a2 = sgn * jnp.where(lo, step_s, step)

    def cmul(v, lvl):
        return a1[lvl:lvl + 1, :] * v + a2[lvl:lvl + 1, :] * pltpu.roll(v, P, axis=1)

    cidx = jax.lax.broadcasted_iota(jnp.int32, (rows, STATE_LANES), 0) % chunks_per_seq
    first = cidx == 0
    st = s_loc + jnp.where(first, cmul(jnp.broadcast_to(sm, s_loc.shape), 0), 0.0)
    for lvl in range(levels):
        shift = 1 << lvl
        prev = jnp.where(cidx >= shift, pltpu.roll(st, shift, axis=0), 0.0)
        st = st + cmul(prev, lvl)
    st_in = jnp.where(first, sm, pltpu.roll(st, 1, axis=0))
    y = _dot(u, toep_ref[...]) + _dot_nt(st_in.astype(_bf16), wct)
    y_ref[...] = jax.nn.gelu(y)


def _ssm(u_t, um_t, lam2, ldt, bt2, c2, d_col, chunks_per_seq, weights):
    groups, rows, _ = u_t.shape
    levels = max(1, (chunks_per_seq - 1).bit_length())
    assert levels <= SUBLANES
    gps = SSM_GROUPS_PER_STEP
    steps = groups // gps
    per_group = lambda *s: pl.BlockSpec((gps,) + s, lambda g: (g,) + (0,) * len(s))
    row_block = lambda w: pl.BlockSpec((w.shape[0] // steps, w.shape[1]), lambda g: (g, 0))
    assert all(w.shape[0] % (steps * 2 * SUBLANES) == 0 for w in weights)
    outs = pl.pallas_call(
        functools.partial(_ssm_kernel, chunks_per_seq=chunks_per_seq, levels=levels,
                          groups_per_step=gps, n_cast=len(weights)),
        grid=(steps,),
        in_specs=[per_group(rows, CHUNK_LANES), per_group(um_t.shape[1], CHUNK_LANES),
                  per_group(2, STATE_LANES), per_group(1, 1),
                  per_group(2, SSM_GROUP, STATE_LANES), per_group(2, SSM_GROUP, STATE_LANES),
                  per_group(SSM_GROUP, 1)] + [row_block(w) for w in weights],
        out_specs=[per_group(rows, CHUNK_LANES)] + [row_block(w) for w in weights],
        out_shape=[jax.ShapeDtypeStruct((groups, rows, CHUNK_LANES), _f32)]
                  + [jax.ShapeDtypeStruct(w.shape, _bf16) for w in weights],
        scratch_shapes=[pltpu.VMEM((gps, CHUNK_LANES, CHUNK_LANES), _bf16)],
        compiler_params=pltpu.CompilerParams(dimension_semantics=("arbitrary",),
                                             vmem_limit_bytes=VMEM_LIMIT),
        name="ssm",
    )(u_t, um_t, lam2, ldt, bt2, c2, d_col, *weights)
    return outs[0], outs[1:]


def _conv_stage(z_ref, first_of_seq, zmeta_ref, dw_ref, dwb_ref, lng_ref, lnb_ref,
                zpad_ref, shift_ref, zs_ref):
    tile = z_ref.shape[0]
    padded = CONV_HALO + tile
    if first_of_seq is True:
        zpad_ref[0:CONV_HALO, :] = zmeta_ref[...]
    else:
        zpad_ref[0:CONV_HALO, :] = jnp.where(first_of_seq, zmeta_ref[...], zpad_ref[0:CONV_HALO, :])
    zpad_ref[CONV_HALO:, :] = z_ref[...].astype(_f32)
    base = CONV_HALO - (CONV_WIDTH - 1)
    convs = []
    dep = None
    for j in range(D_CONV // LANES):
        cols = slice(j * LANES, (j + 1) * LANES)
        zp = zpad_ref[:, cols]
        for r in range(1, SUBLANES):
            shift_ref[r - 1] = pltpu.roll(zp, padded - r, axis=0)
        pieces, acc = [], {}
        n_out, span = tile // SUBLANES, (base + CONV_WIDTH - 1) // SUBLANES
        for v in range(padded // SUBLANES):
            for r in range(SUBLANES):
                taps = [(a, a * SUBLANES + r - base) for a in range(span + 1)
                        if 0 <= a * SUBLANES + r - base < CONV_WIDTH and 0 <= v - a < n_out]
                if not taps:
                    continue
                rows = slice(v * SUBLANES, (v + 1) * SUBLANES)
                src = zpad_ref[rows, cols] if r == 0 else shift_ref[r - 1, rows, :]
                if dep is not None and r % CONV_LINK_EVERY == 0:
                    src = jnp.where(dep != dep, dep, src)
                for a, k in taps:
                    term = dw_ref[k:k + 1, cols] * src
                    acc[v - a] = term if v - a not in acc else acc[v - a] + term
                dep = acc[v - taps[-1][0]]
            if v - span in acc:
                pieces.append(acc.pop(v - span) + dwb_ref[:, cols])
        assert not acc and len(pieces) == n_out
        convs.append(jnp.concatenate(pieces, axis=0))
    conv = jnp.concatenate(convs, axis=1)
    zpad_ref[0:CONV_HALO, :] = zpad_ref[tile:tile + CONV_HALO, :]
    mu = jnp.mean(conv, axis=-1, keepdims=True)
    cen = conv - mu
    var = jnp.mean(cen * cen, axis=-1, keepdims=True)
    zn = cen * jax.lax.rsqrt(var + EPS) * lng_ref[...] + lnb_ref[...]
    zs_ref[...] = (zn * jax.nn.sigmoid(zn)).astype(_bf16)


def _token_major_stage(yt_ref, ys_ref):
    n_chunks = yt_ref.shape[1]
    for j in range(D_SSM // LANES):
        for q in range(SSM_CHUNK // SUBLANES):
            blks = [yt_ref[GROUPS_PER_VREG * j + a, :, q * LANES:(q + 1) * LANES]
                    for a in range(GROUPS_PER_VREG)]
            for r, rows in enumerate(_block_transpose8(blks)):
                ys_ref[j, pl.ds(SUBLANES * q + r, n_chunks, stride=CHUNK_PITCH), :] = rows


def _back_kernel(h1_ref, z0_ref, zn_ref, yt0_ref, ytn_ref, gate_ref, zmeta_ref, dw_ref, dwb_ref,
                 lng_ref, lnb_ref, wcp_ref, wv_ref, wg_ref, wout_ref, g3_ref, w1_ref, w3_ref, w2_ref,
                 gf_ref, out_ref, zpad_ref, shift_ref, zs_ref, ys_ref, *, tiles_per_seq):
    n = pl.program_id(0)
    conv_args = (zmeta_ref, dw_ref, dwb_ref, lng_ref, lnb_ref, zpad_ref, shift_ref, zs_ref)

    @pl.when(n == 0)
    def _():
        _conv_stage(z0_ref, True, *conv_args)
        _token_major_stage(yt0_ref, ys_ref)

    slab = h1_ref.shape[0] // MIX_SLABS
    h2_parts, nrm_parts = [], []
    for s in range(MIX_SLABS):
        rows = slice(s * slab, (s + 1) * slab)
        zs = zs_ref[rows, :]
        ys = jnp.concatenate(
            [jnp.concatenate([ys_ref[j, c * CHUNK_PITCH:c * CHUNK_PITCH + SSM_CHUNK, :]
                              for c in range(s * slab // SSM_CHUNK, (s + 1) * slab // SSM_CHUNK)], axis=0)
             for j in range(D_SSM // LANES)], axis=1).astype(_bf16)
        y_conv = _dot(zs, wcp_ref[...])
        y_ssm = _dot(ys, wv_ref[...]) * jax.nn.sigmoid(_dot(ys, wg_ref[...]))
        mix = (gate_ref[rows, 0:D_MODEL].astype(_f32) * y_conv
               + gate_ref[rows, D_MODEL:].astype(_f32) * y_ssm).astype(_bf16)
        h2_parts.append(h1_ref[rows, :] + _dot(mix, wout_ref[...]))
        nrm_parts.append(_rms_norm(h2_parts[-1], g3_ref[...]).astype(_bf16))
    h2 = jnp.concatenate(h2_parts, axis=0)
    nrm = jnp.concatenate(nrm_parts, axis=0)

    _conv_stage(zn_ref, (n + 1) % tiles_per_seq == 0, *conv_args)
    _token_major_stage(ytn_ref, ys_ref)

    h3 = h2 + 0.5 * _swiglu(nrm, w1_ref, w3_ref, w2_ref)
    out_ref[...] = _rms_norm(h3, gf_ref[...])


def _back(h1, z, y_t, gate, zmeta, dw, dwb, lng, lnb, wcp, wv, wg, wout, g3, w1, w3, w2, gf,
          batch, seq, tile):
    tiles = batch * seq // tile
    nxt = lambda n: jnp.minimum(n + 1, tiles - 1)
    row = lambda w: pl.BlockSpec((tile, w), lambda n: (n, 0))
    yt_block = (N_SSM_GROUPS, tile // SSM_CHUNK, CHUNK_LANES)
    return pl.pallas_call(
        functools.partial(_back_kernel, tiles_per_seq=seq // tile),
        grid=(tiles,),
        in_specs=[row(D_MODEL),
                  pl.BlockSpec((tile, D_CONV), lambda n: (0, 0)),
                  pl.BlockSpec((tile, D_CONV), lambda n: (nxt(n), 0)),
                  pl.BlockSpec(yt_block, lambda n: (0, 0, 0)),
                  pl.BlockSpec(yt_block, lambda n: (0, nxt(n), 0)),
                  row(2 * D_MODEL),
                  _resident((CONV_HALO, D_CONV)), _resident((CONV_WIDTH, D_CONV)),
                  _resident((1, D_CONV)), _resident((1, D_CONV)), _resident((1, D_CONV)),
                  _resident((D_CONV, D_MODEL)), _resident((D_SSM, D_MODEL)),
                  _resident((D_SSM, D_MODEL)), _resident((D_MODEL, D_MODEL)),
                  _resident((1, D_MODEL)), _resident((D_MODEL, D_FF)), _resident((D_MODEL, D_FF)),
                  _resident((D_FF, D_MODEL)), _resident((1, D_MODEL))],
        out_specs=row(D_MODEL),
        out_shape=jax.ShapeDtypeStruct((batch * seq, D_MODEL), _f32),
        scratch_shapes=[pltpu.VMEM((CONV_HALO + tile, D_CONV), _f32),
                        pltpu.VMEM((SUBLANES - 1, CONV_HALO + tile, LANES), _f32),
                        pltpu.VMEM((tile, D_CONV), _bf16),
                        pltpu.VMEM((D_SSM // LANES, tile // SSM_CHUNK * CHUNK_PITCH, LANES), _f32)],
        compiler_params=pltpu.CompilerParams(dimension_semantics=("arbitrary",),
                                             vmem_limit_bytes=VMEM_LIMIT),
        name="back",
    )(h1, z, z, y_t, y_t, gate, zmeta, dw, dwb, lng, lnb, wcp, wv, wg, wout, g3, w1, w3, w2, gf)


def kernel(x, meta_tokens, ffn1_norm, ffn1_w1, ffn1_w3, ffn1_w2, mix_norm, w_in, b_gate, conv_dw, conv_dw_b, conv_ln_g, conv_ln_b, conv_proj, ssm_lam_re, ssm_lam_im, ssm_log_dt, ssm_b_re, ssm_b_im, ssm_c_re, ssm_c_im, ssm_d, ssm_w_v, ssm_w_g, w_out, ffn2_norm, ffn2_w1, ffn2_w3, ffn2_w2, final_norm):
    batch, seq, _ = x.shape
    assert ffn1_norm.shape[0] == 1 and seq % ROW_TILE == 0 and ROW_TILE % (SUBLANES * SSM_CHUNK) == 0
    assert N_META <= SSM_CHUNK and N_META <= CONV_HALO
    T, H, G = SSM_CHUNK, SSM_GROUP, N_SSM_GROUPS
    row = lambda v: v.reshape(1, -1)
    dup = lambda a: jnp.concatenate([a, a], axis=-1)

    meta_chunk = jnp.zeros((T, D_MODEL), _f32).at[T - N_META:].set(meta_tokens)
    h1, z, u_t, gate, z_m, u_m = _front(
        x.reshape(batch * seq, D_MODEL), meta_chunk, row(ffn1_norm[0]), row(mix_norm[0]),
        row(b_gate[0]), ffn1_w1[0], ffn1_w3[0], ffn1_w2[0], w_in[0], ROW_TILE)
    um_t = u_m[::-1].reshape(T, G, H).transpose(1, 0, 2).reshape(G, 1, T * H)
    um_t = jnp.concatenate([um_t, jnp.zeros((G, 2 * SUBLANES - 1, T * H), _f32)], axis=1)

    lam2 = jnp.stack([dup(ssm_lam_re[0]), dup(ssm_lam_im[0])], axis=1)
    bt2 = jnp.stack([dup(ssm_b_re[0].transpose(0, 2, 1)), dup(ssm_b_im[0].transpose(0, 2, 1))], axis=1)
    c2 = jnp.stack([dup(ssm_c_re[0]), dup(ssm_c_im[0])], axis=1)
    back_w = (conv_proj[0], ssm_w_v[0], ssm_w_g[0], w_out[0], ffn2_w1[0], ffn2_w3[0], ffn2_w2[0])
    y_t, (wcp, wv, wg, wo, w1b, w3b, w2b) = _ssm(
        u_t, um_t, lam2, ssm_log_dt[0].reshape(G, 1, 1), bt2, c2, ssm_d[0].reshape(G, H, 1),
        seq // T, back_w)

    z_halo = z_m[T - CONV_HALO:].astype(_f32)
    out = _back(h1, z, y_t, gate, z_halo, conv_dw[0], row(conv_dw_b[0]), row(conv_ln_g[0]),
                row(conv_ln_b[0]), wcp, wv, wg, wo, row(ffn2_norm[0]), w1b, w3b, w2b,
                row(final_norm), batch, seq, ROW_TILE)
    return out.reshape(batch, seq, D_MODEL)
```

```python
import functools

import jax
import jax.numpy as jnp
from jax.experimental import pallas as pl
from jax.experimental.pallas import tpu as pltpu

D_MODEL = 1024
N_META = 16
D_FF = 2816
D_CONV = 512
CONV_WIDTH = 31
D_SSM = 512
SSM_GROUP = 16
N_SSM_GROUPS = D_SSM // SSM_GROUP
SSM_STATE = 64
EPS = 1e-6

LANES = 128
SUBLANES = 8
SSM_CHUNK = 32
CHUNK_LANES = SSM_CHUNK * SSM_GROUP
STATE_LANES = 2 * SSM_STATE
GROUPS_PER_VREG = LANES // SSM_GROUP
CONV_HALO = 32
CONV_LINK_EVERY = 2
MIX_SLABS = 2
WEIGHT_LOAD_CHUNKS = 8
SSM_GROUPS_PER_STEP = 2
CHUNK_PITCH = SSM_CHUNK + SUBLANES
FF_SPLITS = ((0, 1024), (1024, 2048), (2048, 2816))
ROW_TILE = 512
VMEM_LIMIT = 60 * 1024 * 1024

assert STATE_LANES == LANES and GROUPS_PER_VREG == SUBLANES

_bf16 = jnp.bfloat16
_f32 = jnp.float32


def _dot(a, b):
    return jnp.dot(a, b, preferred_element_type=_f32)


def _dot_nt(a, b, precision=None):
    return jax.lax.dot_general(a, b, (((1,), (1,)), ((), ())), precision=precision,
                               preferred_element_type=_f32)


def _rms_norm(x, g):
    return x * jax.lax.rsqrt(jnp.mean(x * x, axis=-1, keepdims=True) + EPS) * g


def _swiglu(n, w1_ref, w3_ref, w2_ref):
    acc = None
    for lo, hi in FF_SPLITS:
        a = _dot(n, w1_ref[:, lo:hi])
        b = _dot(n, w3_ref[:, lo:hi])
        f = (a * jax.nn.sigmoid(a) * b).astype(_bf16)
        part = _dot(f, w2_ref[lo:hi, :])
        acc = part if acc is None else acc + part
    return acc


def _block_transpose8(v):
    blk = jax.lax.broadcasted_iota(jnp.int32, v[0].shape, 1) // SSM_GROUP
    for d in (4, 2, 1):
        hi = (blk & d) != 0
        shift = d * SSM_GROUP
        new = list(v)
        for i in range(GROUPS_PER_VREG):
            if i & d:
                continue
            new[i] = jnp.where(hi, pltpu.roll(v[i + d], shift, axis=1), v[i])
            new[i + d] = jnp.where(hi, v[i + d], pltpu.roll(v[i], LANES - shift, axis=1))
        v = new
    return v


def _load_narrowed(jobs, sem):
    def copy(i, c):
        w_hbm, _, stage_ref = jobs[i]
        chunk = stage_ref.shape[1]
        return pltpu.make_async_copy(w_hbm.at[pl.ds(c * chunk, chunk), :], stage_ref.at[c % 2],
                                     sem.at[i, c % 2])

    for w_hbm, _, stage_ref in jobs:
        assert w_hbm.shape[0] == WEIGHT_LOAD_CHUNKS * stage_ref.shape[1]
    for i in range(len(jobs)):
        copy(i, 0).start()
    for c in range(WEIGHT_LOAD_CHUNKS):
        for i, (_, w_vmem, stage_ref) in enumerate(jobs):
            if c + 1 < WEIGHT_LOAD_CHUNKS:
                copy(i, c + 1).start()
            copy(i, c).wait()
            chunk = stage_ref.shape[1]
            w_vmem[c * chunk:(c + 1) * chunk, :] = stage_ref[c % 2].astype(_bf16)


def _front_rows(x, g1_ref, g2_ref, w1_ref, w3_ref, w2_ref, win_ref):
    n = _rms_norm(x, g1_ref[...]).astype(_bf16)
    h1 = x + 0.5 * _swiglu(n, w1_ref, w3_ref, w2_ref)
    u = _rms_norm(h1, g2_ref[...]).astype(_bf16)
    v = _dot(u, win_ref[:, 0:D_CONV])
    g = _dot(u, win_ref[:, D_CONV:2 * D_CONV])
    z = (v * jax.nn.sigmoid(g)).astype(_bf16)
    us = _dot(u, win_ref[:, 2 * D_CONV:2 * D_CONV + D_SSM])
    return h1, z, us, u


def _front_kernel(x_ref, meta_ref, g1_ref, g2_ref, bg_ref, w1_hbm, w3_hbm, w2_hbm, win_hbm,
                  h1_ref, z_ref, u_ref, gate_ref, zm_ref, um_ref,
                  us_ref, w1_ref, w3_ref, w2_ref, win_ref, stage1_ref, stage3_ref, stage2_ref,
                  stagein_ref, sem):
    weights = (g1_ref, g2_ref, w1_ref, w3_ref, w2_ref, win_ref)

    @pl.when(pl.program_id(0) == 0)
    def _():
        _load_narrowed([(w1_hbm, w1_ref, stage1_ref), (w3_hbm, w3_ref, stage3_ref),
                        (w2_hbm, w2_ref, stage2_ref), (win_hbm, win_ref, stagein_ref)], sem)
        _, zm, usm, _ = _front_rows(meta_ref[...], *weights)
        zm_ref[...] = zm
        um_ref[...] = usm

    h1, z, us, u = _front_rows(x_ref[...], *weights)
    h1_ref[...] = h1
    z_ref[...] = z
    n_chunks = us.shape[0] // SSM_CHUNK
    for j in range(D_SSM // LANES):
        for c in range(n_chunks):
            us_ref[j, c * CHUNK_PITCH:c * CHUNK_PITCH + SSM_CHUNK, :] = (
                us[c * SSM_CHUNK:(c + 1) * SSM_CHUNK, j * LANES:(j + 1) * LANES])
        for q in range(SSM_CHUNK // SUBLANES):
            rows = [us_ref[j, pl.ds(SSM_CHUNK - 1 - (SUBLANES * q + r), n_chunks, stride=CHUNK_PITCH), :]
                    for r in range(SUBLANES)]
            for a, blk in enumerate(_block_transpose8(rows)):
                u_ref[GROUPS_PER_VREG * j + a, :, q * LANES:(q + 1) * LANES] = blk
    gi = _dot(u, win_ref[:, 2 * D_CONV + D_SSM:]) + bg_ref[...]
    gate_ref[...] = jax.nn.sigmoid(gi).astype(_bf16)


def _resident(shape):
    return pl.BlockSpec(shape, lambda *_: (0,) * len(shape), pipeline_mode=pl.Buffered(1))


def _front(x2d, meta_chunk, g1, g2, bg, w1, w3, w2, win, tile):
    rows = x2d.shape[0]
    d_in = win.shape[1]
    n_gate = d_in - 2 * D_CONV - D_SSM
    m_rows = meta_chunk.shape[0]
    row = lambda w: pl.BlockSpec((tile, w), lambda i: (i, 0))
    once = lambda w: pl.BlockSpec((m_rows, w), lambda i: (0, 0))
    hbm = pl.BlockSpec(memory_space=pl.ANY)
    stage_rows = D_MODEL // WEIGHT_LOAD_CHUNKS
    return pl.pallas_call(
        _front_kernel,
        grid=(rows // tile,),
        in_specs=[row(D_MODEL), _resident((m_rows, D_MODEL)), _resident((1, D_MODEL)),
                  _resident((1, D_MODEL)), _resident((1, n_gate)), hbm, hbm, hbm, hbm],
        out_specs=[row(D_MODEL), row(D_CONV),
                   pl.BlockSpec((N_SSM_GROUPS, tile // SSM_CHUNK, CHUNK_LANES), lambda i: (0, i, 0)),
                   row(n_gate), once(D_CONV), once(D_SSM)],
        out_shape=[jax.ShapeDtypeStruct((rows, D_MODEL), _f32),
                   jax.ShapeDtypeStruct((rows, D_CONV), _bf16),
                   jax.ShapeDtypeStruct((N_SSM_GROUPS, rows // SSM_CHUNK, CHUNK_LANES), _f32),
                   jax.ShapeDtypeStruct((rows, n_gate), _bf16),
                   jax.ShapeDtypeStruct((m_rows, D_CONV), _bf16),
                   jax.ShapeDtypeStruct((m_rows, D_SSM), _f32)],
        scratch_shapes=[pltpu.VMEM((D_SSM // LANES, tile // SSM_CHUNK * CHUNK_PITCH, LANES), _f32),
                        pltpu.VMEM((D_MODEL, D_FF), _bf16), pltpu.VMEM((D_MODEL, D_FF), _bf16),
                        pltpu.VMEM((D_FF, D_MODEL), _bf16), pltpu.VMEM((D_MODEL, d_in), _bf16),
                        pltpu.VMEM((2, stage_rows, D_FF), _f32), pltpu.VMEM((2, stage_rows, D_FF), _f32),
                        pltpu.VMEM((2, D_FF // WEIGHT_LOAD_CHUNKS, D_MODEL), _f32),
                        pltpu.VMEM((2, stage_rows, d_in), _f32),
                        pltpu.SemaphoreType.DMA((4, 2))],
        compiler_params=pltpu.CompilerParams(dimension_semantics=("arbitrary",),
                                             vmem_limit_bytes=VMEM_LIMIT),
        name="front",
    )(x2d, meta_chunk, g1, g2, bg, w1, w3, w2, win)


def _ssm_kernel(*refs, chunks_per_seq, levels, groups_per_step, n_cast):
    n_in = len(refs) - 2 * n_cast - 2
    ins, w_f32 = refs[:n_in], refs[n_in:n_in + n_cast]
    y_ref, w_bf16, toep_ref = refs[n_in + n_cast], refs[n_in + n_cast + 1:-1], refs[-1]
    for src, dst in zip(w_f32, w_bf16):
        dst[...] = src[...].astype(_bf16)
    assert groups_per_step == y_ref.shape[0]
    _ssm_groups(*ins, y_ref, toep_ref, chunks_per_seq=chunks_per_seq, levels=levels)


def _per_group(fn, *arrays):
    return jnp.stack([fn(*(a[g] for a in arrays)) for g in range(arrays[0].shape[0])])


def _ssm_groups(u_ref, um_ref, lam_ref, ldt_ref, bt_ref, c_ref, d_ref, y_ref, toep_ref,
                *, chunks_per_seq, levels):
    T, H, P = SSM_CHUNK, SSM_GROUP, SSM_STATE
    lane = jax.lax.broadcasted_iota(jnp.int32, (1, STATE_LANES), 1)
    lo = lane < P
    sgn = jnp.where(lo, -1.0, 1.0)
    lam_re = lam_ref[:, 0:1, :]
    lam_im = lam_ref[:, 1:2, :]
    dt = jnp.exp(ldt_ref[...])
    ar, ai = lam_re * dt, lam_im * dt

    def powers(k):
        mag, ang = jnp.exp(ar * k[None]), ai * k[None]
        cs, sn = jnp.cos(ang), jnp.sin(ang)
        return mag * jnp.where(lo, cs, sn), mag * jnp.where(lo, sn, cs)

    def cmul3(m1, m2, pw, pws):
        prod = m1[:, None] * pw[:, :, None, :] + m2[:, None] * pws[:, :, None, :]
        return prod.reshape(prod.shape[0], T * H, STATE_LANES)

    def cmul_rows(v, vs, m1, m2):
        return v * m1 + vs * m2, vs * m1 - v * m2

    def roll(x, shift, axis):
        return _per_group(lambda a: pltpu.roll(a, shift, axis=axis), x)

    n_dbl = (T // SUBLANES).bit_length() - 1
    assert SUBLANES << n_dbl == T and n_dbl < SUBLANES
    row8 = jax.lax.broadcasted_iota(jnp.int32, (SUBLANES, 1), 0)
    aux, aux_s = powers(jnp.where(row8 < n_dbl, jnp.left_shift(SUBLANES, row8),
                                  jnp.where(row8 == n_dbl, 1, 0)).astype(_f32))
    x1 = jnp.where(lo, aux, aux_s)
    x2 = sgn * jnp.where(lo, aux_s, aux)
    pw, pws = powers(row8.astype(_f32))
    for i in range(n_dbl):
        nxt, nxt_s = cmul_rows(pw, pws, x1[:, i:i + 1], x2[:, i:i + 1])
        pw, pws = jnp.concatenate([pw, nxt], axis=1), jnp.concatenate([pws, nxt_s], axis=1)
    pw1, pw1s = cmul_rows(pw, pws, x1[:, n_dbl:n_dbl + 1], x2[:, n_dbl:n_dbl + 1])

    lb_re, lb_im = x1[:, n_dbl:n_dbl + 1], sgn * x2[:, n_dbl:n_dbl + 1]
    den = lam_re * lam_re + lam_im * lam_im
    co_re = ((lb_re - 1.0) * lam_re + lb_im * lam_im) / den
    co_im = (lb_im * lam_re - (lb_re - 1.0) * lam_im) / den
    b_re, b_im = bt_ref[:, 0], bt_ref[:, 1]
    bb_re = co_re * b_re - co_im * b_im
    bb_im = co_re * b_im + co_im * b_re
    c1, c2 = c_ref[:, 0], sgn * c_ref[:, 1]

    ws = cmul3(bb_re, sgn * bb_im, pw, pws).astype(_bf16)
    wct = (cmul3(c1, c2, pw1, pw1s) * (-sgn)).astype(_bf16)
    e = cmul3(c1, c2, pw, pws)
    r0 = _per_group(functools.partial(_dot_nt, precision=jax.lax.Precision.HIGHEST),
                    jnp.where(lo, bb_re, -bb_im), e)
    lane_c = jax.lax.broadcasted_iota(jnp.int32, (H, CHUNK_LANES), 1)
    row_c = jax.lax.broadcasted_iota(jnp.int32, (H, CHUNK_LANES), 0)
    r0 = r0 + jnp.where(lane_c == row_c, d_ref[...], 0.0)
    for r in range(SUBLANES):
        rr = roll(r0, r * H, 1) if r else r0
        for a in range(T // SUBLANES):
            s = SUBLANES * a + r
            blk = roll(rr, a * LANES, 1) if a else rr
            toep_ref[:, (T - 1 - s) * H:(T - s) * H, :] = jnp.where(lane_c >= s * H, blk, 0.0).astype(_bf16)

    u = u_ref[...].astype(_bf16)
    rows = u.shape[1]
    s_loc = _per_group(_dot, u, ws)
    sm = _per_group(_dot, um_ref[...].astype(_bf16), ws)[:, 0:1, :]

    lvl_col = jax.lax.broadcasted_iota(jnp.int32, (SUBLANES, 1), 0)
    step, step_s = powers(jnp.left_shift(T, lvl_col).astype(_f32))
    a1 = jnp.where(lo, step, step_s)
    a2 = sgn * jnp.where(lo, step_s, step)

    def cmul(v, lvl):
        return a1[:, lvl:lvl + 1, :] * v + a2[:, lvl:lvl + 1, :] * roll(v, P, 1)

    cidx = jax.lax.broadcasted_iota(jnp.int32, (rows, STATE_LANES), 0) % chunks_per_seq
    first = cidx == 0
    st = s_loc + jnp.where(first, cmul(jnp.broadcast_to(sm, s_loc.shape), 0), 0.0)
    for lvl in range(levels):
        shift = 1 << lvl
        prev = jnp.where(cidx >= shift, roll(st, shift, 0), 0.0)
        st = st + cmul(prev, lvl)
    st_in = jnp.where(first, sm, roll(st, 1, 0))
    y = _per_group(_dot, u, toep_ref[...]) + _per_group(_dot_nt, st_in.astype(_bf16), wct)
    y_ref[...] = jax.nn.gelu(y)


def _ssm(u_t, um_t, lam2, ldt, bt2, c2, d_col, chunks_per_seq, weights):
    groups, rows, _ = u_t.shape
    levels = max(1, (chunks_per_seq - 1).bit_length())
    assert levels <= SUBLANES
    gps = SSM_GROUPS_PER_STEP
    steps = groups // gps
    per_group = lambda *s: pl.BlockSpec((gps,) + s, lambda g: (g,) + (0,) * len(s))
    row_block = lambda w: pl.BlockSpec((w.shape[0] // steps, w.shape[1]), lambda g: (g, 0))
    assert all(w.shape[0] % (steps * 2 * SUBLANES) == 0 for w in weights)
    outs = pl.pallas_call(
        functools.partial(_ssm_kernel, chunks_per_seq=chunks_per_seq, levels=levels,
                          groups_per_step=gps, n_cast=len(weights)),
        grid=(steps,),
        in_specs=[per_group(rows, CHUNK_LANES), per_group(um_t.shape[1], CHUNK_LANES),
                  per_group(2, STATE_LANES), per_group(1, 1),
                  per_group(2, SSM_GROUP, STATE_LANES), per_group(2, SSM_GROUP, STATE_LANES),
                  per_group(SSM_GROUP, 1)] + [row_block(w) for w in weights],
        out_specs=[per_group(rows, CHUNK_LANES)] + [row_block(w) for w in weights],
        out_shape=[jax.ShapeDtypeStruct((groups, rows, CHUNK_LANES), _f32)]
                  + [jax.ShapeDtypeStruct(w.shape, _bf16) for w in weights],
        scratch_shapes=[pltpu.VMEM((gps, CHUNK_LANES, CHUNK_LANES), _bf16)],
        compiler_params=pltpu.CompilerParams(dimension_semantics=("arbitrary",),
                                             vmem_limit_bytes=VMEM_LIMIT),
        name="ssm",
    )(u_t, um_t, lam2, ldt, bt2, c2, d_col, *weights)
    return outs[0], outs[1:]


def _conv_stage(z_ref, first_of_seq, zmeta_ref, dw_ref, dwb_ref, lng_ref, lnb_ref,
                zpad_ref, shift_ref, zs_ref):
    tile = z_ref.shape[0]
    padded = CONV_HALO + tile
    if first_of_seq is True:
        zpad_ref[0:CONV_HALO, :] = zmeta_ref[...]
    else:
        zpad_ref[0:CONV_HALO, :] = jnp.where(first_of_seq, zmeta_ref[...], zpad_ref[0:CONV_HALO, :])
    zpad_ref[CONV_HALO:, :] = z_ref[...].astype(_f32)
    base = CONV_HALO - (CONV_WIDTH - 1)
    convs = []
    dep = None
    for j in range(D_CONV // LANES):
        cols = slice(j * LANES, (j + 1) * LANES)
        zp = zpad_ref[:, cols]
        for r in range(1, SUBLANES):
            shift_ref[r - 1] = pltpu.roll(zp, padded - r, axis=0)
        pieces, acc = [], {}
        n_out, span = tile // SUBLANES, (base + CONV_WIDTH - 1) // SUBLANES
        for v in range(padded // SUBLANES):
            for r in range(SUBLANES):
                taps = [(a, a * SUBLANES + r - base) for a in range(span + 1)
                        if 0 <= a * SUBLANES + r - base < CONV_WIDTH and 0 <= v - a < n_out]
                if not taps:
                    continue
                rows = slice(v * SUBLANES, (v + 1) * SUBLANES)
                src = zpad_ref[rows, cols] if r == 0 else shift_ref[r - 1, rows, :]
                if dep is not None and r % CONV_LINK_EVERY == 0:
                    src = jnp.where(dep != dep, dep, src)
                for a, k in taps:
                    term = dw_ref[k:k + 1, cols] * src
                    acc[v - a] = term if v - a not in acc else acc[v - a] + term
                dep = acc[v - taps[-1][0]]
            if v - span in acc:
                pieces.append(acc.pop(v - span) + dwb_ref[:, cols])
        assert not acc and len(pieces) == n_out
        convs.append(jnp.concatenate(pieces, axis=0))
    conv = jnp.concatenate(convs, axis=1)
    zpad_ref[0:CONV_HALO, :] = zpad_ref[tile:tile + CONV_HALO, :]
    mu = jnp.mean(conv, axis=-1, keepdims=True)
    cen = conv - mu
    var = jnp.mean(cen * cen, axis=-1, keepdims=True)
    zn = cen * jax.lax.rsqrt(var + EPS) * lng_ref[...] + lnb_ref[...]
    zs_ref[...] = (zn * jax.nn.sigmoid(zn)).astype(_bf16)


def _token_major_stage(yt_ref, ys_ref):
    n_chunks = yt_ref.shape[1]
    for j in range(D_SSM // LANES):
        for q in range(SSM_CHUNK // SUBLANES):
            blks = [yt_ref[GROUPS_PER_VREG * j + a, :, q * LANES:(q + 1) * LANES]
                    for a in range(GROUPS_PER_VREG)]
            for r, rows in enumerate(_block_transpose8(blks)):
                ys_ref[j, pl.ds(SUBLANES * q + r, n_chunks, stride=CHUNK_PITCH), :] = rows


def _back_kernel(h1_ref, z0_ref, zn_ref, yt0_ref, ytn_ref, gate_ref, zmeta_ref, dw_ref, dwb_ref,
                 lng_ref, lnb_ref, wcp_ref, wv_ref, wg_ref, wout_ref, g3_ref, w1_ref, w3_ref, w2_ref,
                 gf_ref, out_ref, zpad_ref, shift_ref, zs_ref, ys_ref, *, tiles_per_seq):
    n = pl.program_id(0)
    conv_args = (zmeta_ref, dw_ref, dwb_ref, lng_ref, lnb_ref, zpad_ref, shift_ref, zs_ref)

    @pl.when(n == 0)
    def _():
        _conv_stage(z0_ref, True, *conv_args)
        _token_major_stage(yt0_ref, ys_ref)

    slab = h1_ref.shape[0] // MIX_SLABS
    h2_parts, nrm_parts = [], []
    for s in range(MIX_SLABS):
        rows = slice(s * slab, (s + 1) * slab)
        zs = zs_ref[rows, :]
        ys = jnp.concatenate(
            [jnp.concatenate([ys_ref[j, c * CHUNK_PITCH:c * CHUNK_PITCH + SSM_CHUNK, :]
                              for c in range(s * slab // SSM_CHUNK, (s + 1) * slab // SSM_CHUNK)], axis=0)
             for j in range(D_SSM // LANES)], axis=1).astype(_bf16)
        y_conv = _dot(zs, wcp_ref[...])
        y_ssm = _dot(ys, wv_ref[...]) * jax.nn.sigmoid(_dot(ys, wg_ref[...]))
        mix = (gate_ref[rows, 0:D_MODEL].astype(_f32) * y_conv
               + gate_ref[rows, D_MODEL:].astype(_f32) * y_ssm).astype(_bf16)
        h2_parts.append(h1_ref[rows, :] + _dot(mix, wout_ref[...]))
        nrm_parts.append(_rms_norm(h2_parts[-1], g3_ref[...]).astype(_bf16))
    h2 = jnp.concatenate(h2_parts, axis=0)
    nrm = jnp.concatenate(nrm_parts, axis=0)

    _conv_stage(zn_ref, (n + 1) % tiles_per_seq == 0, *conv_args)
    _token_major_stage(ytn_ref, ys_ref)

    h3 = h2 + 0.5 * _swiglu(nrm, w1_ref, w3_ref, w2_ref)
    out_ref[...] = _rms_norm(h3, gf_ref[...])


def _back(h1, z, y_t, gate, zmeta, dw, dwb, lng, lnb, wcp, wv, wg, wout, g3, w1, w3, w2, gf,
          batch, seq, tile):
    tiles = batch * seq // tile
    nxt = lambda n: jnp.minimum(n + 1, tiles - 1)
    row = lambda w: pl.BlockSpec((tile, w), lambda n: (n, 0))
    yt_block = (N_SSM_GROUPS, tile // SSM_CHUNK, CHUNK_LANES)
    return pl.pallas_call(
        functools.partial(_back_kernel, tiles_per_seq=seq // tile),
        grid=(tiles,),
        in_specs=[row(D_MODEL),
                  pl.BlockSpec((tile, D_CONV), lambda n: (0, 0)),
                  pl.BlockSpec((tile, D_CONV), lambda n: (nxt(n), 0)),
                  pl.BlockSpec(yt_block, lambda n: (0, 0, 0)),
                  pl.BlockSpec(yt_block, lambda n: (0, nxt(n), 0)),
                  row(2 * D_MODEL),
                  _resident((CONV_HALO, D_CONV)), _resident((CONV_WIDTH, D_CONV)),
                  _resident((1, D_CONV)), _resident((1, D_CONV)), _resident((1, D_CONV)),
                  _resident((D_CONV, D_MODEL)), _resident((D_SSM, D_MODEL)),
                  _resident((D_SSM, D_MODEL)), _resident((D_MODEL, D_MODEL)),
                  _resident((1, D_MODEL)), _resident((D_MODEL, D_FF)), _resident((D_MODEL, D_FF)),
                  _resident((D_FF, D_MODEL)), _resident((1, D_MODEL))],
        out_specs=row(D_MODEL),
        out_shape=jax.ShapeDtypeStruct((batch * seq, D_MODEL), _f32),
        scratch_shapes=[pltpu.VMEM((CONV_HALO + tile, D_CONV), _f32),
                        pltpu.VMEM((SUBLANES - 1, CONV_HALO + tile, LANES), _f32),
                        pltpu.VMEM((tile, D_CONV), _bf16),
                        pltpu.VMEM((D_SSM // LANES, tile // SSM_CHUNK * CHUNK_PITCH, LANES), _f32)],
        compiler_params=pltpu.CompilerParams(dimension_semantics=("arbitrary",),
                                             vmem_limit_bytes=VMEM_LIMIT),
        name="back",
    )(h1, z, z, y_t, y_t, gate, zmeta, dw, dwb, lng, lnb, wcp, wv, wg, wout, g3, w1, w3, w2, gf)


def kernel(x, meta_tokens, ffn1_norm, ffn1_w1, ffn1_w3, ffn1_w2, mix_norm, w_in, b_gate, conv_dw, conv_dw_b, conv_ln_g, conv_ln_b, conv_proj, ssm_lam_re, ssm_lam_im, ssm_log_dt, ssm_b_re, ssm_b_im, ssm_c_re, ssm_c_im, ssm_d, ssm_w_v, ssm_w_g, w_out, ffn2_norm, ffn2_w1, ffn2_w3, ffn2_w2, final_norm):
    batch, seq, _ = x.shape
    assert ffn1_norm.shape[0] == 1 and seq % ROW_TILE == 0 and ROW_TILE % (SUBLANES * SSM_CHUNK) == 0
    assert N_META <= SSM_CHUNK and N_META <= CONV_HALO
    T, H, G = SSM_CHUNK, SSM_GROUP, N_SSM_GROUPS
    row = lambda v: v.reshape(1, -1)
    dup = lambda a: jnp.concatenate([a, a], axis=-1)

    meta_chunk = jnp.zeros((T, D_MODEL), _f32).at[T - N_META:].set(meta_tokens)
    h1, z, u_t, gate, z_m, u_m = _front(
        x.reshape(batch * seq, D_MODEL), meta_chunk, row(ffn1_norm[0]), row(mix_norm[0]),
        row(b_gate[0]), ffn1_w1[0], ffn1_w3[0], ffn1_w2[0], w_in[0], ROW_TILE)
    um_t = u_m[::-1].reshape(T, G, H).transpose(1, 0, 2).reshape(G, 1, T * H)
    um_t = jnp.concatenate([um_t, jnp.zeros((G, 2 * SUBLANES - 1, T * H), _f32)], axis=1)

    lam2 = jnp.stack([dup(ssm_lam_re[0]), dup(ssm_lam_im[0])], axis=1)
    bt2 = jnp.stack([dup(ssm_b_re[0].transpose(0, 2, 1)), dup(ssm_b_im[0].transpose(0, 2, 1))], axis=1)
    c2 = jnp.stack([dup(ssm_c_re[0]), dup(ssm_c_im[0])], axis=1)
    back_w = (conv_proj[0], ssm_w_v[0], ssm_w_g[0], w_out[0], ffn2_w1[0], ffn2_w3[0], ffn2_w2[0])
    y_t, (wcp, wv, wg, wo, w1b, w3b, w2b) = _ssm(
        u_t, um_t, lam2, ssm_log_dt[0].reshape(G, 1, 1), bt2, c2, ssm_d[0].reshape(G, H, 1),
        seq // T, back_w)

    z_halo = z_m[T - CONV_HALO:].astype(_f32)
    out = _back(h1, z, y_t, gate, z_halo, conv_dw[0], row(conv_dw_b[0]), row(conv_ln_g[0]),
                row(conv_ln_b[0]), wcp, wv, wg, wo, row(ffn2_norm[0]), w1b, w3b, w2b,
                row(final_norm), batch, seq, ROW_TILE)
    return out.reshape(batch, seq, D_MODEL)
```

```python
import functools

import jax
import jax.numpy as jnp
from jax.experimental import pallas as pl
from jax.experimental.pallas import tpu as pltpu

D_MODEL = 1024
N_META = 16
D_FF = 2816
D_CONV = 512
CONV_WIDTH = 31
D_SSM = 512
SSM_GROUP = 16
N_SSM_GROUPS = D_SSM // SSM_GROUP
SSM_STATE = 64
EPS = 1e-6

LANES = 128
SUBLANES = 8
SSM_CHUNK = 32
CHUNK_LANES = SSM_CHUNK * SSM_GROUP
STATE_LANES = 2 * SSM_STATE
GROUPS_PER_VREG = LANES // SSM_GROUP
CONV_HALO = 32
CONV_LINK_EVERY = 2
MIX_SLABS = 2
WEIGHT_LOAD_CHUNKS = 8
SSM_GROUPS_PER_STEP = 4
CHUNK_PITCH = SSM_CHUNK + SUBLANES
FF_SPLITS = ((0, 1024), (1024, 2048), (2048, 2816))
ROW_TILE = 512
VMEM_LIMIT = 60 * 1024 * 1024

assert STATE_LANES == LANES and GROUPS_PER_VREG == SUBLANES

_bf16 = jnp.bfloat16
_f32 = jnp.float32


def _dot(a, b):
    return jnp.dot(a, b, preferred_element_type=_f32)


def _dot_nt(a, b, precision=None):
    return jax.lax.dot_general(a, b, (((1,), (1,)), ((), ())), precision=precision,
                               preferred_element_type=_f32)


def _rms_norm(x, g):
    return x * jax.lax.rsqrt(jnp.mean(x * x, axis=-1, keepdims=True) + EPS) * g


def _swiglu(n, w1_ref, w3_ref, w2_ref):
    acc = None
    for lo, hi in FF_SPLITS:
        a = _dot(n, w1_ref[:, lo:hi])
        b = _dot(n, w3_ref[:, lo:hi])
        f = (a * jax.nn.sigmoid(a) * b).astype(_bf16)
        part = _dot(f, w2_ref[lo:hi, :])
        acc = part if acc is None else acc + part
    return acc


def _block_transpose8(v):
    blk = jax.lax.broadcasted_iota(jnp.int32, v[0].shape, 1) // SSM_GROUP
    for d in (4, 2, 1):
        hi = (blk & d) != 0
        shift = d * SSM_GROUP
        new = list(v)
        for i in range(GROUPS_PER_VREG):
            if i & d:
                continue
            new[i] = jnp.where(hi, pltpu.roll(v[i + d], shift, axis=1), v[i])
            new[i + d] = jnp.where(hi, v[i + d], pltpu.roll(v[i], LANES - shift, axis=1))
        v = new
    return v


def _load_narrowed(jobs, sem):
    def copy(i, c):
        w_hbm, _, stage_ref = jobs[i]
        chunk = stage_ref.shape[1]
        return pltpu.make_async_copy(w_hbm.at[pl.ds(c * chunk, chunk), :], stage_ref.at[c % 2],
                                     sem.at[i, c % 2])

    for w_hbm, _, stage_ref in jobs:
        assert w_hbm.shape[0] == WEIGHT_LOAD_CHUNKS * stage_ref.shape[1]
    for i in range(len(jobs)):
        copy(i, 0).start()
    for c in range(WEIGHT_LOAD_CHUNKS):
        for i, (_, w_vmem, stage_ref) in enumerate(jobs):
            if c + 1 < WEIGHT_LOAD_CHUNKS:
                copy(i, c + 1).start()
            copy(i, c).wait()
            chunk = stage_ref.shape[1]
            w_vmem[c * chunk:(c + 1) * chunk, :] = stage_ref[c % 2].astype(_bf16)


def _front_rows(x, g1_ref, g2_ref, w1_ref, w3_ref, w2_ref, win_ref):
    n = _rms_norm(x, g1_ref[...]).astype(_bf16)
    h1 = x + 0.5 * _swiglu(n, w1_ref, w3_ref, w2_ref)
    u = _rms_norm(h1, g2_ref[...]).astype(_bf16)
    v = _dot(u, win_ref[:, 0:D_CONV])
    g = _dot(u, win_ref[:, D_CONV:2 * D_CONV])
    z = (v * jax.nn.sigmoid(g)).astype(_bf16)
    us = _dot(u, win_ref[:, 2 * D_CONV:2 * D_CONV + D_SSM])
    return h1, z, us, u


def _front_kernel(x_ref, meta_ref, g1_ref, g2_ref, bg_ref, w1_hbm, w3_hbm, w2_hbm, win_hbm,
                  h1_ref, z_ref, u_ref, gate_ref, zm_ref, um_ref,
                  us_ref, w1_ref, w3_ref, w2_ref, win_ref, stage1_ref, stage3_ref, stage2_ref,
                  stagein_ref, sem):
    weights = (g1_ref, g2_ref, w1_ref, w3_ref, w2_ref, win_ref)

    @pl.when(pl.program_id(0) == 0)
    def _():
        _load_narrowed([(w1_hbm, w1_ref, stage1_ref), (w3_hbm, w3_ref, stage3_ref),
                        (w2_hbm, w2_ref, stage2_ref), (win_hbm, win_ref, stagein_ref)], sem)
        _, zm, usm, _ = _front_rows(meta_ref[...], *weights)
        zm_ref[...] = zm
        um_ref[...] = usm

    h1, z, us, u = _front_rows(x_ref[...], *weights)
    h1_ref[...] = h1
    z_ref[...] = z
    n_chunks = us.shape[0] // SSM_CHUNK
    for j in range(D_SSM // LANES):
        for c in range(n_chunks):
            us_ref[j, c * CHUNK_PITCH:c * CHUNK_PITCH + SSM_CHUNK, :] = (
                us[c * SSM_CHUNK:(c + 1) * SSM_CHUNK, j * LANES:(j + 1) * LANES])
        for q in range(SSM_CHUNK // SUBLANES):
            rows = [us_ref[j, pl.ds(SSM_CHUNK - 1 - (SUBLANES * q + r), n_chunks, stride=CHUNK_PITCH), :]
                    for r in range(SUBLANES)]
            for a, blk in enumerate(_block_transpose8(rows)):
                u_ref[GROUPS_PER_VREG * j + a, :, q * LANES:(q + 1) * LANES] = blk
    gi = _dot(u, win_ref[:, 2 * D_CONV + D_SSM:]) + bg_ref[...]
    gate_ref[...] = jax.nn.sigmoid(gi).astype(_bf16)


def _resident(shape):
    return pl.BlockSpec(shape, lambda *_: (0,) * len(shape), pipeline_mode=pl.Buffered(1))


def _front(x2d, meta_chunk, g1, g2, bg, w1, w3, w2, win, tile):
    rows = x2d.shape[0]
    d_in = win.shape[1]
    n_gate = d_in - 2 * D_CONV - D_SSM
    m_rows = meta_chunk.shape[0]
    row = lambda w: pl.BlockSpec((tile, w), lambda i: (i, 0))
    once = lambda w: pl.BlockSpec((m_rows, w), lambda i: (0, 0))
    hbm = pl.BlockSpec(memory_space=pl.ANY)
    stage_rows = D_MODEL // WEIGHT_LOAD_CHUNKS
    return pl.pallas_call(
        _front_kernel,
        grid=(rows // tile,),
        in_specs=[row(D_MODEL), _resident((m_rows, D_MODEL)), _resident((1, D_MODEL)),
                  _resident((1, D_MODEL)), _resident((1, n_gate)), hbm, hbm, hbm, hbm],
        out_specs=[row(D_MODEL), row(D_CONV),
                   pl.BlockSpec((N_SSM_GROUPS, tile // SSM_CHUNK, CHUNK_LANES), lambda i: (0, i, 0)),
                   row(n_gate), once(D_CONV), once(D_SSM)],
        out_shape=[jax.ShapeDtypeStruct((rows, D_MODEL), _f32),
                   jax.ShapeDtypeStruct((rows, D_CONV), _bf16),
                   jax.ShapeDtypeStruct((N_SSM_GROUPS, rows // SSM_CHUNK, CHUNK_LANES), _f32),
                   jax.ShapeDtypeStruct((rows, n_gate), _bf16),
                   jax.ShapeDtypeStruct((m_rows, D_CONV), _bf16),
                   jax.ShapeDtypeStruct((m_rows, D_SSM), _f32)],
        scratch_shapes=[pltpu.VMEM((D_SSM // LANES, tile // SSM_CHUNK * CHUNK_PITCH, LANES), _f32),
                        pltpu.VMEM((D_MODEL, D_FF), _bf16), pltpu.VMEM((D_MODEL, D_FF), _bf16),
                        pltpu.VMEM((D_FF, D_MODEL), _bf16), pltpu.VMEM((D_MODEL, d_in), _bf16),
                        pltpu.VMEM((2, stage_rows, D_FF), _f32), pltpu.VMEM((2, stage_rows, D_FF), _f32),
                        pltpu.VMEM((2, D_FF // WEIGHT_LOAD_CHUNKS, D_MODEL), _f32),
                        pltpu.VMEM((2, stage_rows, d_in), _f32),
                        pltpu.SemaphoreType.DMA((4, 2))],
        compiler_params=pltpu.CompilerParams(dimension_semantics=("arbitrary",),
                                             vmem_limit_bytes=VMEM_LIMIT),
        name="front",
    )(x2d, meta_chunk, g1, g2, bg, w1, w3, w2, win)


def _ssm_kernel(*refs, chunks_per_seq, levels, groups_per_step, n_cast):
    n_in = len(refs) - 2 * n_cast - 2
    ins, w_f32 = refs[:n_in], refs[n_in:n_in + n_cast]
    y_ref, w_bf16, toep_ref = refs[n_in + n_cast], refs[n_in + n_cast + 1:-1], refs[-1]
    for src, dst in zip(w_f32, w_bf16):
        dst[...] = src[...].astype(_bf16)
    assert groups_per_step == y_ref.shape[0]
    _ssm_groups(*ins, y_ref, toep_ref, chunks_per_seq=chunks_per_seq, levels=levels)


def _per_group(fn, *arrays):
    return jnp.stack([fn(*(a[g] for a in arrays)) for g in range(arrays[0].shape[0])])


def _ssm_groups(u_ref, um_ref, lam_ref, ldt_ref, bt_ref, c_ref, d_ref, y_ref, toep_ref,
                *, chunks_per_seq, levels):
    T, H, P = SSM_CHUNK, SSM_GROUP, SSM_STATE
    lane = jax.lax.broadcasted_iota(jnp.int32, (1, STATE_LANES), 1)
    lo = lane < P
    sgn = jnp.where(lo, -1.0, 1.0)
    lam_re = lam_ref[:, 0:1, :]
    lam_im = lam_ref[:, 1:2, :]
    dt = jnp.exp(ldt_ref[...])
    ar, ai = lam_re * dt, lam_im * dt

    def powers(k):
        mag, ang = jnp.exp(ar * k[None]), ai * k[None]
        cs, sn = jnp.cos(ang), jnp.sin(ang)
        return mag * jnp.where(lo, cs, sn), mag * jnp.where(lo, sn, cs)

    def cmul3(m1, m2, pw, pws):
        prod = m1[:, None] * pw[:, :, None, :] + m2[:, None] * pws[:, :, None, :]
        return prod.reshape(prod.shape[0], T * H, STATE_LANES)

    def cmul_rows(v, vs, m1, m2):
        return v * m1 + vs * m2, vs * m1 - v * m2

    def roll(x, shift, axis):
        return _per_group(lambda a: pltpu.roll(a, shift, axis=axis), x)

    n_dbl = (T // SUBLANES).bit_length() - 1
    assert SUBLANES << n_dbl == T and n_dbl < SUBLANES
    row8 = jax.lax.broadcasted_iota(jnp.int32, (SUBLANES, 1), 0)
    aux, aux_s = powers(jnp.where(row8 < n_dbl, jnp.left_shift(SUBLANES, row8),
                                  jnp.where(row8 == n_dbl, 1, 0)).astype(_f32))
    x1 = jnp.where(lo, aux, aux_s)
    x2 = sgn * jnp.where(lo, aux_s, aux)
    pw, pws = powers(row8.astype(_f32))
    for i in range(n_dbl):
        nxt, nxt_s = cmul_rows(pw, pws, x1[:, i:i + 1], x2[:, i:i + 1])
        pw, pws = jnp.concatenate([pw, nxt], axis=1), jnp.concatenate([pws, nxt_s], axis=1)
    pw1, pw1s = cmul_rows(pw, pws, x1[:, n_dbl:n_dbl + 1], x2[:, n_dbl:n_dbl + 1])

    lb_re, lb_im = x1[:, n_dbl:n_dbl + 1], sgn * x2[:, n_dbl:n_dbl + 1]
    den = lam_re * lam_re + lam_im * lam_im
    co_re = ((lb_re - 1.0) * lam_re + lb_im * lam_im) / den
    co_im = (lb_im * lam_re - (lb_re - 1.0) * lam_im) / den
    b_re, b_im = bt_ref[:, 0], bt_ref[:, 1]
    bb_re = co_re * b_re - co_im * b_im
    bb_im = co_re * b_im + co_im * b_re
    c1, c2 = c_ref[:, 0], sgn * c_ref[:, 1]

    ws = cmul3(bb_re, sgn * bb_im, pw, pws).astype(_bf16)
    wct = (cmul3(c1, c2, pw1, pw1s) * (-sgn)).astype(_bf16)
    e = cmul3(c1, c2, pw, pws)
    r0 = _per_group(functools.partial(_dot_nt, precision=jax.lax.Precision.HIGHEST),
                    jnp.where(lo, bb_re, -bb_im), e)
    lane_c = jax.lax.broadcasted_iota(jnp.int32, (H, CHUNK_LANES), 1)
    row_c = jax.lax.broadcasted_iota(jnp.int32, (H, CHUNK_LANES), 0)
    r0 = r0 + jnp.where(lane_c == row_c, d_ref[...], 0.0)
    for r in range(SUBLANES):
        rr = roll(r0, r * H, 1) if r else r0
        for a in range(T // SUBLANES):
            s = SUBLANES * a + r
            blk = roll(rr, a * LANES, 1) if a else rr
            toep_ref[:, (T - 1 - s) * H:(T - s) * H, :] = jnp.where(lane_c >= s * H, blk, 0.0).astype(_bf16)

    u = u_ref[...].astype(_bf16)
    rows = u.shape[1]
    s_loc = _per_group(_dot, u, ws)
    sm = _per_group(_dot, um_ref[...].astype(_bf16), ws)[:, 0:1, :]

    lvl_col = jax.lax.broadcasted_iota(jnp.int32, (SUBLANES, 1), 0)
    step, step_s = powers(jnp.left_shift(T, lvl_col).astype(_f32))
    a1 = jnp.where(lo, step, step_s)
    a2 = sgn * jnp.where(lo, step_s, step)

    def cmul(v, lvl):
        return a1[:, lvl:lvl + 1, :] * v + a2[:, lvl:lvl + 1, :] * roll(v, P, 1)

    cidx = jax.lax.broadcasted_iota(jnp.int32, (rows, STATE_LANES), 0) % chunks_per_seq
    first = cidx == 0
    st = s_loc + jnp.where(first, cmul(jnp.broadcast_to(sm, s_loc.shape), 0), 0.0)
    for lvl in range(levels):
        shift = 1 << lvl
        prev = jnp.where(cidx >= shift, roll(st, shift, 0), 0.0)
        st = st + cmul(prev, lvl)
    st_in = jnp.where(first, sm, roll(st, 1, 0))
    y = _per_group(_dot, u, toep_ref[...]) + _per_group(_dot_nt, st_in.astype(_bf16), wct)
    y_ref[...] = jax.nn.gelu(y)


def _ssm(u_t, um_t, lam2, ldt, bt2, c2, d_col, chunks_per_seq, weights):
    groups, rows, _ = u_t.shape
    levels = max(1, (chunks_per_seq - 1).bit_length())
    assert levels <= SUBLANES
    gps = SSM_GROUPS_PER_STEP
    steps = groups // gps
    per_group = lambda *s: pl.BlockSpec((gps,) + s, lambda g: (g,) + (0,) * len(s))
    row_block = lambda w: pl.BlockSpec((w.shape[0] // steps, w.shape[1]), lambda g: (g, 0))
    assert all(w.shape[0] % (steps * 2 * SUBLANES) == 0 for w in weights)
    outs = pl.pallas_call(
        functools.partial(_ssm_kernel, chunks_per_seq=chunks_per_seq, levels=levels,
                          groups_per_step=gps, n_cast=len(weights)),
        grid=(steps,),
        in_specs=[per_group(rows, CHUNK_LANES), per_group(um_t.shape[1], CHUNK_LANES),
                  per_group(2, STATE_LANES), per_group(1, 1),
                  per_group(2, SSM_GROUP, STATE_LANES), per_group(2, SSM_GROUP, STATE_LANES),
                  per_group(SSM_GROUP, 1)] + [row_block(w) for w in weights],
        out_specs=[per_group(rows, CHUNK_LANES)] + [row_block(w) for w in weights],
        out_shape=[jax.ShapeDtypeStruct((groups, rows, CHUNK_LANES), _f32)]
                  + [jax.ShapeDtypeStruct(w.shape, _bf16) for w in weights],
        scratch_shapes=[pltpu.VMEM((gps, CHUNK_LANES, CHUNK_LANES), _bf16)],
        compiler_params=pltpu.CompilerParams(dimension_semantics=("arbitrary",),
                                             vmem_limit_bytes=VMEM_LIMIT),
        name="ssm",
    )(u_t, um_t, lam2, ldt, bt2, c2, d_col, *weights)
    return outs[0], outs[1:]


def _conv_stage(z_ref, first_of_seq, zmeta_ref, dw_ref, dwb_ref, lng_ref, lnb_ref,
                zpad_ref, shift_ref, zs_ref):
    tile = z_ref.shape[0]
    padded = CONV_HALO + tile
    if first_of_seq is True:
        zpad_ref[0:CONV_HALO, :] = zmeta_ref[...]
    else:
        zpad_ref[0:CONV_HALO, :] = jnp.where(first_of_seq, zmeta_ref[...], zpad_ref[0:CONV_HALO, :])
    zpad_ref[CONV_HALO:, :] = z_ref[...].astype(_f32)
    base = CONV_HALO - (CONV_WIDTH - 1)
    convs = []
    dep = None
    for j in range(D_CONV // LANES):
        cols = slice(j * LANES, (j + 1) * LANES)
        zp = zpad_ref[:, cols]
        for r in range(1, SUBLANES):
            shift_ref[r - 1] = pltpu.roll(zp, padded - r, axis=0)
        pieces, acc = [], {}
        n_out, span = tile // SUBLANES, (base + CONV_WIDTH - 1) // SUBLANES
        for v in range(padded // SUBLANES):
            for r in range(SUBLANES):
                taps = [(a, a * SUBLANES + r - base) for a in range(span + 1)
                        if 0 <= a * SUBLANES + r - base < CONV_WIDTH and 0 <= v - a < n_out]
                if not taps:
                    continue
                rows = slice(v * SUBLANES, (v + 1) * SUBLANES)
                src = zpad_ref[rows, cols] if r == 0 else shift_ref[r - 1, rows, :]
                if dep is not None and r % CONV_LINK_EVERY == 0:
                    src = jnp.where(dep != dep, dep, src)
                for a, k in taps:
                    term = dw_ref[k:k + 1, cols] * src
                    acc[v - a] = term if v - a not in acc else acc[v - a] + term
                dep = acc[v - taps[-1][0]]
            if v - span in acc:
                pieces.append(acc.pop(v - span) + dwb_ref[:, cols])
        assert not acc and len(pieces) == n_out
        convs.append(jnp.concatenate(pieces, axis=0))
    conv = jnp.concatenate(convs, axis=1)
    zpad_ref[0:CONV_HALO, :] = zpad_ref[tile:tile + CONV_HALO, :]
    mu = jnp.mean(conv, axis=-1, keepdims=True)
    cen = conv - mu
    var = jnp.mean(cen * cen, axis=-1, keepdims=True)
    zn = cen * jax.lax.rsqrt(var + EPS) * lng_ref[...] + lnb_ref[...]
    zs_ref[...] = (zn * jax.nn.sigmoid(zn)).astype(_bf16)


def _token_major_stage(yt_ref, ys_ref):
    n_chunks = yt_ref.shape[1]
    for j in range(D_SSM // LANES):
        for q in range(SSM_CHUNK // SUBLANES):
            blks = [yt_ref[GROUPS_PER_VREG * j + a, :, q * LANES:(q + 1) * LANES]
                    for a in range(GROUPS_PER_VREG)]
            for r, rows in enumerate(_block_transpose8(blks)):
                ys_ref[j, pl.ds(SUBLANES * q + r, n_chunks, stride=CHUNK_PITCH), :] = rows


def _back_kernel(h1_ref, z0_ref, zn_ref, yt0_ref, ytn_ref, gate_ref, zmeta_ref, dw_ref, dwb_ref,
                 lng_ref, lnb_ref, wcp_ref, wv_ref, wg_ref, wout_ref, g3_ref, w1_ref, w3_ref, w2_ref,
                 gf_ref, out_ref, zpad_ref, shift_ref, zs_ref, ys_ref, *, tiles_per_seq):
    n = pl.program_id(0)
    conv_args = (zmeta_ref, dw_ref, dwb_ref, lng_ref, lnb_ref, zpad_ref, shift_ref, zs_ref)

    @pl.when(n == 0)
    def _():
        _conv_stage(z0_ref, True, *conv_args)
        _token_major_stage(yt0_ref, ys_ref)

    slab = h1_ref.shape[0] // MIX_SLABS
    h2_parts, nrm_parts = [], []
    for s in range(MIX_SLABS):
        rows = slice(s * slab, (s + 1) * slab)
        zs = zs_ref[rows, :]
        ys = jnp.concatenate(
            [jnp.concatenate([ys_ref[j, c * CHUNK_PITCH:c * CHUNK_PITCH + SSM_CHUNK, :]
                              for c in range(s * slab // SSM_CHUNK, (s + 1) * slab // SSM_CHUNK)], axis=0)
             for j in range(D_SSM // LANES)], axis=1).astype(_bf16)
        y_conv = _dot(zs, wcp_ref[...])
        y_ssm = _dot(ys, wv_ref[...]) * jax.nn.sigmoid(_dot(ys, wg_ref[...]))
        mix = (gate_ref[rows, 0:D_MODEL].astype(_f32) * y_conv
               + gate_ref[rows, D_MODEL:].astype(_f32) * y_ssm).astype(_bf16)
        h2_parts.append(h1_ref[rows, :] + _dot(mix, wout_ref[...]))
        nrm_parts.append(_rms_norm(h2_parts[-1], g3_ref[...]).astype(_bf16))
    h2 = jnp.concatenate(h2_parts, axis=0)
    nrm = jnp.concatenate(nrm_parts, axis=0)

    _conv_stage(zn_ref, (n + 1) % tiles_per_seq == 0, *conv_args)
    _token_major_stage(ytn_ref, ys_ref)

    h3 = h2 + 0.5 * _swiglu(nrm, w1_ref, w3_ref, w2_ref)
    out_ref[...] = _rms_norm(h3, gf_ref[...])


def _back(h1, z, y_t, gate, zmeta, dw, dwb, lng, lnb, wcp, wv, wg, wout, g3, w1, w3, w2, gf,
          batch, seq, tile):
    tiles = batch * seq // tile
    nxt = lambda n: jnp.minimum(n + 1, tiles - 1)
    row = lambda w: pl.BlockSpec((tile, w), lambda n: (n, 0))
    yt_block = (N_SSM_GROUPS, tile // SSM_CHUNK, CHUNK_LANES)
    return pl.pallas_call(
        functools.partial(_back_kernel, tiles_per_seq=seq // tile),
        grid=(tiles,),
        in_specs=[row(D_MODEL),
                  pl.BlockSpec((tile, D_CONV), lambda n: (0, 0)),
                  pl.BlockSpec((tile, D_CONV), lambda n: (nxt(n), 0)),
                  pl.BlockSpec(yt_block, lambda n: (0, 0, 0)),
                  pl.BlockSpec(yt_block, lambda n: (0, nxt(n), 0)),
                  row(2 * D_MODEL),
                  _resident((CONV_HALO, D_CONV)), _resident((CONV_WIDTH, D_CONV)),
                  _resident((1, D_CONV)), _resident((1, D_CONV)), _resident((1, D_CONV)),
                  _resident((D_CONV, D_MODEL)), _resident((D_SSM, D_MODEL)),
                  _resident((D_SSM, D_MODEL)), _resident((D_MODEL, D_MODEL)),
                  _resident((1, D_MODEL)), _resident((D_MODEL, D_FF)), _resident((D_MODEL, D_FF)),
                  _resident((D_FF, D_MODEL)), _resident((1, D_MODEL))],
        out_specs=row(D_MODEL),
        out_shape=jax.ShapeDtypeStruct((batch * seq, D_MODEL), _f32),
        scratch_shapes=[pltpu.VMEM((CONV_HALO + tile, D_CONV), _f32),
                        pltpu.VMEM((SUBLANES - 1, CONV_HALO + tile, LANES), _f32),
                        pltpu.VMEM((tile, D_CONV), _bf16),
                        pltpu.VMEM((D_SSM // LANES, tile // SSM_CHUNK * CHUNK_PITCH, LANES), _f32)],
        compiler_params=pltpu.CompilerParams(dimension_semantics=("arbitrary",),
                                             vmem_limit_bytes=VMEM_LIMIT),
        name="back",
    )(h1, z, z, y_t, y_t, gate, zmeta, dw, dwb, lng, lnb, wcp, wv, wg, wout, g3, w1, w3, w2, gf)


def kernel(x, meta_tokens, ffn1_norm, ffn1_w1, ffn1_w3, ffn1_w2, mix_norm, w_in, b_gate, conv_dw, conv_dw_b, conv_ln_g, conv_ln_b, conv_proj, ssm_lam_re, ssm_lam_im, ssm_log_dt, ssm_b_re, ssm_b_im, ssm_c_re, ssm_c_im, ssm_d, ssm_w_v, ssm_w_g, w_out, ffn2_norm, ffn2_w1, ffn2_w3, ffn2_w2, final_norm):
    batch, seq, _ = x.shape
    assert ffn1_norm.shape[0] == 1 and seq % ROW_TILE == 0 and ROW_TILE % (SUBLANES * SSM_CHUNK) == 0
    assert N_META <= SSM_CHUNK and N_META <= CONV_HALO
    T, H, G = SSM_CHUNK, SSM_GROUP, N_SSM_GROUPS
    row = lambda v: v.reshape(1, -1)
    dup = lambda a: jnp.concatenate([a, a], axis=-1)

    meta_chunk = jnp.zeros((T, D_MODEL), _f32).at[T - N_META:].set(meta_tokens)
    h1, z, u_t, gate, z_m, u_m = _front(
        x.reshape(batch * seq, D_MODEL), meta_chunk, row(ffn1_norm[0]), row(mix_norm[0]),
        row(b_gate[0]), ffn1_w1[0], ffn1_w3[0], ffn1_w2[0], w_in[0], ROW_TILE)
    um_t = u_m[::-1].reshape(T, G, H).transpose(1, 0, 2).reshape(G, 1, T * H)
    um_t = jnp.concatenate([um_t, jnp.zeros((G, 2 * SUBLANES - 1, T * H), _f32)], axis=1)

    lam2 = jnp.stack([dup(ssm_lam_re[0]), dup(ssm_lam_im[0])], axis=1)
    bt2 = jnp.stack([dup(ssm_b_re[0].transpose(0, 2, 1)), dup(ssm_b_im[0].transpose(0, 2, 1))], axis=1)
    c2 = jnp.stack([dup(ssm_c_re[0]), dup(ssm_c_im[0])], axis=1)
    back_w = (conv_proj[0], ssm_w_v[0], ssm_w_g[0], w_out[0], ffn2_w1[0], ffn2_w3[0], ffn2_w2[0])
    y_t, (wcp, wv, wg, wo, w1b, w3b, w2b) = _ssm(
        u_t, um_t, lam2, ssm_log_dt[0].reshape(G, 1, 1), bt2, c2, ssm_d[0].reshape(G, H, 1),
        seq // T, back_w)

    z_halo = z_m[T - CONV_HALO:].astype(_f32)
    out = _back(h1, z, y_t, gate, z_halo, conv_dw[0], row(conv_dw_b[0]), row(conv_ln_g[0]),
                row(conv_ln_b[0]), wcp, wv, wg, wo, row(ffn2_norm[0]), w1b, w3b, w2b,
                row(final_norm), batch, seq, ROW_TILE)
    return out.reshape(batch, seq, D_MODEL)
```

```python
import functools

import jax
import jax.numpy as jnp
from jax.experimental import pallas as pl
from jax.experimental.pallas import tpu as pltpu

D_MODEL = 1024
N_META = 16
D_FF = 2816
D_CONV = 512
CONV_WIDTH = 31
D_SSM = 512
SSM_GROUP = 16
N_SSM_GROUPS = D_SSM // SSM_GROUP
SSM_STATE = 64
EPS = 1e-6

LANES = 128
SUBLANES = 8
SSM_CHUNK = 32
CHUNK_LANES = SSM_CHUNK * SSM_GROUP
STATE_LANES = 2 * SSM_STATE
GROUPS_PER_VREG = LANES // SSM_GROUP
CONV_HALO = 32
CONV_LINK_EVERY = 2
MIX_SLABS = 2
WEIGHT_LOAD_CHUNKS = 8
SSM_GROUPS_PER_STEP = 4
CHUNK_PITCH = SSM_CHUNK + SUBLANES
FF_SPLITS = ((0, 1024), (1024, 2048), (2048, 2816))
ROW_TILE = 512
VMEM_LIMIT = 60 * 1024 * 1024

assert STATE_LANES == LANES and GROUPS_PER_VREG == SUBLANES

_bf16 = jnp.bfloat16
_f32 = jnp.float32


def _dot(a, b):
    return jnp.dot(a, b, preferred_element_type=_f32)


def _dot_nt(a, b, precision=None):
    return jax.lax.dot_general(a, b, (((1,), (1,)), ((), ())), precision=precision,
                               preferred_element_type=_f32)


def _sigmoid(x):
    return 0.5 * jnp.tanh(0.5 * x) + 0.5


def _rms_norm(x, g):
    return x * jax.lax.rsqrt(jnp.mean(x * x, axis=-1, keepdims=True) + EPS) * g


def _swiglu(n, w1_ref, w3_ref, w2_ref):
    acc = None
    for lo, hi in FF_SPLITS:
        a = _dot(n, w1_ref[:, lo:hi])
        b = _dot(n, w3_ref[:, lo:hi])
        f = (a * _sigmoid(a) * b).astype(_bf16)
        part = _dot(f, w2_ref[lo:hi, :])
        acc = part if acc is None else acc + part
    return acc


def _block_transpose8(v):
    blk = jax.lax.broadcasted_iota(jnp.int32, v[0].shape, 1) // SSM_GROUP
    for d in (4, 2, 1):
        hi = (blk & d) != 0
        shift = d * SSM_GROUP
        new = list(v)
        for i in range(GROUPS_PER_VREG):
            if i & d:
                continue
            new[i] = jnp.where(hi, pltpu.roll(v[i + d], shift, axis=1), v[i])
            new[i + d] = jnp.where(hi, v[i + d], pltpu.roll(v[i], LANES - shift, axis=1))
        v = new
    return v


def _load_narrowed(jobs, sem):
    def copy(i, c):
        w_hbm, _, stage_ref = jobs[i]
        chunk = stage_ref.shape[1]
        return pltpu.make_async_copy(w_hbm.at[pl.ds(c * chunk, chunk), :], stage_ref.at[c % 2],
                                     sem.at[i, c % 2])

    for w_hbm, _, stage_ref in jobs:
        assert w_hbm.shape[0] == WEIGHT_LOAD_CHUNKS * stage_ref.shape[1]
    for i in range(len(jobs)):
        copy(i, 0).start()
    for c in range(WEIGHT_LOAD_CHUNKS):
        for i, (_, w_vmem, stage_ref) in enumerate(jobs):
            if c + 1 < WEIGHT_LOAD_CHUNKS:
                copy(i, c + 1).start()
            copy(i, c).wait()
            chunk = stage_ref.shape[1]
            w_vmem[c * chunk:(c + 1) * chunk, :] = stage_ref[c % 2].astype(_bf16)


def _front_rows(x, g1_ref, g2_ref, w1_ref, w3_ref, w2_ref, win_ref):
    n = _rms_norm(x, g1_ref[...]).astype(_bf16)
    h1 = x + 0.5 * _swiglu(n, w1_ref, w3_ref, w2_ref)
    u = _rms_norm(h1, g2_ref[...]).astype(_bf16)
    v = _dot(u, win_ref[:, 0:D_CONV])
    g = _dot(u, win_ref[:, D_CONV:2 * D_CONV])
    z = (v * _sigmoid(g)).astype(_bf16)
    us = _dot(u, win_ref[:, 2 * D_CONV:2 * D_CONV + D_SSM])
    return h1, z, us, u


def _front_kernel(x_ref, meta_ref, g1_ref, g2_ref, bg_ref, w1_hbm, w3_hbm, w2_hbm, win_hbm,
                  h1_ref, z_ref, u_ref, gate_ref, zm_ref, um_ref,
                  us_ref, w1_ref, w3_ref, w2_ref, win_ref, stage1_ref, stage3_ref, stage2_ref,
                  stagein_ref, sem):
    weights = (g1_ref, g2_ref, w1_ref, w3_ref, w2_ref, win_ref)

    @pl.when(pl.program_id(0) == 0)
    def _():
        _load_narrowed([(w1_hbm, w1_ref, stage1_ref), (w3_hbm, w3_ref, stage3_ref),
                        (w2_hbm, w2_ref, stage2_ref), (win_hbm, win_ref, stagein_ref)], sem)
        _, zm, usm, _ = _front_rows(meta_ref[...], *weights)
        zm_ref[...] = zm
        um_ref[...] = usm

    h1, z, us, u = _front_rows(x_ref[...], *weights)
    h1_ref[...] = h1
    z_ref[...] = z
    n_chunks = us.shape[0] // SSM_CHUNK
    for j in range(D_SSM // LANES):
        for c in range(n_chunks):
            us_ref[j, c * CHUNK_PITCH:c * CHUNK_PITCH + SSM_CHUNK, :] = (
                us[c * SSM_CHUNK:(c + 1) * SSM_CHUNK, j * LANES:(j + 1) * LANES])
        for q in range(SSM_CHUNK // SUBLANES):
            rows = [us_ref[j, pl.ds(SSM_CHUNK - 1 - (SUBLANES * q + r), n_chunks, stride=CHUNK_PITCH), :]
                    for r in range(SUBLANES)]
            for a, blk in enumerate(_block_transpose8(rows)):
                u_ref[GROUPS_PER_VREG * j + a, :, q * LANES:(q + 1) * LANES] = blk
    gi = _dot(u, win_ref[:, 2 * D_CONV + D_SSM:]) + bg_ref[...]
    gate_ref[...] = _sigmoid(gi).astype(_bf16)


def _resident(shape):
    return pl.BlockSpec(shape, lambda *_: (0,) * len(shape), pipeline_mode=pl.Buffered(1))


def _front(x2d, meta_chunk, g1, g2, bg, w1, w3, w2, win, tile):
    rows = x2d.shape[0]
    d_in = win.shape[1]
    n_gate = d_in - 2 * D_CONV - D_SSM
    m_rows = meta_chunk.shape[0]
    row = lambda w: pl.BlockSpec((tile, w), lambda i: (i, 0))
    once = lambda w: pl.BlockSpec((m_rows, w), lambda i: (0, 0))
    hbm = pl.BlockSpec(memory_space=pl.ANY)
    stage_rows = D_MODEL // WEIGHT_LOAD_CHUNKS
    return pl.pallas_call(
        _front_kernel,
        grid=(rows // tile,),
        in_specs=[row(D_MODEL), _resident((m_rows, D_MODEL)), _resident((1, D_MODEL)),
                  _resident((1, D_MODEL)), _resident((1, n_gate)), hbm, hbm, hbm, hbm],
        out_specs=[row(D_MODEL), row(D_CONV),
                   pl.BlockSpec((N_SSM_GROUPS, tile // SSM_CHUNK, CHUNK_LANES), lambda i: (0, i, 0)),
                   row(n_gate), once(D_CONV), once(D_SSM)],
        out_shape=[jax.ShapeDtypeStruct((rows, D_MODEL), _f32),
                   jax.ShapeDtypeStruct((rows, D_CONV), _bf16),
                   jax.ShapeDtypeStruct((N_SSM_GROUPS, rows // SSM_CHUNK, CHUNK_LANES), _f32),
                   jax.ShapeDtypeStruct((rows, n_gate), _bf16),
                   jax.ShapeDtypeStruct((m_rows, D_CONV), _bf16),
                   jax.ShapeDtypeStruct((m_rows, D_SSM), _f32)],
        scratch_shapes=[pltpu.VMEM((D_SSM // LANES, tile // SSM_CHUNK * CHUNK_PITCH, LANES), _f32),
                        pltpu.VMEM((D_MODEL, D_FF), _bf16), pltpu.VMEM((D_MODEL, D_FF), _bf16),
                        pltpu.VMEM((D_FF, D_MODEL), _bf16), pltpu.VMEM((D_MODEL, d_in), _bf16),
                        pltpu.VMEM((2, stage_rows, D_FF), _f32), pltpu.VMEM((2, stage_rows, D_FF), _f32),
                        pltpu.VMEM((2, D_FF // WEIGHT_LOAD_CHUNKS, D_MODEL), _f32),
                        pltpu.VMEM((2, stage_rows, d_in), _f32),
                        pltpu.SemaphoreType.DMA((4, 2))],
        compiler_params=pltpu.CompilerParams(dimension_semantics=("arbitrary",),
                                             vmem_limit_bytes=VMEM_LIMIT),
        name="front",
    )(x2d, meta_chunk, g1, g2, bg, w1, w3, w2, win)


def _ssm_kernel(*refs, chunks_per_seq, levels, groups_per_step, n_cast):
    n_in = len(refs) - 2 * n_cast - 2
    ins, w_f32 = refs[:n_in], refs[n_in:n_in + n_cast]
    y_ref, w_bf16, toep_ref = refs[n_in + n_cast], refs[n_in + n_cast + 1:-1], refs[-1]
    for src, dst in zip(w_f32, w_bf16):
        dst[...] = src[...].astype(_bf16)
    assert groups_per_step == y_ref.shape[0]
    _ssm_groups(*ins, y_ref, toep_ref, chunks_per_seq=chunks_per_seq, levels=levels)


def _per_group(fn, *arrays):
    return jnp.stack([fn(*(a[g] for a in arrays)) for g in range(arrays[0].shape[0])])


def _ssm_groups(u_ref, um_ref, lam_ref, ldt_ref, bt_ref, c_ref, d_ref, y_ref, toep_ref,
                *, chunks_per_seq, levels):
    T, H, P = SSM_CHUNK, SSM_GROUP, SSM_STATE
    lane = jax.lax.broadcasted_iota(jnp.int32, (1, STATE_LANES), 1)
    lo = lane < P
    sgn = jnp.where(lo, -1.0, 1.0)
    lam_re = lam_ref[:, 0:1, :]
    lam_im = lam_ref[:, 1:2, :]
    dt = jnp.exp(ldt_ref[...])
    ar, ai = lam_re * dt, lam_im * dt

    def powers(k):
        mag, ang = jnp.exp(ar * k[None]), ai * k[None]
        cs, sn = jnp.cos(ang), jnp.sin(ang)
        return mag * jnp.where(lo, cs, sn), mag * jnp.where(lo, sn, cs)

    def cmul3(m1, m2, pw, pws):
        prod = m1[:, None] * pw[:, :, None, :] + m2[:, None] * pws[:, :, None, :]
        return prod.reshape(prod.shape[0], T * H, STATE_LANES)

    def cmul_rows(v, vs, m1, m2):
        return v * m1 + vs * m2, vs * m1 - v * m2

    def roll(x, shift, axis):
        return _per_group(lambda a: pltpu.roll(a, shift, axis=axis), x)

    n_dbl = (T // SUBLANES).bit_length() - 1
    assert SUBLANES << n_dbl == T and n_dbl < SUBLANES
    row8 = jax.lax.broadcasted_iota(jnp.int32, (SUBLANES, 1), 0)
    aux, aux_s = powers(jnp.where(row8 < n_dbl, jnp.left_shift(SUBLANES, row8),
                                  jnp.where(row8 == n_dbl, 1, 0)).astype(_f32))
    x1 = jnp.where(lo, aux, aux_s)
    x2 = sgn * jnp.where(lo, aux_s, aux)
    pw, pws = powers(row8.astype(_f32))
    for i in range(n_dbl):
        nxt, nxt_s = cmul_rows(pw, pws, x1[:, i:i + 1], x2[:, i:i + 1])
        pw, pws = jnp.concatenate([pw, nxt], axis=1), jnp.concatenate([pws, nxt_s], axis=1)
    pw1, pw1s = cmul_rows(pw, pws, x1[:, n_dbl:n_dbl + 1], x2[:, n_dbl:n_dbl + 1])

    lb_re, lb_im = x1[:, n_dbl:n_dbl + 1], sgn * x2[:, n_dbl:n_dbl + 1]
    den = lam_re * lam_re + lam_im * lam_im
    co_re = ((lb_re - 1.0) * lam_re + lb_im * lam_im) / den
    co_im = (lb_im * lam_re - (lb_re - 1.0) * lam_im) / den
    b_re, b_im = bt_ref[:, 0], bt_ref[:, 1]
    bb_re = co_re * b_re - co_im * b_im
    bb_im = co_re * b_im + co_im * b_re
    c1, c2 = c_ref[:, 0], sgn * c_ref[:, 1]

    ws = cmul3(bb_re, sgn * bb_im, pw, pws).astype(_bf16)
    wct = (cmul3(c1, c2, pw1, pw1s) * (-sgn)).astype(_bf16)
    e = cmul3(c1, c2, pw, pws)
    r0 = _per_group(functools.partial(_dot_nt, precision=jax.lax.Precision.HIGHEST),
                    jnp.where(lo, bb_re, -bb_im), e)
    lane_c = jax.lax.broadcasted_iota(jnp.int32, (H, CHUNK_LANES), 1)
    row_c = jax.lax.broadcasted_iota(jnp.int32, (H, CHUNK_LANES), 0)
    r0 = r0 + jnp.where(lane_c == row_c, d_ref[...], 0.0)
    for r in range(SUBLANES):
        rr = roll(r0, r * H, 1) if r else r0
        for a in range(T // SUBLANES):
            s = SUBLANES * a + r
            blk = roll(rr, a * LANES, 1) if a else rr
            toep_ref[:, (T - 1 - s) * H:(T - s) * H, :] = jnp.where(lane_c >= s * H, blk, 0.0).astype(_bf16)

    u = u_ref[...].astype(_bf16)
    rows = u.shape[1]
    s_loc = _per_group(_dot, u, ws)
    sm = _per_group(_dot, um_ref[...].astype(_bf16), ws)[:, 0:1, :]

    lvl_col = jax.lax.broadcasted_iota(jnp.int32, (SUBLANES, 1), 0)
    step, step_s = powers(jnp.left_shift(T, lvl_col).astype(_f32))
    a1 = jnp.where(lo, step, step_s)
    a2 = sgn * jnp.where(lo, step_s, step)

    def cmul(v, lvl):
        return a1[:, lvl:lvl + 1, :] * v + a2[:, lvl:lvl + 1, :] * roll(v, P, 1)

    cidx = jax.lax.broadcasted_iota(jnp.int32, (rows, STATE_LANES), 0) % chunks_per_seq
    first = cidx == 0
    st = s_loc + jnp.where(first, cmul(jnp.broadcast_to(sm, s_loc.shape), 0), 0.0)
    for lvl in range(levels):
        shift = 1 << lvl
        prev = jnp.where(cidx >= shift, roll(st, shift, 0), 0.0)
        st = st + cmul(prev, lvl)
    st_in = jnp.where(first, sm, roll(st, 1, 0))
    y = _per_group(_dot, u, toep_ref[...]) + _per_group(_dot_nt, st_in.astype(_bf16), wct)
    y_ref[...] = jax.nn.gelu(y)


def _ssm(u_t, um_t, lam2, ldt, bt2, c2, d_col, chunks_per_seq, weights):
    groups, rows, _ = u_t.shape
    levels = max(1, (chunks_per_seq - 1).bit_length())
    assert levels <= SUBLANES
    gps = SSM_GROUPS_PER_STEP
    steps = groups // gps
    per_group = lambda *s: pl.BlockSpec((gps,) + s, lambda g: (g,) + (0,) * len(s))
    row_block = lambda w: pl.BlockSpec((w.shape[0] // steps, w.shape[1]), lambda g: (g, 0))
    assert all(w.shape[0] % (steps * 2 * SUBLANES) == 0 for w in weights)
    outs = pl.pallas_call(
        functools.partial(_ssm_kernel, chunks_per_seq=chunks_per_seq, levels=levels,
                          groups_per_step=gps, n_cast=len(weights)),
        grid=(steps,),
        in_specs=[per_group(rows, CHUNK_LANES), per_group(um_t.shape[1], CHUNK_LANES),
                  per_group(2, STATE_LANES), per_group(1, 1),
                  per_group(2, SSM_GROUP, STATE_LANES), per_group(2, SSM_GROUP, STATE_LANES),
                  per_group(SSM_GROUP, 1)] + [row_block(w) for w in weights],
        out_specs=[per_group(rows, CHUNK_LANES)] + [row_block(w) for w in weights],
        out_shape=[jax.ShapeDtypeStruct((groups, rows, CHUNK_LANES), _f32)]
                  + [jax.ShapeDtypeStruct(w.shape, _bf16) for w in weights],
        scratch_shapes=[pltpu.VMEM((gps, CHUNK_LANES, CHUNK_LANES), _bf16)],
        compiler_params=pltpu.CompilerParams(dimension_semantics=("arbitrary",),
                                             vmem_limit_bytes=VMEM_LIMIT),
        name="ssm",
    )(u_t, um_t, lam2, ldt, bt2, c2, d_col, *weights)
    return outs[0], outs[1:]


def _conv_stage(z_ref, first_of_seq, zmeta_ref, dw_ref, dwb_ref, lng_ref, lnb_ref,
                zpad_ref, shift_ref, zs_ref):
    tile = z_ref.shape[0]
    padded = CONV_HALO + tile
    if first_of_seq is True:
        zpad_ref[0:CONV_HALO, :] = zmeta_ref[...]
    else:
        zpad_ref[0:CONV_HALO, :] = jnp.where(first_of_seq, zmeta_ref[...], zpad_ref[0:CONV_HALO, :])
    zpad_ref[CONV_HALO:, :] = z_ref[...].astype(_f32)
    base = CONV_HALO - (CONV_WIDTH - 1)
    convs = []
    dep = None
    for j in range(D_CONV // LANES):
        cols = slice(j * LANES, (j + 1) * LANES)
        zp = zpad_ref[:, cols]
        for r in range(1, SUBLANES):
            shift_ref[r - 1] = pltpu.roll(zp, padded - r, axis=0)
        pieces, acc = [], {}
        n_out, span = tile // SUBLANES, (base + CONV_WIDTH - 1) // SUBLANES
        for v in range(padded // SUBLANES):
            for r in range(SUBLANES):
                taps = [(a, a * SUBLANES + r - base) for a in range(span + 1)
                        if 0 <= a * SUBLANES + r - base < CONV_WIDTH and 0 <= v - a < n_out]
                if not taps:
                    continue
                rows = slice(v * SUBLANES, (v + 1) * SUBLANES)
                src = zpad_ref[rows, cols] if r == 0 else shift_ref[r - 1, rows, :]
                if dep is not None and r % CONV_LINK_EVERY == 0:
                    src = jnp.where(dep != dep, dep, src)
                for a, k in taps:
                    term = dw_ref[k:k + 1, cols] * src
                    acc[v - a] = term if v - a not in acc else acc[v - a] + term
                dep = acc[v - taps[-1][0]]
            if v - span in acc:
                pieces.append(acc.pop(v - span) + dwb_ref[:, cols])
        assert not acc and len(pieces) == n_out
        convs.append(jnp.concatenate(pieces, axis=0))
    conv = jnp.concatenate(convs, axis=1)
    zpad_ref[0:CONV_HALO, :] = zpad_ref[tile:tile + CONV_HALO, :]
    mu = jnp.mean(conv, axis=-1, keepdims=True)
    cen = conv - mu
    var = jnp.mean(cen * cen, axis=-1, keepdims=True)
    zn = cen * jax.lax.rsqrt(var + EPS) * lng_ref[...] + lnb_ref[...]
    zs_ref[...] = (zn * _sigmoid(zn)).astype(_bf16)


def _token_major_stage(yt_ref, ys_ref):
    n_chunks = yt_ref.shape[1]
    for j in range(D_SSM // LANES):
        for q in range(SSM_CHUNK // SUBLANES):
            blks = [yt_ref[GROUPS_PER_VREG * j + a, :, q * LANES:(q + 1) * LANES]
                    for a in range(GROUPS_PER_VREG)]
            for r, rows in enumerate(_block_transpose8(blks)):
                ys_ref[j, pl.ds(SUBLANES * q + r, n_chunks, stride=CHUNK_PITCH), :] = rows


def _back_kernel(h1_ref, z0_ref, zn_ref, yt0_ref, ytn_ref, gate_ref, zmeta_ref, dw_ref, dwb_ref,
                 lng_ref, lnb_ref, wcp_ref, wv_ref, wg_ref, wout_ref, g3_ref, w1_ref, w3_ref, w2_ref,
                 gf_ref, out_ref, zpad_ref, shift_ref, zs_ref, ys_ref, *, tiles_per_seq):
    n = pl.program_id(0)
    conv_args = (zmeta_ref, dw_ref, dwb_ref, lng_ref, lnb_ref, zpad_ref, shift_ref, zs_ref)

    @pl.when(n == 0)
    def _():
        _conv_stage(z0_ref, True, *conv_args)
        _token_major_stage(yt0_ref, ys_ref)

    slab = h1_ref.shape[0] // MIX_SLABS
    h2_parts, nrm_parts = [], []
    for s in range(MIX_SLABS):
        rows = slice(s * slab, (s + 1) * slab)
        zs = zs_ref[rows, :]
        ys = jnp.concatenate(
            [jnp.concatenate([ys_ref[j, c * CHUNK_PITCH:c * CHUNK_PITCH + SSM_CHUNK, :]
                              for c in range(s * slab // SSM_CHUNK, (s + 1) * slab // SSM_CHUNK)], axis=0)
             for j in range(D_SSM // LANES)], axis=1).astype(_bf16)
        y_conv = _dot(zs, wcp_ref[...])
        y_ssm = _dot(ys, wv_ref[...]) * _sigmoid(_dot(ys, wg_ref[...]))
        mix = (gate_ref[rows, 0:D_MODEL].astype(_f32) * y_conv
               + gate_ref[rows, D_MODEL:].astype(_f32) * y_ssm).astype(_bf16)
        h2_parts.append(h1_ref[rows, :] + _dot(mix, wout_ref[...]))
        nrm_parts.append(_rms_norm(h2_parts[-1], g3_ref[...]).astype(_bf16))
    h2 = jnp.concatenate(h2_parts, axis=0)
    nrm = jnp.concatenate(nrm_parts, axis=0)

    _conv_stage(zn_ref, (n + 1) % tiles_per_seq == 0, *conv_args)
    _token_major_stage(ytn_ref, ys_ref)

    h3 = h2 + 0.5 * _swiglu(nrm, w1_ref, w3_ref, w2_ref)
    out_ref[...] = _rms_norm(h3, gf_ref[...])


def _back(h1, z, y_t, gate, zmeta, dw, dwb, lng, lnb, wcp, wv, wg, wout, g3, w1, w3, w2, gf,
          batch, seq, tile):
    tiles = batch * seq // tile
    nxt = lambda n: jnp.minimum(n + 1, tiles - 1)
    row = lambda w: pl.BlockSpec((tile, w), lambda n: (n, 0))
    yt_block = (N_SSM_GROUPS, tile // SSM_CHUNK, CHUNK_LANES)
    return pl.pallas_call(
        functools.partial(_back_kernel, tiles_per_seq=seq // tile),
        grid=(tiles,),
        in_specs=[row(D_MODEL),
                  pl.BlockSpec((tile, D_CONV), lambda n: (0, 0)),
                  pl.BlockSpec((tile, D_CONV), lambda n: (nxt(n), 0)),
                  pl.BlockSpec(yt_block, lambda n: (0, 0, 0)),
                  pl.BlockSpec(yt_block, lambda n: (0, nxt(n), 0)),
                  row(2 * D_MODEL),
                  _resident((CONV_HALO, D_CONV)), _resident((CONV_WIDTH, D_CONV)),
                  _resident((1, D_CONV)), _resident((1, D_CONV)), _resident((1, D_CONV)),
                  _resident((D_CONV, D_MODEL)), _resident((D_SSM, D_MODEL)),
                  _resident((D_SSM, D_MODEL)), _resident((D_MODEL, D_MODEL)),
                  _resident((1, D_MODEL)), _resident((D_MODEL, D_FF)), _resident((D_MODEL, D_FF)),
                  _resident((D_FF, D_MODEL)), _resident((1, D_MODEL))],
        out_specs=row(D_MODEL),
        out_shape=jax.ShapeDtypeStruct((batch * seq, D_MODEL), _f32),
        scratch_shapes=[pltpu.VMEM((CONV_HALO + tile, D_CONV), _f32),
                        pltpu.VMEM((SUBLANES - 1, CONV_HALO + tile, LANES), _f32),
                        pltpu.VMEM((tile, D_CONV), _bf16),
                        pltpu.VMEM((D_SSM // LANES, tile // SSM_CHUNK * CHUNK_PITCH, LANES), _f32)],
        compiler_params=pltpu.CompilerParams(dimension_semantics=("arbitrary",),
                                             vmem_limit_bytes=VMEM_LIMIT),
        name="back",
    )(h1, z, z, y_t, y_t, gate, zmeta, dw, dwb, lng, lnb, wcp, wv, wg, wout, g3, w1, w3, w2, gf)


def kernel(x, meta_tokens, ffn1_norm, ffn1_w1, ffn1_w3, ffn1_w2, mix_norm, w_in, b_gate, conv_dw, conv_dw_b, conv_ln_g, conv_ln_b, conv_proj, ssm_lam_re, ssm_lam_im, ssm_log_dt, ssm_b_re, ssm_b_im, ssm_c_re, ssm_c_im, ssm_d, ssm_w_v, ssm_w_g, w_out, ffn2_norm, ffn2_w1, ffn2_w3, ffn2_w2, final_norm):
    batch, seq, _ = x.shape
    assert ffn1_norm.shape[0] == 1 and seq % ROW_TILE == 0 and ROW_TILE % (SUBLANES * SSM_CHUNK) == 0
    assert N_META <= SSM_CHUNK and N_META <= CONV_HALO
    T, H, G = SSM_CHUNK, SSM_GROUP, N_SSM_GROUPS
    row = lambda v: v.reshape(1, -1)
    dup = lambda a: jnp.concatenate([a, a], axis=-1)

    meta_chunk = jnp.zeros((T, D_MODEL), _f32).at[T - N_META:].set(meta_tokens)
    h1, z, u_t, gate, z_m, u_m = _front(
        x.reshape(batch * seq, D_MODEL), meta_chunk, row(ffn1_norm[0]), row(mix_norm[0]),
        row(b_gate[0]), ffn1_w1[0], ffn1_w3[0], ffn1_w2[0], w_in[0], ROW_TILE)
    um_t = u_m[::-1].reshape(T, G, H).transpose(1, 0, 2).reshape(G, 1, T * H)
    um_t = jnp.concatenate([um_t, jnp.zeros((G, 2 * SUBLANES - 1, T * H), _f32)], axis=1)

    lam2 = jnp.stack([dup(ssm_lam_re[0]), dup(ssm_lam_im[0])], axis=1)
    bt2 = jnp.stack([dup(ssm_b_re[0].transpose(0, 2, 1)), dup(ssm_b_im[0].transpose(0, 2, 1))], axis=1)
    c2 = jnp.stack([dup(ssm_c_re[0]), dup(ssm_c_im[0])], axis=1)
    back_w = (conv_proj[0], ssm_w_v[0], ssm_w_g[0], w_out[0], ffn2_w1[0], ffn2_w3[0], ffn2_w2[0])
    y_t, (wcp, wv, wg, wo, w1b, w3b, w2b) = _ssm(
        u_t, um_t, lam2, ssm_log_dt[0].reshape(G, 1, 1), bt2, c2, ssm_d[0].reshape(G, H, 1),
        seq // T, back_w)

    z_halo = z_m[T - CONV_HALO:].astype(_f32)
    out = _back(h1, z, y_t, gate, z_halo, conv_dw[0], row(conv_dw_b[0]), row(conv_ln_g[0]),
                row(conv_ln_b[0]), wcp, wv, wg, wo, row(ffn2_norm[0]), w1b, w3b, w2b,
                row(final_norm), batch, seq, ROW_TILE)
    return out.reshape(batch, seq, D_MODEL)
```

```python
import functools

import jax
import jax.numpy as jnp
from jax.experimental import pallas as pl
from jax.experimental.pallas import tpu as pltpu

D_MODEL = 1024
N_META = 16
D_FF = 2816
D_CONV = 512
CONV_WIDTH = 31
D_SSM = 512
SSM_GROUP = 16
N_SSM_GROUPS = D_SSM // SSM_GROUP
SSM_STATE = 64
EPS = 1e-6

LANES = 128
SUBLANES = 8
SSM_CHUNK = 32
CHUNK_LANES = SSM_CHUNK * SSM_GROUP
STATE_LANES = 2 * SSM_STATE
GROUPS_PER_VREG = LANES // SSM_GROUP
CONV_HALO = 32
CONV_LINK_EVERY = 2
MIX_SLABS = 2
WEIGHT_LOAD_CHUNKS = 8
SSM_GROUPS_PER_STEP = 4
CHUNK_PITCH = SSM_CHUNK + SUBLANES
FF_SPLITS = ((0, 1024), (1024, 2048), (2048, 2816))
ROW_TILE = 512
VMEM_LIMIT = 60 * 1024 * 1024

assert STATE_LANES == LANES and GROUPS_PER_VREG == SUBLANES

_bf16 = jnp.bfloat16
_f32 = jnp.float32


def _dot(a, b):
    return jnp.dot(a, b, preferred_element_type=_f32)


def _dot_nt(a, b, precision=None):
    return jax.lax.dot_general(a, b, (((1,), (1,)), ((), ())), precision=precision,
                               preferred_element_type=_f32)


def _sigmoid(x):
    return 0.5 * jnp.tanh(0.5 * x) + 0.5


def _rms_norm(x, g):
    return x * jax.lax.rsqrt(jnp.mean(x * x, axis=-1, keepdims=True) + EPS) * g


def _swiglu(n, w1_ref, w3_ref, w2_ref, sigmoid=_sigmoid):
    acc = None
    for lo, hi in FF_SPLITS:
        a = _dot(n, w1_ref[:, lo:hi])
        b = _dot(n, w3_ref[:, lo:hi])
        f = (a * sigmoid(a) * b).astype(_bf16)
        part = _dot(f, w2_ref[lo:hi, :])
        acc = part if acc is None else acc + part
    return acc


def _block_transpose8(v):
    blk = jax.lax.broadcasted_iota(jnp.int32, v[0].shape, 1) // SSM_GROUP
    for d in (4, 2, 1):
        hi = (blk & d) != 0
        shift = d * SSM_GROUP
        new = list(v)
        for i in range(GROUPS_PER_VREG):
            if i & d:
                continue
            new[i] = jnp.where(hi, pltpu.roll(v[i + d], shift, axis=1), v[i])
            new[i + d] = jnp.where(hi, v[i + d], pltpu.roll(v[i], LANES - shift, axis=1))
        v = new
    return v


def _load_narrowed(jobs, sem):
    def copy(i, c):
        w_hbm, _, stage_ref = jobs[i]
        chunk = stage_ref.shape[1]
        return pltpu.make_async_copy(w_hbm.at[pl.ds(c * chunk, chunk), :], stage_ref.at[c % 2],
                                     sem.at[i, c % 2])

    for w_hbm, _, stage_ref in jobs:
        assert w_hbm.shape[0] == WEIGHT_LOAD_CHUNKS * stage_ref.shape[1]
    for i in range(len(jobs)):
        copy(i, 0).start()
    for c in range(WEIGHT_LOAD_CHUNKS):
        for i, (_, w_vmem, stage_ref) in enumerate(jobs):
            if c + 1 < WEIGHT_LOAD_CHUNKS:
                copy(i, c + 1).start()
            copy(i, c).wait()
            chunk = stage_ref.shape[1]
            w_vmem[c * chunk:(c + 1) * chunk, :] = stage_ref[c % 2].astype(_bf16)


def _front_rows(x, g1_ref, g2_ref, w1_ref, w3_ref, w2_ref, win_ref):
    n = _rms_norm(x, g1_ref[...]).astype(_bf16)
    h1 = x + 0.5 * _swiglu(n, w1_ref, w3_ref, w2_ref)
    u = _rms_norm(h1, g2_ref[...]).astype(_bf16)
    v = _dot(u, win_ref[:, 0:D_CONV])
    g = _dot(u, win_ref[:, D_CONV:2 * D_CONV])
    z = (v * _sigmoid(g)).astype(_bf16)
    us = _dot(u, win_ref[:, 2 * D_CONV:2 * D_CONV + D_SSM])
    return h1, z, us, u


def _front_kernel(x_ref, meta_ref, g1_ref, g2_ref, bg_ref, w1_hbm, w3_hbm, w2_hbm, win_hbm,
                  h1_ref, z_ref, u_ref, gate_ref, zm_ref, um_ref,
                  us_ref, w1_ref, w3_ref, w2_ref, win_ref, stage1_ref, stage3_ref, stage2_ref,
                  stagein_ref, sem):
    weights = (g1_ref, g2_ref, w1_ref, w3_ref, w2_ref, win_ref)

    @pl.when(pl.program_id(0) == 0)
    def _():
        _load_narrowed([(w1_hbm, w1_ref, stage1_ref), (w3_hbm, w3_ref, stage3_ref),
                        (w2_hbm, w2_ref, stage2_ref), (win_hbm, win_ref, stagein_ref)], sem)
        _, zm, usm, _ = _front_rows(meta_ref[...], *weights)
        zm_ref[...] = zm
        um_ref[...] = usm

    h1, z, us, u = _front_rows(x_ref[...], *weights)
    h1_ref[...] = h1
    z_ref[...] = z
    n_chunks = us.shape[0] // SSM_CHUNK
    for j in range(D_SSM // LANES):
        for c in range(n_chunks):
            us_ref[j, c * CHUNK_PITCH:c * CHUNK_PITCH + SSM_CHUNK, :] = (
                us[c * SSM_CHUNK:(c + 1) * SSM_CHUNK, j * LANES:(j + 1) * LANES])
        for q in range(SSM_CHUNK // SUBLANES):
            rows = [us_ref[j, pl.ds(SSM_CHUNK - 1 - (SUBLANES * q + r), n_chunks, stride=CHUNK_PITCH), :]
                    for r in range(SUBLANES)]
            for a, blk in enumerate(_block_transpose8(rows)):
                u_ref[GROUPS_PER_VREG * j + a, :, q * LANES:(q + 1) * LANES] = blk
    gi = _dot(u, win_ref[:, 2 * D_CONV + D_SSM:]) + bg_ref[...]
    gate_ref[...] = _sigmoid(gi).astype(_bf16)


def _resident(shape):
    return pl.BlockSpec(shape, lambda *_: (0,) * len(shape), pipeline_mode=pl.Buffered(1))


def _front(x2d, meta_chunk, g1, g2, bg, w1, w3, w2, win, tile):
    rows = x2d.shape[0]
    d_in = win.shape[1]
    n_gate = d_in - 2 * D_CONV - D_SSM
    m_rows = meta_chunk.shape[0]
    row = lambda w: pl.BlockSpec((tile, w), lambda i: (i, 0))
    once = lambda w: pl.BlockSpec((m_rows, w), lambda i: (0, 0))
    hbm = pl.BlockSpec(memory_space=pl.ANY)
    stage_rows = D_MODEL // WEIGHT_LOAD_CHUNKS
    return pl.pallas_call(
        _front_kernel,
        grid=(rows // tile,),
        in_specs=[row(D_MODEL), _resident((m_rows, D_MODEL)), _resident((1, D_MODEL)),
                  _resident((1, D_MODEL)), _resident((1, n_gate)), hbm, hbm, hbm, hbm],
        out_specs=[row(D_MODEL), row(D_CONV),
                   pl.BlockSpec((N_SSM_GROUPS, tile // SSM_CHUNK, CHUNK_LANES), lambda i: (0, i, 0)),
                   row(n_gate), once(D_CONV), once(D_SSM)],
        out_shape=[jax.ShapeDtypeStruct((rows, D_MODEL), _f32),
                   jax.ShapeDtypeStruct((rows, D_CONV), _bf16),
                   jax.ShapeDtypeStruct((N_SSM_GROUPS, rows // SSM_CHUNK, CHUNK_LANES), _f32),
                   jax.ShapeDtypeStruct((rows, n_gate), _bf16),
                   jax.ShapeDtypeStruct((m_rows, D_CONV), _bf16),
                   jax.ShapeDtypeStruct((m_rows, D_SSM), _f32)],
        scratch_shapes=[pltpu.VMEM((D_SSM // LANES, tile // SSM_CHUNK * CHUNK_PITCH, LANES), _f32),
                        pltpu.VMEM((D_MODEL, D_FF), _bf16), pltpu.VMEM((D_MODEL, D_FF), _bf16),
                        pltpu.VMEM((D_FF, D_MODEL), _bf16), pltpu.VMEM((D_MODEL, d_in), _bf16),
                        pltpu.VMEM((2, stage_rows, D_FF), _f32), pltpu.VMEM((2, stage_rows, D_FF), _f32),
                        pltpu.VMEM((2, D_FF // WEIGHT_LOAD_CHUNKS, D_MODEL), _f32),
                        pltpu.VMEM((2, stage_rows, d_in), _f32),
                        pltpu.SemaphoreType.DMA((4, 2))],
        compiler_params=pltpu.CompilerParams(dimension_semantics=("arbitrary",),
                                             vmem_limit_bytes=VMEM_LIMIT),
        name="front",
    )(x2d, meta_chunk, g1, g2, bg, w1, w3, w2, win)


def _ssm_kernel(*refs, chunks_per_seq, levels, groups_per_step, n_cast):
    n_in = len(refs) - 2 * n_cast - 2
    ins, w_f32 = refs[:n_in], refs[n_in:n_in + n_cast]
    y_ref, w_bf16, toep_ref = refs[n_in + n_cast], refs[n_in + n_cast + 1:-1], refs[-1]
    for src, dst in zip(w_f32, w_bf16):
        dst[...] = src[...].astype(_bf16)
    assert groups_per_step == y_ref.shape[0]
    _ssm_groups(*ins, y_ref, toep_ref, chunks_per_seq=chunks_per_seq, levels=levels)


def _per_group(fn, *arrays):
    return jnp.stack([fn(*(a[g] for a in arrays)) for g in range(arrays[0].shape[0])])


def _ssm_groups(u_ref, um_ref, lam_ref, ldt_ref, bt_ref, c_ref, d_ref, y_ref, toep_ref,
                *, chunks_per_seq, levels):
    T, H, P = SSM_CHUNK, SSM_GROUP, SSM_STATE
    lane = jax.lax.broadcasted_iota(jnp.int32, (1, STATE_LANES), 1)
    lo = lane < P
    sgn = jnp.where(lo, -1.0, 1.0)
    lam_re = lam_ref[:, 0:1, :]
    lam_im = lam_ref[:, 1:2, :]
    dt = jnp.exp(ldt_ref[...])
    ar, ai = lam_re * dt, lam_im * dt

    def powers(k):
        mag, ang = jnp.exp(ar * k[None]), ai * k[None]
        cs, sn = jnp.cos(ang), jnp.sin(ang)
        return mag * jnp.where(lo, cs, sn), mag * jnp.where(lo, sn, cs)

    def cmul3(m1, m2, pw, pws):
        prod = m1[:, None] * pw[:, :, None, :] + m2[:, None] * pws[:, :, None, :]
        return prod.reshape(prod.shape[0], T * H, STATE_LANES)

    def cmul_rows(v, vs, m1, m2):
        return v * m1 + vs * m2, vs * m1 - v * m2

    def roll(x, shift, axis):
        return _per_group(lambda a: pltpu.roll(a, shift, axis=axis), x)

    n_dbl = (T // SUBLANES).bit_length() - 1
    assert SUBLANES << n_dbl == T and n_dbl < SUBLANES
    row8 = jax.lax.broadcasted_iota(jnp.int32, (SUBLANES, 1), 0)
    aux, aux_s = powers(jnp.where(row8 < n_dbl, jnp.left_shift(SUBLANES, row8),
                                  jnp.where(row8 == n_dbl, 1, 0)).astype(_f32))
    x1 = jnp.where(lo, aux, aux_s)
    x2 = sgn * jnp.where(lo, aux_s, aux)
    pw, pws = powers(row8.astype(_f32))
    for i in range(n_dbl):
        nxt, nxt_s = cmul_rows(pw, pws, x1[:, i:i + 1], x2[:, i:i + 1])
        pw, pws = jnp.concatenate([pw, nxt], axis=1), jnp.concatenate([pws, nxt_s], axis=1)
    pw1, pw1s = cmul_rows(pw, pws, x1[:, n_dbl:n_dbl + 1], x2[:, n_dbl:n_dbl + 1])

    lb_re, lb_im = x1[:, n_dbl:n_dbl + 1], sgn * x2[:, n_dbl:n_dbl + 1]
    den = lam_re * lam_re + lam_im * lam_im
    co_re = ((lb_re - 1.0) * lam_re + lb_im * lam_im) / den
    co_im = (lb_im * lam_re - (lb_re - 1.0) * lam_im) / den
    b_re, b_im = bt_ref[:, 0], bt_ref[:, 1]
    bb_re = co_re * b_re - co_im * b_im
    bb_im = co_re * b_im + co_im * b_re
    c1, c2 = c_ref[:, 0], sgn * c_ref[:, 1]

    ws = cmul3(bb_re, sgn * bb_im, pw, pws).astype(_bf16)
    wct = (cmul3(c1, c2, pw1, pw1s) * (-sgn)).astype(_bf16)
    e = cmul3(c1, c2, pw, pws)
    r0 = _per_group(functools.partial(_dot_nt, precision=jax.lax.Precision.HIGHEST),
                    jnp.where(lo, bb_re, -bb_im), e)
    lane_c = jax.lax.broadcasted_iota(jnp.int32, (H, CHUNK_LANES), 1)
    row_c = jax.lax.broadcasted_iota(jnp.int32, (H, CHUNK_LANES), 0)
    r0 = r0 + jnp.where(lane_c == row_c, d_ref[...], 0.0)
    for r in range(SUBLANES):
        rr = roll(r0, r * H, 1) if r else r0
        for a in range(T // SUBLANES):
            s = SUBLANES * a + r
            blk = roll(rr, a * LANES, 1) if a else rr
            toep_ref[:, (T - 1 - s) * H:(T - s) * H, :] = jnp.where(lane_c >= s * H, blk, 0.0).astype(_bf16)

    u = u_ref[...].astype(_bf16)
    rows = u.shape[1]
    s_loc = _per_group(_dot, u, ws)
    sm = _per_group(_dot, um_ref[...].astype(_bf16), ws)[:, 0:1, :]

    lvl_col = jax.lax.broadcasted_iota(jnp.int32, (SUBLANES, 1), 0)
    step, step_s = powers(jnp.left_shift(T, lvl_col).astype(_f32))
    a1 = jnp.where(lo, step, step_s)
    a2 = sgn * jnp.where(lo, step_s, step)

    def cmul(v, lvl):
        return a1[:, lvl:lvl + 1, :] * v + a2[:, lvl:lvl + 1, :] * roll(v, P, 1)

    cidx = jax.lax.broadcasted_iota(jnp.int32, (rows, STATE_LANES), 0) % chunks_per_seq
    first = cidx == 0
    st = s_loc + jnp.where(first, cmul(jnp.broadcast_to(sm, s_loc.shape), 0), 0.0)
    for lvl in range(levels):
        shift = 1 << lvl
        prev = jnp.where(cidx >= shift, roll(st, shift, 0), 0.0)
        st = st + cmul(prev, lvl)
    st_in = jnp.where(first, sm, roll(st, 1, 0))
    y = _per_group(_dot, u, toep_ref[...]) + _per_group(_dot_nt, st_in.astype(_bf16), wct)
    y_ref[...] = jax.nn.gelu(y)


def _ssm(u_t, um_t, lam2, ldt, bt2, c2, d_col, chunks_per_seq, weights):
    groups, rows, _ = u_t.shape
    levels = max(1, (chunks_per_seq - 1).bit_length())
    assert levels <= SUBLANES
    gps = SSM_GROUPS_PER_STEP
    steps = groups // gps
    per_group = lambda *s: pl.BlockSpec((gps,) + s, lambda g: (g,) + (0,) * len(s))
    row_block = lambda w: pl.BlockSpec((w.shape[0] // steps, w.shape[1]), lambda g: (g, 0))
    assert all(w.shape[0] % (steps * 2 * SUBLANES) == 0 for w in weights)
    outs = pl.pallas_call(
        functools.partial(_ssm_kernel, chunks_per_seq=chunks_per_seq, levels=levels,
                          groups_per_step=gps, n_cast=len(weights)),
        grid=(steps,),
        in_specs=[per_group(rows, CHUNK_LANES), per_group(um_t.shape[1], CHUNK_LANES),
                  per_group(2, STATE_LANES), per_group(1, 1),
                  per_group(2, SSM_GROUP, STATE_LANES), per_group(2, SSM_GROUP, STATE_LANES),
                  per_group(SSM_GROUP, 1)] + [row_block(w) for w in weights],
        out_specs=[per_group(rows, CHUNK_LANES)] + [row_block(w) for w in weights],
        out_shape=[jax.ShapeDtypeStruct((groups, rows, CHUNK_LANES), _f32)]
                  + [jax.ShapeDtypeStruct(w.shape, _bf16) for w in weights],
        scratch_shapes=[pltpu.VMEM((gps, CHUNK_LANES, CHUNK_LANES), _bf16)],
        compiler_params=pltpu.CompilerParams(dimension_semantics=("arbitrary",),
                                             vmem_limit_bytes=VMEM_LIMIT),
        name="ssm",
    )(u_t, um_t, lam2, ldt, bt2, c2, d_col, *weights)
    return outs[0], outs[1:]


def _conv_stage(z_ref, first_of_seq, zmeta_ref, dw_ref, dwb_ref, lng_ref, lnb_ref,
                zpad_ref, shift_ref, zs_ref):
    tile = z_ref.shape[0]
    padded = CONV_HALO + tile
    if first_of_seq is True:
        zpad_ref[0:CONV_HALO, :] = zmeta_ref[...]
    else:
        zpad_ref[0:CONV_HALO, :] = jnp.where(first_of_seq, zmeta_ref[...], zpad_ref[0:CONV_HALO, :])
    zpad_ref[CONV_HALO:, :] = z_ref[...].astype(_f32)
    base = CONV_HALO - (CONV_WIDTH - 1)
    convs = []
    dep = None
    for j in range(D_CONV // LANES):
        cols = slice(j * LANES, (j + 1) * LANES)
        zp = zpad_ref[:, cols]
        for r in range(1, SUBLANES):
            shift_ref[r - 1] = pltpu.roll(zp, padded - r, axis=0)
        pieces, acc = [], {}
        n_out, span = tile // SUBLANES, (base + CONV_WIDTH - 1) // SUBLANES
        for v in range(padded // SUBLANES):
            for r in range(SUBLANES):
                taps = [(a, a * SUBLANES + r - base) for a in range(span + 1)
                        if 0 <= a * SUBLANES + r - base < CONV_WIDTH and 0 <= v - a < n_out]
                if not taps:
                    continue
                rows = slice(v * SUBLANES, (v + 1) * SUBLANES)
                src = zpad_ref[rows, cols] if r == 0 else shift_ref[r - 1, rows, :]
                if dep is not None and r % CONV_LINK_EVERY == 0:
                    src = jnp.where(dep != dep, dep, src)
                for a, k in taps:
                    term = dw_ref[k:k + 1, cols] * src
                    acc[v - a] = term if v - a not in acc else acc[v - a] + term
                dep = acc[v - taps[-1][0]]
            if v - span in acc:
                pieces.append(acc.pop(v - span) + dwb_ref[:, cols])
        assert not acc and len(pieces) == n_out
        convs.append(jnp.concatenate(pieces, axis=0))
    conv = jnp.concatenate(convs, axis=1)
    zpad_ref[0:CONV_HALO, :] = zpad_ref[tile:tile + CONV_HALO, :]
    mu = jnp.mean(conv, axis=-1, keepdims=True)
    cen = conv - mu
    var = jnp.mean(cen * cen, axis=-1, keepdims=True)
    zn = cen * jax.lax.rsqrt(var + EPS) * lng_ref[...] + lnb_ref[...]
    zs_ref[...] = (zn * jax.nn.sigmoid(zn)).astype(_bf16)


def _token_major_stage(yt_ref, ys_ref):
    n_chunks = yt_ref.shape[1]
    for j in range(D_SSM // LANES):
        for q in range(SSM_CHUNK // SUBLANES):
            blks = [yt_ref[GROUPS_PER_VREG * j + a, :, q * LANES:(q + 1) * LANES]
                    for a in range(GROUPS_PER_VREG)]
            for r, rows in enumerate(_block_transpose8(blks)):
                ys_ref[j, pl.ds(SUBLANES * q + r, n_chunks, stride=CHUNK_PITCH), :] = rows


def _back_kernel(h1_ref, z0_ref, zn_ref, yt0_ref, ytn_ref, gate_ref, zmeta_ref, dw_ref, dwb_ref,
                 lng_ref, lnb_ref, wcp_ref, wv_ref, wg_ref, wout_ref, g3_ref, w1_ref, w3_ref, w2_ref,
                 gf_ref, out_ref, zpad_ref, shift_ref, zs_ref, ys_ref, *, tiles_per_seq):
    n = pl.program_id(0)
    conv_args = (zmeta_ref, dw_ref, dwb_ref, lng_ref, lnb_ref, zpad_ref, shift_ref, zs_ref)

    @pl.when(n == 0)
    def _():
        _conv_stage(z0_ref, True, *conv_args)
        _token_major_stage(yt0_ref, ys_ref)

    slab = h1_ref.shape[0] // MIX_SLABS
    h2_parts, nrm_parts = [], []
    for s in range(MIX_SLABS):
        rows = slice(s * slab, (s + 1) * slab)
        zs = zs_ref[rows, :]
        ys = jnp.concatenate(
            [jnp.concatenate([ys_ref[j, c * CHUNK_PITCH:c * CHUNK_PITCH + SSM_CHUNK, :]
                              for c in range(s * slab // SSM_CHUNK, (s + 1) * slab // SSM_CHUNK)], axis=0)
             for j in range(D_SSM // LANES)], axis=1).astype(_bf16)
        y_conv = _dot(zs, wcp_ref[...])
        y_ssm = _dot(ys, wv_ref[...]) * jax.nn.sigmoid(_dot(ys, wg_ref[...]))
        mix = (gate_ref[rows, 0:D_MODEL].astype(_f32) * y_conv
               + gate_ref[rows, D_MODEL:].astype(_f32) * y_ssm).astype(_bf16)
        h2_parts.append(h1_ref[rows, :] + _dot(mix, wout_ref[...]))
        nrm_parts.append(_rms_norm(h2_parts[-1], g3_ref[...]).astype(_bf16))
    h2 = jnp.concatenate(h2_parts, axis=0)
    nrm = jnp.concatenate(nrm_parts, axis=0)

    _conv_stage(zn_ref, (n + 1) % tiles_per_seq == 0, *conv_args)
    _token_major_stage(ytn_ref, ys_ref)

    h3 = h2 + 0.5 * _swiglu(nrm, w1_ref, w3_ref, w2_ref, sigmoid=jax.nn.sigmoid)
    out_ref[...] = _rms_norm(h3, gf_ref[...])


def _back(h1, z, y_t, gate, zmeta, dw, dwb, lng, lnb, wcp, wv, wg, wout, g3, w1, w3, w2, gf,
          batch, seq, tile):
    tiles = batch * seq // tile
    nxt = lambda n: jnp.minimum(n + 1, tiles - 1)
    row = lambda w: pl.BlockSpec((tile, w), lambda n: (n, 0))
    yt_block = (N_SSM_GROUPS, tile // SSM_CHUNK, CHUNK_LANES)
    return pl.pallas_call(
        functools.partial(_back_kernel, tiles_per_seq=seq // tile),
        grid=(tiles,),
        in_specs=[row(D_MODEL),
                  pl.BlockSpec((tile, D_CONV), lambda n: (0, 0)),
                  pl.BlockSpec((tile, D_CONV), lambda n: (nxt(n), 0)),
                  pl.BlockSpec(yt_block, lambda n: (0, 0, 0)),
                  pl.BlockSpec(yt_block, lambda n: (0, nxt(n), 0)),
                  row(2 * D_MODEL),
                  _resident((CONV_HALO, D_CONV)), _resident((CONV_WIDTH, D_CONV)),
                  _resident((1, D_CONV)), _resident((1, D_CONV)), _resident((1, D_CONV)),
                  _resident((D_CONV, D_MODEL)), _resident((D_SSM, D_MODEL)),
                  _resident((D_SSM, D_MODEL)), _resident((D_MODEL, D_MODEL)),
                  _resident((1, D_MODEL)), _resident((D_MODEL, D_FF)), _resident((D_MODEL, D_FF)),
                  _resident((D_FF, D_MODEL)), _resident((1, D_MODEL))],
        out_specs=row(D_MODEL),
        out_shape=jax.ShapeDtypeStruct((batch * seq, D_MODEL), _f32),
        scratch_shapes=[pltpu.VMEM((CONV_HALO + tile, D_CONV), _f32),
                        pltpu.VMEM((SUBLANES - 1, CONV_HALO + tile, LANES), _f32),
                        pltpu.VMEM((tile, D_CONV), _bf16),
                        pltpu.VMEM((D_SSM // LANES, tile // SSM_CHUNK * CHUNK_PITCH, LANES), _f32)],
        compiler_params=pltpu.CompilerParams(dimension_semantics=("arbitrary",),
                                             vmem_limit_bytes=VMEM_LIMIT),
        name="back",
    )(h1, z, z, y_t, y_t, gate, zmeta, dw, dwb, lng, lnb, wcp, wv, wg, wout, g3, w1, w3, w2, gf)


def kernel(x, meta_tokens, ffn1_norm, ffn1_w1, ffn1_w3, ffn1_w2, mix_norm, w_in, b_gate, conv_dw, conv_dw_b, conv_ln_g, conv_ln_b, conv_proj, ssm_lam_re, ssm_lam_im, ssm_log_dt, ssm_b_re, ssm_b_im, ssm_c_re, ssm_c_im, ssm_d, ssm_w_v, ssm_w_g, w_out, ffn2_norm, ffn2_w1, ffn2_w3, ffn2_w2, final_norm):
    batch, seq, _ = x.shape
    assert ffn1_norm.shape[0] == 1 and seq % ROW_TILE == 0 and ROW_TILE % (SUBLANES * SSM_CHUNK) == 0
    assert N_META <= SSM_CHUNK and N_META <= CONV_HALO
    T, H, G = SSM_CHUNK, SSM_GROUP, N_SSM_GROUPS
    row = lambda v: v.reshape(1, -1)
    dup = lambda a: jnp.concatenate([a, a], axis=-1)

    meta_chunk = jnp.zeros((T, D_MODEL), _f32).at[T - N_META:].set(meta_tokens)
    h1, z, u_t, gate, z_m, u_m = _front(
        x.reshape(batch * seq, D_MODEL), meta_chunk, row(ffn1_norm[0]), row(mix_norm[0]),
        row(b_gate[0]), ffn1_w1[0], ffn1_w3[0], ffn1_w2[0], w_in[0], ROW_TILE)
    um_t = u_m[::-1].reshape(T, G, H).transpose(1, 0, 2).reshape(G, 1, T * H)
    um_t = jnp.concatenate([um_t, jnp.zeros((G, 2 * SUBLANES - 1, T * H), _f32)], axis=1)

    lam2 = jnp.stack([dup(ssm_lam_re[0]), dup(ssm_lam_im[0])], axis=1)
    bt2 = jnp.stack([dup(ssm_b_re[0].transpose(0, 2, 1)), dup(ssm_b_im[0].transpose(0, 2, 1))], axis=1)
    c2 = jnp.stack([dup(ssm_c_re[0]), dup(ssm_c_im[0])], axis=1)
    back_w = (conv_proj[0], ssm_w_v[0], ssm_w_g[0], w_out[0], ffn2_w1[0], ffn2_w3[0], ffn2_w2[0])
    y_t, (wcp, wv, wg, wo, w1b, w3b, w2b) = _ssm(
        u_t, um_t, lam2, ssm_log_dt[0].reshape(G, 1, 1), bt2, c2, ssm_d[0].reshape(G, H, 1),
        seq // T, back_w)

    z_halo = z_m[T - CONV_HALO:].astype(_f32)
    out = _back(h1, z, y_t, gate, z_halo, conv_dw[0], row(conv_dw_b[0]), row(conv_ln_g[0]),
                row(conv_ln_b[0]), wcp, wv, wg, wo, row(ffn2_norm[0]), w1b, w3b, w2b,
                row(final_norm), batch, seq, ROW_TILE)
    return out.reshape(batch, seq, D_MODEL)
```

```python
import functools

import jax
import jax.numpy as jnp
from jax.experimental import pallas as pl
from jax.experimental.pallas import tpu as pltpu

D_MODEL = 1024
N_META = 16
D_FF = 2816
D_CONV = 512
CONV_WIDTH = 31
D_SSM = 512
SSM_GROUP = 16
N_SSM_GROUPS = D_SSM // SSM_GROUP
SSM_STATE = 64
EPS = 1e-6

LANES = 128
SUBLANES = 8
SSM_CHUNK = 32
CHUNK_LANES = SSM_CHUNK * SSM_GROUP
STATE_LANES = 2 * SSM_STATE
GROUPS_PER_VREG = LANES // SSM_GROUP
CONV_HALO = 32
CONV_LINK_EVERY = 2
MIX_SLABS = 2
WEIGHT_LOAD_CHUNKS = 8
SSM_GROUPS_PER_STEP = 4
CHUNK_PITCH = SSM_CHUNK + SUBLANES
FF_SPLITS = ((0, 1024), (1024, 2048), (2048, 2816))
ROW_TILE = 512
VMEM_LIMIT = 60 * 1024 * 1024

assert STATE_LANES == LANES and GROUPS_PER_VREG == SUBLANES

_bf16 = jnp.bfloat16
_f32 = jnp.float32


def _dot(a, b):
    return jnp.dot(a, b, preferred_element_type=_f32)


def _dot_nt(a, b, precision=None):
    return jax.lax.dot_general(a, b, (((1,), (1,)), ((), ())), precision=precision,
                               preferred_element_type=_f32)


def _sigmoid(x):
    return 0.5 * jnp.tanh(0.5 * x) + 0.5


def _rms_norm(x, g):
    return x * jax.lax.rsqrt(jnp.mean(x * x, axis=-1, keepdims=True) + EPS) * g


def _swiglu(n, w1_ref, w3_ref, w2_ref):
    acc = None
    for lo, hi in FF_SPLITS:
        a = _dot(n, w1_ref[:, lo:hi])
        b = _dot(n, w3_ref[:, lo:hi])
        f = (a * _sigmoid(a) * b).astype(_bf16)
        part = _dot(f, w2_ref[lo:hi, :])
        acc = part if acc is None else acc + part
    return acc


def _block_transpose8(v):
    blk = jax.lax.broadcasted_iota(jnp.int32, v[0].shape, 1) // SSM_GROUP
    for d in (4, 2, 1):
        hi = (blk & d) != 0
        shift = d * SSM_GROUP
        new = list(v)
        for i in range(GROUPS_PER_VREG):
            if i & d:
                continue
            new[i] = jnp.where(hi, pltpu.roll(v[i + d], shift, axis=1), v[i])
            new[i + d] = jnp.where(hi, v[i + d], pltpu.roll(v[i], LANES - shift, axis=1))
        v = new
    return v


def _load_narrowed(jobs, sem):
    def copy(i, c):
        w_hbm, _, stage_ref = jobs[i]
        chunk = stage_ref.shape[1]
        return pltpu.make_async_copy(w_hbm.at[pl.ds(c * chunk, chunk), :], stage_ref.at[c % 2],
                                     sem.at[i, c % 2])

    for w_hbm, _, stage_ref in jobs:
        assert w_hbm.shape[0] == WEIGHT_LOAD_CHUNKS * stage_ref.shape[1]
    for i in range(len(jobs)):
        copy(i, 0).start()
    for c in range(WEIGHT_LOAD_CHUNKS):
        for i, (_, w_vmem, stage_ref) in enumerate(jobs):
            if c + 1 < WEIGHT_LOAD_CHUNKS:
                copy(i, c + 1).start()
            copy(i, c).wait()
            chunk = stage_ref.shape[1]
            w_vmem[c * chunk:(c + 1) * chunk, :] = stage_ref[c % 2].astype(_bf16)


def _front_rows(x, g1_ref, g2_ref, w1_ref, w3_ref, w2_ref, win_ref):
    n = _rms_norm(x, g1_ref[...]).astype(_bf16)
    h1 = x + 0.5 * _swiglu(n, w1_ref, w3_ref, w2_ref)
    u = _rms_norm(h1, g2_ref[...]).astype(_bf16)
    v = _dot(u, win_ref[:, 0:D_CONV])
    g = _dot(u, win_ref[:, D_CONV:2 * D_CONV])
    z = (v * _sigmoid(g)).astype(_bf16)
    us = _dot(u, win_ref[:, 2 * D_CONV:2 * D_CONV + D_SSM])
    return h1, z, us, u


def _front_kernel(x_ref, meta_ref, g1_ref, g2_ref, bg_ref, w1_hbm, w3_hbm, w2_hbm, win_hbm,
                  h1_ref, z_ref, u_ref, gate_ref, zm_ref, um_ref,
                  us_ref, w1_ref, w3_ref, w2_ref, win_ref, stage1_ref, stage3_ref, stage2_ref,
                  stagein_ref, sem):
    weights = (g1_ref, g2_ref, w1_ref, w3_ref, w2_ref, win_ref)

    @pl.when(pl.program_id(0) == 0)
    def _():
        _load_narrowed([(w1_hbm, w1_ref, stage1_ref), (w3_hbm, w3_ref, stage3_ref),
                        (w2_hbm, w2_ref, stage2_ref), (win_hbm, win_ref, stagein_ref)], sem)
        _, zm, usm, _ = _front_rows(meta_ref[...], *weights)
        zm_ref[...] = zm
        um_ref[...] = usm

    h1, z, us, u = _front_rows(x_ref[...], *weights)
    h1_ref[...] = h1
    z_ref[...] = z
    n_chunks = us.shape[0] // SSM_CHUNK
    for j in range(D_SSM // LANES):
        for c in range(n_chunks):
            us_ref[j, c * CHUNK_PITCH:c * CHUNK_PITCH + SSM_CHUNK, :] = (
                us[c * SSM_CHUNK:(c + 1) * SSM_CHUNK, j * LANES:(j + 1) * LANES])
        for q in range(SSM_CHUNK // SUBLANES):
            rows = [us_ref[j, pl.ds(SSM_CHUNK - 1 - (SUBLANES * q + r), n_chunks, stride=CHUNK_PITCH), :]
                    for r in range(SUBLANES)]
            for a, blk in enumerate(_block_transpose8(rows)):
                u_ref[GROUPS_PER_VREG * j + a, :, q * LANES:(q + 1) * LANES] = blk
    gi = _dot(u, win_ref[:, 2 * D_CONV + D_SSM:]) + bg_ref[...]
    gate_ref[...] = _sigmoid(gi).astype(_bf16)


def _resident(shape):
    return pl.BlockSpec(shape, lambda *_: (0,) * len(shape), pipeline_mode=pl.Buffered(1))


def _front(x2d, meta_chunk, g1, g2, bg, w1, w3, w2, win, tile):
    rows = x2d.shape[0]
    d_in = win.shape[1]
    n_gate = d_in - 2 * D_CONV - D_SSM
    m_rows = meta_chunk.shape[0]
    row = lambda w: pl.BlockSpec((tile, w), lambda i: (i, 0))
    once = lambda w: pl.BlockSpec((m_rows, w), lambda i: (0, 0))
    hbm = pl.BlockSpec(memory_space=pl.ANY)
    stage_rows = D_MODEL // WEIGHT_LOAD_CHUNKS
    return pl.pallas_call(
        _front_kernel,
        grid=(rows // tile,),
        in_specs=[row(D_MODEL), _resident((m_rows, D_MODEL)), _resident((1, D_MODEL)),
                  _resident((1, D_MODEL)), _resident((1, n_gate)), hbm, hbm, hbm, hbm],
        out_specs=[row(D_MODEL), row(D_CONV),
                   pl.BlockSpec((N_SSM_GROUPS, tile // SSM_CHUNK, CHUNK_LANES), lambda i: (0, i, 0)),
                   row(n_gate), once(D_CONV), once(D_SSM)],
        out_shape=[jax.ShapeDtypeStruct((rows, D_MODEL), _f32),
                   jax.ShapeDtypeStruct((rows, D_CONV), _bf16),
                   jax.ShapeDtypeStruct((N_SSM_GROUPS, rows // SSM_CHUNK, CHUNK_LANES), _f32),
                   jax.ShapeDtypeStruct((rows, n_gate), _bf16),
                   jax.ShapeDtypeStruct((m_rows, D_CONV), _bf16),
                   jax.ShapeDtypeStruct((m_rows, D_SSM), _f32)],
        scratch_shapes=[pltpu.VMEM((D_SSM // LANES, tile // SSM_CHUNK * CHUNK_PITCH, LANES), _f32),
                        pltpu.VMEM((D_MODEL, D_FF), _bf16), pltpu.VMEM((D_MODEL, D_FF), _bf16),
                        pltpu.VMEM((D_FF, D_MODEL), _bf16), pltpu.VMEM((D_MODEL, d_in), _bf16),
                        pltpu.VMEM((2, stage_rows, D_FF), _f32), pltpu.VMEM((2, stage_rows, D_FF), _f32),
                        pltpu.VMEM((2, D_FF // WEIGHT_LOAD_CHUNKS, D_MODEL), _f32),
                        pltpu.VMEM((2, stage_rows, d_in), _f32),
                        pltpu.SemaphoreType.DMA((4, 2))],
        compiler_params=pltpu.CompilerParams(dimension_semantics=("arbitrary",),
                                             vmem_limit_bytes=VMEM_LIMIT),
        name="front",
    )(x2d, meta_chunk, g1, g2, bg, w1, w3, w2, win)


def _ssm_kernel(*refs, chunks_per_seq, levels, groups_per_step, n_cast):
    n_in = len(refs) - 2 * n_cast - 2
    ins, w_f32 = refs[:n_in], refs[n_in:n_in + n_cast]
    y_ref, w_bf16, toep_ref = refs[n_in + n_cast], refs[n_in + n_cast + 1:-1], refs[-1]
    for src, dst in zip(w_f32, w_bf16):
        dst[...] = src[...].astype(_bf16)
    assert groups_per_step == y_ref.shape[0]
    _ssm_groups(*ins, y_ref, toep_ref, chunks_per_seq=chunks_per_seq, levels=levels)


def _per_group(fn, *arrays):
    return jnp.stack([fn(*(a[g] for a in arrays)) for g in range(arrays[0].shape[0])])


def _ssm_groups(u_ref, um_ref, lam_ref, ldt_ref, bt_ref, c_ref, d_ref, y_ref, toep_ref,
                *, chunks_per_seq, levels):
    T, H, P = SSM_CHUNK, SSM_GROUP, SSM_STATE
    lane = jax.lax.broadcasted_iota(jnp.int32, (1, STATE_LANES), 1)
    lo = lane < P
    sgn = jnp.where(lo, -1.0, 1.0)
    lam_re = lam_ref[:, 0:1, :]
    lam_im = lam_ref[:, 1:2, :]
    dt = jnp.exp(ldt_ref[...])
    ar, ai = lam_re * dt, lam_im * dt

    def powers(k):
        mag, ang = jnp.exp(ar * k[None]), ai * k[None]
        cs, sn = jnp.cos(ang), jnp.sin(ang)
        return mag * jnp.where(lo, cs, sn), mag * jnp.where(lo, sn, cs)

    def cmul3(m1, m2, pw, pws):
        prod = m1[:, None] * pw[:, :, None, :] + m2[:, None] * pws[:, :, None, :]
        return prod.reshape(prod.shape[0], T * H, STATE_LANES)

    def cmul_rows(v, vs, m1, m2):
        return v * m1 + vs * m2, vs * m1 - v * m2

    def roll(x, shift, axis):
        return _per_group(lambda a: pltpu.roll(a, shift, axis=axis), x)

    n_dbl = (T // SUBLANES).bit_length() - 1
    assert SUBLANES << n_dbl == T and n_dbl < SUBLANES
    row8 = jax.lax.broadcasted_iota(jnp.int32, (SUBLANES, 1), 0)
    aux, aux_s = powers(jnp.where(row8 < n_dbl, jnp.left_shift(SUBLANES, row8),
                                  jnp.where(row8 == n_dbl, 1, 0)).astype(_f32))
    x1 = jnp.where(lo, aux, aux_s)
    x2 = sgn * jnp.where(lo, aux_s, aux)
    pw, pws = powers(row8.astype(_f32))
    for i in range(n_dbl):
        nxt, nxt_s = cmul_rows(pw, pws, x1[:, i:i + 1], x2[:, i:i + 1])
        pw, pws = jnp.concatenate([pw, nxt], axis=1), jnp.concatenate([pws, nxt_s], axis=1)
    pw1, pw1s = cmul_rows(pw, pws, x1[:, n_dbl:n_dbl + 1], x2[:, n_dbl:n_dbl + 1])

    lb_re, lb_im = x1[:, n_dbl:n_dbl + 1], sgn * x2[:, n_dbl:n_dbl + 1]
    den = lam_re * lam_re + lam_im * lam_im
    co_re = ((lb_re - 1.0) * lam_re + lb_im * lam_im) / den
    co_im = (lb_im * lam_re - (lb_re - 1.0) * lam_im) / den
    b_re, b_im = bt_ref[:, 0], bt_ref[:, 1]
    bb_re = co_re * b_re - co_im * b_im
    bb_im = co_re * b_im + co_im * b_re
    c1, c2 = c_ref[:, 0], sgn * c_ref[:, 1]

    ws = cmul3(bb_re, sgn * bb_im, pw, pws).astype(_bf16)
    wct = (cmul3(c1, c2, pw1, pw1s) * (-sgn)).astype(_bf16)
    e = cmul3(c1, c2, pw, pws)
    r0 = _per_group(functools.partial(_dot_nt, precision=jax.lax.Precision.HIGHEST),
                    jnp.where(lo, bb_re, -bb_im), e)
    lane_c = jax.lax.broadcasted_iota(jnp.int32, (H, CHUNK_LANES), 1)
    row_c = jax.lax.broadcasted_iota(jnp.int32, (H, CHUNK_LANES), 0)
    r0 = r0 + jnp.where(lane_c == row_c, d_ref[...], 0.0)
    for r in range(SUBLANES):
        rr = roll(r0, r * H, 1) if r else r0
        for a in range(T // SUBLANES):
            s = SUBLANES * a + r
            blk = roll(rr, a * LANES, 1) if a else rr
            toep_ref[:, (T - 1 - s) * H:(T - s) * H, :] = jnp.where(lane_c >= s * H, blk, 0.0).astype(_bf16)

    u = u_ref[...].astype(_bf16)
    rows = u.shape[1]
    s_loc = _per_group(_dot, u, ws)
    sm = _per_group(_dot, um_ref[...].astype(_bf16), ws)[:, 0:1, :]

    lvl_col = jax.lax.broadcasted_iota(jnp.int32, (SUBLANES, 1), 0)
    step, step_s = powers(jnp.left_shift(T, lvl_col).astype(_f32))
    a1 = jnp.where(lo, step, step_s)
    a2 = sgn * jnp.where(lo, step_s, step)

    def cmul(v, lvl):
        return a1[:, lvl:lvl + 1, :] * v + a2[:, lvl:lvl + 1, :] * roll(v, P, 1)

    cidx = jax.lax.broadcasted_iota(jnp.int32, (rows, STATE_LANES), 0) % chunks_per_seq
    first = cidx == 0
    st = s_loc + jnp.where(first, cmul(jnp.broadcast_to(sm, s_loc.shape), 0), 0.0)
    for lvl in range(levels):
        shift = 1 << lvl
        prev = jnp.where(cidx >= shift, roll(st, shift, 0), 0.0)
        st = st + cmul(prev, lvl)
    st_in = jnp.where(first, sm, roll(st, 1, 0))
    y = _per_group(_dot, u, toep_ref[...]) + _per_group(_dot_nt, st_in.astype(_bf16), wct)
    y_ref[...] = jax.nn.gelu(y)


def _ssm(u_t, um_t, lam2, ldt, bt2, c2, d_col, chunks_per_seq, weights):
    groups, rows, _ = u_t.shape
    levels = max(1, (chunks_per_seq - 1).bit_length())
    assert levels <= SUBLANES
    gps = SSM_GROUPS_PER_STEP
    steps = groups // gps
    per_group = lambda *s: pl.BlockSpec((gps,) + s, lambda g: (g,) + (0,) * len(s))
    row_block = lambda w: pl.BlockSpec((w.shape[0] // steps, w.shape[1]), lambda g: (g, 0))
    assert all(w.shape[0] % (steps * 2 * SUBLANES) == 0 for w in weights)
    outs = pl.pallas_call(
        functools.partial(_ssm_kernel, chunks_per_seq=chunks_per_seq, levels=levels,
                          groups_per_step=gps, n_cast=len(weights)),
        grid=(steps,),
        in_specs=[per_group(rows, CHUNK_LANES), per_group(um_t.shape[1], CHUNK_LANES),
                  per_group(2, STATE_LANES), per_group(1, 1),
                  per_group(2, SSM_GROUP, STATE_LANES), per_group(2, SSM_GROUP, STATE_LANES),
                  per_group(SSM_GROUP, 1)] + [row_block(w) for w in weights],
        out_specs=[per_group(rows, CHUNK_LANES)] + [row_block(w) for w in weights],
        out_shape=[jax.ShapeDtypeStruct((groups, rows, CHUNK_LANES), _f32)]
                  + [jax.ShapeDtypeStruct(w.shape, _bf16) for w in weights],
        scratch_shapes=[pltpu.VMEM((gps, CHUNK_LANES, CHUNK_LANES), _bf16)],
        compiler_params=pltpu.CompilerParams(dimension_semantics=("arbitrary",),
                                             vmem_limit_bytes=VMEM_LIMIT),
        name="ssm",
    )(u_t, um_t, lam2, ldt, bt2, c2, d_col, *weights)
    return outs[0], outs[1:]


def _conv_stage(z_ref, first_of_seq, zmeta_ref, dw_ref, dwb_ref, lng_ref, lnb_ref,
                zpad_ref, shift_ref, zs_ref):
    tile = z_ref.shape[0]
    padded = CONV_HALO + tile
    if first_of_seq is True:
        zpad_ref[0:CONV_HALO, :] = zmeta_ref[...]
    else:
        zpad_ref[0:CONV_HALO, :] = jnp.where(first_of_seq, zmeta_ref[...], zpad_ref[0:CONV_HALO, :])
    zpad_ref[CONV_HALO:, :] = z_ref[...].astype(_f32)
    base = CONV_HALO - (CONV_WIDTH - 1)
    convs = []
    dep = None
    for j in range(D_CONV // LANES):
        cols = slice(j * LANES, (j + 1) * LANES)
        zp = zpad_ref[:, cols]
        for r in range(1, SUBLANES):
            shift_ref[r - 1] = pltpu.roll(zp, padded - r, axis=0)
        pieces, acc = [], {}
        n_out, span = tile // SUBLANES, (base + CONV_WIDTH - 1) // SUBLANES
        for v in range(padded // SUBLANES):
            for r in range(SUBLANES):
                taps = [(a, a * SUBLANES + r - base) for a in range(span + 1)
                        if 0 <= a * SUBLANES + r - base < CONV_WIDTH and 0 <= v - a < n_out]
                if not taps:
                    continue
                rows = slice(v * SUBLANES, (v + 1) * SUBLANES)
                src = zpad_ref[rows, cols] if r == 0 else shift_ref[r - 1, rows, :]
                if dep is not None and r % CONV_LINK_EVERY == 0:
                    src = jnp.where(dep != dep, dep, src)
                for a, k in taps:
                    term = dw_ref[k:k + 1, cols] * src
                    acc[v - a] = term if v - a not in acc else acc[v - a] + term
                dep = acc[v - taps[-1][0]]
            if v - span in acc:
                pieces.append(acc.pop(v - span) + dwb_ref[:, cols])
        assert not acc and len(pieces) == n_out
        convs.append(jnp.concatenate(pieces, axis=0))
    conv = jnp.concatenate(convs, axis=1)
    zpad_ref[0:CONV_HALO, :] = zpad_ref[tile:tile + CONV_HALO, :]
    mu = jnp.mean(conv, axis=-1, keepdims=True)
    cen = conv - mu
    var = jnp.mean(cen * cen, axis=-1, keepdims=True)
    zn = cen * jax.lax.rsqrt(var + EPS) * lng_ref[...] + lnb_ref[...]
    zs_ref[...] = (zn * _sigmoid(zn)).astype(_bf16)


def _token_major_stage(yt_ref, ys_ref):
    n_chunks = yt_ref.shape[1]
    for j in range(D_SSM // LANES):
        for q in range(SSM_CHUNK // SUBLANES):
            blks = [yt_ref[GROUPS_PER_VREG * j + a, :, q * LANES:(q + 1) * LANES]
                    for a in range(GROUPS_PER_VREG)]
            for r, rows in enumerate(_block_transpose8(blks)):
                ys_ref[j, pl.ds(SUBLANES * q + r, n_chunks, stride=CHUNK_PITCH), :] = rows


def _back_kernel(h1_ref, z0_ref, zn_ref, yt0_ref, ytn_ref, gate_ref, zmeta_ref, dw_ref, dwb_ref,
                 lng_ref, lnb_ref, wcp_ref, wv_ref, wg_ref, wout_ref, g3_ref, w1_ref, w3_ref, w2_ref,
                 gf_ref, out_ref, zpad_ref, shift_ref, zs_ref, ys_ref, *, tiles_per_seq):
    n = pl.program_id(0)
    conv_args = (zmeta_ref, dw_ref, dwb_ref, lng_ref, lnb_ref, zpad_ref, shift_ref, zs_ref)

    @pl.when(n == 0)
    def _():
        _conv_stage(z0_ref, True, *conv_args)
        _token_major_stage(yt0_ref, ys_ref)

    slab = h1_ref.shape[0] // MIX_SLABS
    h2_parts, nrm_parts = [], []
    for s in range(MIX_SLABS):
        rows = slice(s * slab, (s + 1) * slab)
        zs = zs_ref[rows, :]
        ys = jnp.concatenate(
            [jnp.concatenate([ys_ref[j, c * CHUNK_PITCH:c * CHUNK_PITCH + SSM_CHUNK, :]
                              for c in range(s * slab // SSM_CHUNK, (s + 1) * slab // SSM_CHUNK)], axis=0)
             for j in range(D_SSM // LANES)], axis=1).astype(_bf16)
        y_conv = _dot(zs, wcp_ref[...])
        y_ssm = _dot(ys, wv_ref[...]) * _sigmoid(_dot(ys, wg_ref[...]))
        mix = (gate_ref[rows, 0:D_MODEL].astype(_f32) * y_conv
               + gate_ref[rows, D_MODEL:].astype(_f32) * y_ssm).astype(_bf16)
        h2_parts.append(h1_ref[rows, :] + _dot(mix, wout_ref[...]))
        nrm_parts.append(_rms_norm(h2_parts[-1], g3_ref[...]).astype(_bf16))
    h2 = jnp.concatenate(h2_parts, axis=0)
    nrm = jnp.concatenate(nrm_parts, axis=0)

    _conv_stage(zn_ref, (n + 1) % tiles_per_seq == 0, *conv_args)
    _token_major_stage(ytn_ref, ys_ref)

    h3 = h2 + 0.5 * _swiglu(nrm, w1_ref, w3_ref, w2_ref)
    out_ref[...] = _rms_norm(h3, gf_ref[...])


def _back(h1, z, y_t, gate, zmeta, dw, dwb, lng, lnb, wcp, wv, wg, wout, g3, w1, w3, w2, gf,
          batch, seq, tile):
    tiles = batch * seq // tile
    nxt = lambda n: jnp.minimum(n + 1, tiles - 1)
    row = lambda w: pl.BlockSpec((tile, w), lambda n: (n, 0))
    yt_block = (N_SSM_GROUPS, tile // SSM_CHUNK, CHUNK_LANES)
    return pl.pallas_call(
        functools.partial(_back_kernel, tiles_per_seq=seq // tile),
        grid=(tiles,),
        in_specs=[row(D_MODEL),
                  pl.BlockSpec((tile, D_CONV), lambda n: (0, 0), pipeline_mode=pl.Buffered(1)),
                  pl.BlockSpec((tile, D_CONV), lambda n: (nxt(n), 0)),
                  pl.BlockSpec(yt_block, lambda n: (0, 0, 0), pipeline_mode=pl.Buffered(1)),
                  pl.BlockSpec(yt_block, lambda n: (0, nxt(n), 0)),
                  row(2 * D_MODEL),
                  _resident((CONV_HALO, D_CONV)), _resident((CONV_WIDTH, D_CONV)),
                  _resident((1, D_CONV)), _resident((1, D_CONV)), _resident((1, D_CONV)),
                  _resident((D_CONV, D_MODEL)), _resident((D_SSM, D_MODEL)),
                  _resident((D_SSM, D_MODEL)), _resident((D_MODEL, D_MODEL)),
                  _resident((1, D_MODEL)), _resident((D_MODEL, D_FF)), _resident((D_MODEL, D_FF)),
                  _resident((D_FF, D_MODEL)), _resident((1, D_MODEL))],
        out_specs=row(D_MODEL),
        out_shape=jax.ShapeDtypeStruct((batch * seq, D_MODEL), _f32),
        scratch_shapes=[pltpu.VMEM((CONV_HALO + tile, D_CONV), _f32),
                        pltpu.VMEM((SUBLANES - 1, CONV_HALO + tile, LANES), _f32),
                        pltpu.VMEM((tile, D_CONV), _bf16),
                        pltpu.VMEM((D_SSM // LANES, tile // SSM_CHUNK * CHUNK_PITCH, LANES), _f32)],
        compiler_params=pltpu.CompilerParams(dimension_semantics=("arbitrary",),
                                             vmem_limit_bytes=VMEM_LIMIT),
        name="back",
    )(h1, z, z, y_t, y_t, gate, zmeta, dw, dwb, lng, lnb, wcp, wv, wg, wout, g3, w1, w3, w2, gf)


def kernel(x, meta_tokens, ffn1_norm, ffn1_w1, ffn1_w3, ffn1_w2, mix_norm, w_in, b_gate, conv_dw, conv_dw_b, conv_ln_g, conv_ln_b, conv_proj, ssm_lam_re, ssm_lam_im, ssm_log_dt, ssm_b_re, ssm_b_im, ssm_c_re, ssm_c_im, ssm_d, ssm_w_v, ssm_w_g, w_out, ffn2_norm, ffn2_w1, ffn2_w3, ffn2_w2, final_norm):
    batch, seq, _ = x.shape
    assert ffn1_norm.shape[0] == 1 and seq % ROW_TILE == 0 and ROW_TILE % (SUBLANES * SSM_CHUNK) == 0
    assert N_META <= SSM_CHUNK and N_META <= CONV_HALO
    T, H, G = SSM_CHUNK, SSM_GROUP, N_SSM_GROUPS
    row = lambda v: v.reshape(1, -1)
    dup = lambda a: jnp.concatenate([a, a], axis=-1)

    meta_chunk = jnp.zeros((T, D_MODEL), _f32).at[T - N_META:].set(meta_tokens)
    h1, z, u_t, gate, z_m, u_m = _front(
        x.reshape(batch * seq, D_MODEL), meta_chunk, row(ffn1_norm[0]), row(mix_norm[0]),
        row(b_gate[0]), ffn1_w1[0], ffn1_w3[0], ffn1_w2[0], w_in[0], ROW_TILE)
    um_t = u_m[::-1].reshape(T, G, H).transpose(1, 0, 2).reshape(G, 1, T * H)
    um_t = jnp.concatenate([um_t, jnp.zeros((G, 2 * SUBLANES - 1, T * H), _f32)], axis=1)

    lam2 = jnp.stack([dup(ssm_lam_re[0]), dup(ssm_lam_im[0])], axis=1)
    bt2 = jnp.stack([dup(ssm_b_re[0].transpose(0, 2, 1)), dup(ssm_b_im[0].transpose(0, 2, 1))], axis=1)
    c2 = jnp.stack([dup(ssm_c_re[0]), dup(ssm_c_im[0])], axis=1)
    back_w = (conv_proj[0], ssm_w_v[0], ssm_w_g[0], w_out[0], ffn2_w1[0], ffn2_w3[0], ffn2_w2[0])
    y_t, (wcp, wv, wg, wo, w1b, w3b, w2b) = _ssm(
        u_t, um_t, lam2, ssm_log_dt[0].reshape(G, 1, 1), bt2, c2, ssm_d[0].reshape(G, H, 1),
        seq // T, back_w)

    z_halo = z_m[T - CONV_HALO:].astype(_f32)
    out = _back(h1, z, y_t, gate, z_halo, conv_dw[0], row(conv_dw_b[0]), row(conv_ln_g[0]),
                row(conv_ln_b[0]), wcp, wv, wg, wo, row(ffn2_norm[0]), w1b, w3b, w2b,
                row(final_norm), batch, seq, ROW_TILE)
    return out.reshape(batch, seq, D_MODEL)
```
